```python
import math
import jax
import jax.numpy as jnp
from jax import lax
import numpy as np

D_MODEL = 1024
BATCH = 16
SEQ = 2048
DEPTH = 2

HEAD_DIM = 64
BLOCK = 128
NSA_HEADS = 4
NSA_KV_HEADS = 1
CMP_LEN = 32
CMP_STRIDE = 16
SEL_LEN = 64
N_SEL = 16
NSA_WINDOW = 512
SEL_Q_BLOCK = 64
SWA_HEADS = 4
SWA_KV_HEADS = 2
SWA_WINDOW = 128
DIL_PATTERNS = ((128, 1), (512, 4), (2048, 16))
DIL_HEADS_PER_GROUP = 4
DIL_Q_HEADS = len(DIL_PATTERNS) * DIL_HEADS_PER_GROUP
DIL_KV_HEADS = 4
SB_HEADS = 4
N_BRANCHES = 4
BRANCH_WIDTH = 4 * HEAD_DIM
NUM_BUCKETS = 32
MAX_DISTANCE = 2048
N_BIAS_HEADS = NSA_HEADS + SWA_HEADS + DIL_Q_HEADS
D_FF_DENSE = 2816
N_EXPERTS = 8
TOP_K = 2
D_FF_EXPERT = 3584
N_DENSE_LAYERS = (DEPTH + 1) // 2
N_MOE_LAYERS = DEPTH // 2
ALPHA = (2 * DEPTH) ** 0.25
BETA = (8 * DEPTH) ** -0.25
LN_EPS = 1e-5
NEG_INF = -1e30
TINY = 1e-30
FORCE_SCORE = 1e4

IN_SPLITS = (
    ('nsa_q', NSA_HEADS * HEAD_DIM), ('nsa_k_cmp', NSA_KV_HEADS * HEAD_DIM), ('nsa_v_cmp', NSA_KV_HEADS * HEAD_DIM),
    ('nsa_k_sel', NSA_KV_HEADS * HEAD_DIM), ('nsa_v_sel', NSA_KV_HEADS * HEAD_DIM),
    ('nsa_k_win', NSA_KV_HEADS * HEAD_DIM), ('nsa_v_win', NSA_KV_HEADS * HEAD_DIM), ('nsa_gate', NSA_HEADS * 3),
    ('swa_q', SWA_HEADS * HEAD_DIM), ('swa_k', SWA_KV_HEADS * HEAD_DIM), ('swa_v', SWA_KV_HEADS * HEAD_DIM),
    ('dil_q', DIL_Q_HEADS * HEAD_DIM), ('dil_k', DIL_KV_HEADS * HEAD_DIM), ('dil_v', DIL_KV_HEADS * HEAD_DIM),
    ('sb_q', SB_HEADS * HEAD_DIM), ('sb_k', SB_HEADS * HEAD_DIM), ('sb_v', SB_HEADS * HEAD_DIM),
    ('merge_gate', N_BRANCHES * D_MODEL),
)
D_IN = sum(sz for _, sz in IN_SPLITS)

kernel_name = 'hybrid_nsa_swa_dilated_stickbreak_moe_block'


def split_cols(z):
    out = {}
    off = 0
    for name, size in IN_SPLITS:
        out[name] = z[..., off:off + size]
        off += size
    return out


def layer_norm(x):
    x32 = x.astype(jnp.float32)
    mu = x32.mean(-1, keepdims=True)
    var = jnp.square(x32 - mu).mean(-1, keepdims=True)
    return ((x32 - mu) * lax.rsqrt(var + LN_EPS)).astype(x.dtype)


def t5_bucket(dist):
    dist = jnp.maximum(dist, 0)
    max_exact = NUM_BUCKETS // 2
    d_f = jnp.maximum(dist, 1).astype(jnp.float32)
    large = max_exact + (jnp.log(d_f / max_exact) / math.log(MAX_DISTANCE / max_exact)
                         * (NUM_BUCKETS - max_exact)).astype(jnp.int32)
    large = jnp.minimum(large, NUM_BUCKETS - 1)
    return jnp.where(dist < max_exact, dist, large)


def banded_attention(q, k, v, n_back, bias_tbl, dist_mult, sink=None):
    n, length, h, dh = q.shape
    g = k.shape[2]
    r = h // g
    blk = min(BLOCK, length)
    nb = -(-length // blk)
    lp = nb * blk
    n_prev = -(-n_back // blk)
    span = (n_prev + 1) * blk
    pad_end = lp - length
    qp = jnp.pad(q, ((0, 0), (0, pad_end), (0, 0), (0, 0))).reshape(n, nb, blk, g, r, dh)
    kv_pad = ((0, 0), (n_prev * blk, pad_end), (0, 0), (0, 0))
    kp = jnp.pad(k, kv_pad)
    vp = jnp.pad(v, kv_pad)

    def band(a):
        return jnp.concatenate(
            [a[:, j * blk:(j + nb) * blk].reshape(n, nb, blk, g, dh) for j in range(n_prev + 1)], axis=2)

    kb, vb = band(kp), band(vp)
    dist = np.arange(blk)[:, None] + n_prev * blk - np.arange(span)[None, :]
    local_ok = (dist >= 0) & (dist <= n_back)
    front_ok = (np.arange(nb)[:, None] - n_prev) * blk + np.arange(span)[None, :] >= 0
    mask = local_ok[None] & front_ok[:, None, :]
    bias = bias_tbl[t5_bucket(jnp.asarray(dist * dist_mult))]
    bias = jnp.transpose(bias, (2, 0, 1)).reshape(g, r, blk, span).astype(jnp.float32)
    s = jnp.einsum('nbqgrd,nbkgd->nbgrqk', qp, kb).astype(jnp.float32) / math.sqrt(dh)
    s = jnp.where(mask[None, :, None, None], s + bias, NEG_INF)
    m = s.max(-1)
    if sink is not None:
        sink_f = sink.astype(jnp.float32).reshape(g, r, 1)
        m = jnp.maximum(m, sink_f)
    p = jnp.exp(s - m[..., None])
    den = p.sum(-1)
    if sink is not None:
        den = den + jnp.exp(sink_f - m)
    o = jnp.einsum('nbgrqk,nbkgd->nbqgrd', p.astype(v.dtype), vb)
    o = o / jnp.transpose(den, (0, 1, 4, 2, 3))[..., None].astype(v.dtype)
    o = o.reshape(n, lp, h, dh)[:, :length]
    lse = jnp.transpose(m + jnp.log(den), (0, 1, 4, 2, 3)).reshape(n, lp, h)[:, :length]
    return o, lse


def compress_blocks(a, pe, w1, w2):
    b, s, g, dh = a.shape
    ratio = CMP_LEN // CMP_STRIDE
    n_chunk = s // CMP_STRIDE
    n_cmp = n_chunk - ratio + 1
    chunks = a.reshape(b, n_chunk, CMP_STRIDE, g, dh)
    blocks = jnp.concatenate([chunks[:, j:j + n_cmp] for j in range(ratio)], axis=2)
    blocks = blocks + pe[None, None, :, None, :]
    flat = jnp.transpose(blocks, (0, 1, 3, 2, 4)).reshape(b, n_cmp, g, CMP_LEN * dh)
    return jax.nn.gelu(flat @ w1) @ w2


def nsa_mixer(q, k_cmp, v_cmp, k_sel, v_sel, k_win, v_win, gate_logits, cmp_pe, cmp_w1, cmp_w2, bias_tbl):
    b, s, h, dh = q.shape
    g = k_cmp.shape[2]
    r = h // g
    qg = q.reshape(b, s, g, r, dh)
    pos = jnp.arange(s)
    kc = compress_blocks(k_cmp, cmp_pe[0], cmp_w1[0], cmp_w2[0])
    vc = compress_blocks(v_cmp, cmp_pe[1], cmp_w1[1], cmp_w2[1])
    n_cmp = kc.shape[1]
    cmp_end = jnp.arange(n_cmp) * CMP_STRIDE + CMP_LEN - 1
    cmp_ok = (cmp_end[None, :] <= pos[:, None])[None, :, None, None, :]
    sc = jnp.einsum('btgrd,bigd->btgri', qg, kc).astype(jnp.float32) / math.sqrt(dh)
    sc = jnp.where(cmp_ok, sc, NEG_INF)
    p = jnp.exp(sc - sc.max(-1, keepdims=True)) * cmp_ok
    p = p / jnp.maximum(p.sum(-1, keepdims=True), TINY)
    o_cmp = jnp.einsum('btgri,bigd->btgrd', p.astype(vc.dtype), vc).reshape(b, s, h, dh)
    n_blk = s // SEL_LEN
    n_blk_pad = max(n_blk, N_SEL)
    ci = np.arange(n_cmp)[:, None]
    sj = np.arange(n_blk_pad)[None, :]
    overlap = ((ci * CMP_STRIDE + CMP_LEN - 1 >= sj * SEL_LEN) & (ci * CMP_STRIDE < (sj + 1) * SEL_LEN)
               & (sj < n_blk)).astype(np.float32)
    blk_score = jnp.einsum('btgi,ij->btgj', p.sum(3), overlap)
    cur = pos // SEL_LEN
    jj = jnp.arange(n_blk_pad)
    forced = (jj[None] == 0) | (jj[None] == cur[:, None]) | (jj[None] == cur[:, None] - 1)
    future = jj[None] > cur[:, None]
    blk_score = jnp.where(forced[None, :, None], FORCE_SCORE,
                          jnp.where(future[None, :, None], NEG_INF, blk_score))
    _, sel_idx = lax.top_k(blk_score, N_SEL)
    pad_len = n_blk_pad * SEL_LEN - s

    def to_blocks(a):
        a = jnp.pad(a, ((0, 0), (0, pad_len), (0, 0), (0, 0)))
        return jnp.transpose(a.reshape(b, n_blk_pad, SEL_LEN, g, dh), (0, 3, 1, 2, 4))

    kb, vb = to_blocks(k_sel), to_blocks(v_sel)
    n_chunks = s // SEL_Q_BLOCK
    q_c = jnp.transpose(qg.reshape(b, n_chunks, SEL_Q_BLOCK, g, r, dh), (1, 0, 2, 3, 4, 5))
    idx_c = jnp.transpose(sel_idx.reshape(b, n_chunks, SEL_Q_BLOCK, g, N_SEL), (1, 0, 3, 2, 4))
    pos_c = pos.reshape(n_chunks, SEL_Q_BLOCK)
    gather = jax.vmap(jax.vmap(lambda tbl, ix: tbl[ix]))
    tbl_g = jnp.transpose(bias_tbl.reshape(NUM_BUCKETS, g, r), (1, 0, 2))
    lookup = jax.vmap(lambda tb, bk: tb[bk], in_axes=(0, 1), out_axes=1)

    def sel_chunk(args):
        qq, ix, qpos = args
        kg = gather(kb, ix)
        vg = gather(vb, ix)
        kpos = ix[..., None] * SEL_LEN + jnp.arange(SEL_LEN)
        dist = qpos[None, None, :, None, None] - kpos
        ok = dist >= 0
        bias = jnp.transpose(lookup(tbl_g, t5_bucket(dist)), (0, 1, 2, 5, 3, 4)).astype(jnp.float32)
        sc_s = jnp.einsum('btgrd,bgtnsd->bgtrns', qq, kg).astype(jnp.float32) / math.sqrt(dh) + bias
        sc_s = jnp.where(ok[:, :, :, None], sc_s, NEG_INF)
        tq = qq.shape[1]
        p_s = jax.nn.softmax(sc_s.reshape(b, g, tq, r, N_SEL * SEL_LEN), axis=-1)
        return jnp.einsum('bgtrk,bgtkd->btgrd', p_s.astype(vg.dtype), vg.reshape(b, g, tq, N_SEL * SEL_LEN, dh))

    o_sel = lax.map(sel_chunk, (q_c, idx_c, pos_c))
    o_sel = jnp.transpose(o_sel, (1, 0, 2, 3, 4, 5)).reshape(b, s, h, dh)
    o_win, _ = banded_attention(q, k_win, v_win, NSA_WINDOW - 1, bias_tbl, 1)
    gt = jax.nn.sigmoid(gate_logits.reshape(b, s, h, 3))
    o = gt[..., 0:1] * o_cmp + gt[..., 1:2] * o_sel + gt[..., 2:3] * o_win
    return o.reshape(b, s, h * dh)


def dilated_mixer(q, k, v, bias_tbl):
    b, s, _, dh = q.shape
    hh = DIL_HEADS_PER_GROUP
    outs = []
    lses = []
    for gi, (win, dil) in enumerate(DIL_PATTERNS):
        L = s // dil

        def to_residue(a):
            return jnp.transpose(a.reshape(b, L, dil, a.shape[2], dh), (0, 2, 1, 3, 4)).reshape(b * dil, L, a.shape[2], dh)

        o, lse = banded_attention(to_residue(q[:, :, gi * hh:(gi + 1) * hh]), to_residue(k), to_residue(v),
                                  win // dil, bias_tbl[:, gi * hh:(gi + 1) * hh], dil)
        outs.append(jnp.transpose(o.reshape(b, dil, L, hh, dh), (0, 2, 1, 3, 4)).reshape(b, s, hh, dh))
        lses.append(jnp.transpose(lse.reshape(b, dil, L, hh), (0, 2, 1, 3)).reshape(b, s, hh))
    wts = jax.nn.softmax(jnp.stack(lses, 0), axis=0)
    o = jnp.sum(wts[..., None].astype(q.dtype) * jnp.stack(outs, 0), axis=0)
    return o.reshape(b, s, hh * dh)


def stick_breaking(q, k, v):
    b, s, h, dh = q.shape
    blk = min(BLOCK, s)
    nb = s // blk
    q_c = jnp.transpose(q.reshape(b, nb, blk, h, dh), (1, 0, 2, 3, 4))
    pos_c = jnp.arange(s).reshape(nb, blk)
    kpos = jnp.arange(s)

    def block_fn(args):
        qq, qpos = args
        z = jnp.einsum('bqhd,bkhd->bhqk', qq, k).astype(jnp.float32) / math.sqrt(dh)
        strict = (kpos[None, :] < qpos[:, None])[None, None]
        log_fail = jnp.where(strict, jax.nn.log_sigmoid(-z), 0.0)
        after = lax.cumsum(log_fail, axis=3, reverse=True) - log_fail
        a = jnp.where(strict, jnp.exp(jax.nn.log_sigmoid(z) + after), 0.0)
        return jnp.einsum('bhqk,bkhd->bqhd', a.astype(v.dtype), v)

    o = lax.map(block_fn, (q_c, pos_c))
    return jnp.transpose(o, (1, 0, 2, 3, 4)).reshape(b, s, h * dh)


def mixer_sublayer(h, w_in, w_branch, w_out, cmp_pe, cmp_w1, cmp_w2, sinks, rel_bias):
    b, s, _ = h.shape
    cols = split_cols(h @ w_in)

    def heads(name, n):
        return cols[name].reshape(b, s, n, HEAD_DIM)

    o_a = nsa_mixer(heads('nsa_q', NSA_HEADS), heads('nsa_k_cmp', NSA_KV_HEADS), heads('nsa_v_cmp', NSA_KV_HEADS),
                    heads('nsa_k_sel', NSA_KV_HEADS), heads('nsa_v_sel', NSA_KV_HEADS),
                    heads('nsa_k_win', NSA_KV_HEADS), heads('nsa_v_win', NSA_KV_HEADS), cols['nsa_gate'],
                    cmp_pe, cmp_w1, cmp_w2, rel_bias[:, :NSA_HEADS])
    o_b, _ = banded_attention(heads('swa_q', SWA_HEADS), heads('swa_k', SWA_KV_HEADS), heads('swa_v', SWA_KV_HEADS),
                              SWA_WINDOW - 1, rel_bias[:, NSA_HEADS:NSA_HEADS + SWA_HEADS], 1, sinks)
    o_b = o_b.reshape(b, s, BRANCH_WIDTH)
    o_c = dilated_mixer(heads('dil_q', DIL_Q_HEADS), heads('dil_k', DIL_KV_HEADS), heads('dil_v', DIL_KV_HEADS),
                        rel_bias[:, NSA_HEADS + SWA_HEADS:])
    o_d = stick_breaking(heads('sb_q', SB_HEADS), heads('sb_k', SB_HEADS), heads('sb_v', SB_HEADS))
    gates = jax.nn.sigmoid(cols['merge_gate'].reshape(b, s, N_BRANCHES, D_MODEL))
    merged = (gates[:, :, 0] * (o_a @ w_branch[0]) + gates[:, :, 1] * (o_b @ w_branch[1])
              + gates[:, :, 2] * (o_c @ w_branch[2]) + gates[:, :, 3] * (o_d @ w_branch[3]))
    return merged @ w_out


def swiglu(h, w_gate, w_up, w_down):
    return (jax.nn.silu(h @ w_gate) * (h @ w_up)) @ w_down


def moe_ffn(h, w_router, e_gate, e_up, e_down):
    logits = (h @ w_router).astype(jnp.float32)
    top_val, top_idx = lax.top_k(logits, TOP_K)
    top_w = jax.nn.softmax(top_val, axis=-1)
    gates = jnp.sum(jax.nn.one_hot(top_idx, N_EXPERTS, dtype=jnp.float32) * top_w[..., None], axis=-2)
    y = jnp.zeros_like(h)
    for e in range(N_EXPERTS):
        y = y + gates[..., e:e + 1].astype(h.dtype) * swiglu(h, e_gate[e], e_up[e], e_down[e])
    return y


def setup_inputs(seed: int = 0) -> dict:
    key = jax.random.key(seed)
    ks = jax.random.split(key, 21)

    def nrm(k, shape, scale):
        return jax.random.normal(k, shape, jnp.float32) * scale

    return {
        'x': nrm(ks[0], (BATCH, SEQ, D_MODEL), 1.0),
        'c': nrm(ks[1], (BATCH, D_MODEL), 1.0),
        'rel_bias': nrm(ks[2], (NUM_BUCKETS, N_BIAS_HEADS), 0.5),
        'w_ada': nrm(ks[3], (DEPTH, D_MODEL, 6 * D_MODEL), 0.5 * D_MODEL ** -0.5),
        'b_ada': nrm(ks[4], (DEPTH, 6 * D_MODEL), 0.01),
        'w_in': nrm(ks[5], (DEPTH, D_MODEL, D_IN), D_MODEL ** -0.5),
        'w_branch': nrm(ks[6], (DEPTH, N_BRANCHES, BRANCH_WIDTH, D_MODEL), BETA * BRANCH_WIDTH ** -0.5),
        'w_out': nrm(ks[7], (DEPTH, D_MODEL, D_MODEL), BETA * D_MODEL ** -0.5),
        'cmp_pe': nrm(ks[8], (DEPTH, 2, CMP_LEN, HEAD_DIM), 0.1),
        'cmp_w1': nrm(ks[9], (DEPTH, 2, CMP_LEN * HEAD_DIM, HEAD_DIM), (CMP_LEN * HEAD_DIM) ** -0.5),
        'cmp_w2': nrm(ks[10], (DEPTH, 2, HEAD_DIM, HEAD_DIM), HEAD_DIM ** -0.5),
        'swa_sinks': nrm(ks[11], (DEPTH, SWA_HEADS), 1.0),
        'ln_g': 1.0 + nrm(ks[12], (DEPTH, 2, D_MODEL), 0.02),
        'ln_b': nrm(ks[13], (DEPTH, 2, D_MODEL), 0.02),
        'ffn_w_gate': nrm(ks[14], (N_DENSE_LAYERS, D_MODEL, D_FF_DENSE), D_MODEL ** -0.5),
        'ffn_w_up': nrm(ks[15], (N_DENSE_LAYERS, D_MODEL, D_FF_DENSE), D_MODEL ** -0.5),
        'ffn_w_down': nrm(ks[16], (N_DENSE_LAYERS, D_FF_DENSE, D_MODEL), BETA * D_FF_DENSE ** -0.5),
        'moe_router': nrm(ks[17], (N_MOE_LAYERS, D_MODEL, N_EXPERTS), D_MODEL ** -0.5),
        'moe_w_gate': nrm(ks[18], (N_MOE_LAYERS, N_EXPERTS, D_MODEL, D_FF_EXPERT), D_MODEL ** -0.5),
        'moe_w_up': nrm(ks[19], (N_MOE_LAYERS, N_EXPERTS, D_MODEL, D_FF_EXPERT), D_MODEL ** -0.5),
        'moe_w_down': nrm(ks[20], (N_MOE_LAYERS, N_EXPERTS, D_FF_EXPERT, D_MODEL), BETA * D_FF_EXPERT ** -0.5),
    }


def reference(x, c, rel_bias, w_ada, b_ada, w_in, w_branch, w_out, cmp_pe, cmp_w1, cmp_w2, swa_sinks,
              ln_g, ln_b, ffn_w_gate, ffn_w_up, ffn_w_down, moe_router, moe_w_gate, moe_w_up, moe_w_down):
    for layer in range(DEPTH):
        mod = jax.nn.silu(c) @ w_ada[layer] + b_ada[layer]
        sh1, sc1, g1, sh2, sc2, g2 = jnp.split(mod, 6, axis=-1)
        h = layer_norm(x) * (1.0 + sc1[:, None]) + sh1[:, None]
        y = mixer_sublayer(h, w_in[layer], w_branch[layer], w_out[layer], cmp_pe[layer], cmp_w1[layer],
                           cmp_w2[layer], swa_sinks[layer], rel_bias)
        x = layer_norm(ALPHA * x + g1[:, None] * y) * ln_g[layer, 0] + ln_b[layer, 0]
        h = layer_norm(x) * (1.0 + sc2[:, None]) + sh2[:, None]
        if layer % 2 == 0:
            i = layer // 2
            y = swiglu(h, ffn_w_gate[i], ffn_w_up[i], ffn_w_down[i])
        else:
            i = layer // 2
            y = moe_ffn(h, moe_router[i], moe_w_gate[i], moe_w_up[i], moe_w_down[i])
        x = layer_norm(ALPHA * x + g2[:, None] * y) * ln_g[layer, 1] + ln_b[layer, 1]
    return x
```

```python
import functools
import math

import numpy as np
import jax
import jax.numpy as jnp
from jax import lax
from jax.experimental import pallas as pl
from jax.experimental.pallas import tpu as pltpu

F32 = jnp.float32
BF16 = jnp.bfloat16
HIGHEST = lax.Precision.HIGHEST

HEAD_DIM = 64
BLK = 128
NSA_HEADS = 4
CMP_LEN = 32
CMP_STRIDE = 16
SEL_LEN = 64
N_SEL = 16
NSA_WINDOW = 512
SWA_HEADS = 4
SWA_KV_HEADS = 2
SWA_WINDOW = 128
DIL_PATTERNS = ((128, 1), (512, 4), (2048, 16))
DIL_HEADS_PER_GROUP = 4
DIL_KV_HEADS = 4
SB_HEADS = 4
N_BRANCHES = 4
BRANCH_WIDTH = 4 * HEAD_DIM
NUM_BUCKETS = 32
MAX_DISTANCE = 2048
N_EXPERTS = 8
LN_EPS = 1e-5
NEG_INF = -1e30
TINY = 1e-30
FORCE_SCORE = 1e4

GATE_COLS = NSA_HEADS * 3
GATE_PAD = 128 - GATE_COLS
COL_MG = 0
COL_NSA_Q = 4096
COL_NSA_CMP = 4352
COL_NSA_SEL = 4480
COL_NSA_WIN = 4608
COL_NSA_GATE = 4736
COL_SWA_Q = 4864
COL_SWA_K = 5120
COL_SWA_V = 5248
COL_DIL_Q = 5376
COL_DIL_K = 6144
COL_DIL_V = 6400
COL_SB_Q = 6656
COL_SB_K = 6912
COL_SB_V = 7168
Q_COL_RANGES = ((COL_NSA_Q, 256), (COL_SWA_Q, 256), (COL_DIL_Q, 768), (COL_SB_Q, 256))

VMEM_LIMIT = 56 * 1024 * 1024


def _params(sem):
    return pltpu.CompilerParams(dimension_semantics=sem, vmem_limit_bytes=VMEM_LIMIT)


def _ln(x):
    mu = jnp.mean(x, axis=-1, keepdims=True)
    xc = x - mu
    var = jnp.mean(xc * xc, axis=-1, keepdims=True)
    return xc * lax.rsqrt(var + LN_EPS)


def _dot_nt(a, b):
    return lax.dot_general(a, b, (((1,), (1,)), ((), ())), preferred_element_type=F32)


def _ada_kernel(c_ref, w_ref, b_ref, o_ref):
    c = c_ref[...]
    s = c * jax.nn.sigmoid(c)
    o_ref[...] = jnp.dot(s, w_ref[...], preferred_element_type=F32, precision=HIGHEST) + b_ref[...]


def _ada(c, w, b):
    bsz, d = c.shape
    n = w.shape[1]
    tn = 1024
    return pl.pallas_call(
        _ada_kernel,
        grid=(n // tn,),
        in_specs=[pl.BlockSpec((bsz, d), lambda j: (0, 0)),
                  pl.BlockSpec((d, tn), lambda j: (0, j)),
                  pl.BlockSpec((1, tn), lambda j: (0, j))],
        out_specs=pl.BlockSpec((bsz, tn), lambda j: (0, j)),
        out_shape=jax.ShapeDtypeStruct((bsz, n), F32),
        compiler_params=_params(("parallel",)),
        name="ada",
    )(c, w, b.reshape(1, n))


def _inproj_kernel(x_ref, sc_ref, sh_ref, w_ref, o_ref):
    h = _ln(x_ref[...]) * (1.0 + sc_ref[...]) + sh_ref[...]
    o_ref[...] = jnp.dot(h.astype(BF16), w_ref[...], preferred_element_type=F32).astype(o_ref.dtype)


def _inproj(x2, mod4, w, seq, *, sc_idx, sh_idx, tm=512):
    m, d = x2.shape
    n = w.shape[1]
    tn = n // 2
    per = seq // tm
    return pl.pallas_call(
        _inproj_kernel,
        grid=(n // tn, m // tm),
        in_specs=[pl.BlockSpec((tm, d), lambda j, i: (i, 0)),
                  pl.BlockSpec((None, None, 1, d), lambda j, i: (i // per, sc_idx, 0, 0)),
                  pl.BlockSpec((None, None, 1, d), lambda j, i: (i // per, sh_idx, 0, 0)),
                  pl.BlockSpec((d, tn), lambda j, i: (0, j))],
        out_specs=pl.BlockSpec((tm, tn), lambda j, i: (i, j)),
        out_shape=jax.ShapeDtypeStruct((m, n), BF16),
        compiler_params=_params(("parallel", "parallel")),
        name="inproj",
    )(x2, mod4, mod4, w)


def _np_bucket(dist):
    dist = np.maximum(dist, 0)
    max_exact = NUM_BUCKETS // 2
    d_f = np.maximum(dist, 1).astype(np.float32)
    large = max_exact + (np.log(d_f / np.float32(max_exact)) / np.float32(math.log(MAX_DISTANCE / max_exact))
                         * np.float32(NUM_BUCKETS - max_exact)).astype(np.int32)
    large = np.minimum(large, NUM_BUCKETS - 1)
    return np.where(dist < max_exact, dist, large).astype(np.int32)


def _bucket_tiles(n_off, n_back, dist_mult):
    a = np.arange(BLK)[:, None]
    b = np.arange(BLK)[None, :]
    tiles = []
    for o in range(n_off):
        dist = o * BLK + a - b
        ok = (dist >= 0) & (dist <= n_back)
        tiles.append(np.where(ok, _np_bucket(dist * dist_mult), -1))
    return np.stack(tiles).astype(np.int32)


def _bias_kernel(tbl_ref, bk_ref, o_ref, *, h0):
    h = pl.program_id(0) + h0
    b = bk_ref[...]
    out = jnp.where(b < 0, NEG_INF, 0.0).astype(F32)
    for k in range(NUM_BUCKETS):
        out = jnp.where(b == k, tbl_ref[k, h], out)
    o_ref[...] = out


def _bias_tiles(rel_bias, buckets, h0, n_heads):
    n_off = buckets.shape[0]
    return pl.pallas_call(
        functools.partial(_bias_kernel, h0=h0),
        grid=(n_heads,),
        in_specs=[pl.BlockSpec(memory_space=pltpu.SMEM),
                  pl.BlockSpec((n_off, BLK, BLK), lambda h: (0, 0, 0))],
        out_specs=pl.BlockSpec((None, n_off, BLK, BLK), lambda h: (h, 0, 0, 0)),
        out_shape=jax.ShapeDtypeStruct((n_heads, n_off, BLK, BLK), F32),
        compiler_params=_params(("parallel",)),
        name="bias_tiles",
    )(rel_bias, jnp.asarray(buckets))


def _band_kernel(*refs, n_heads, n_kv, n_off, k_off, v_off, has_sink, want_lse):
    q_ref, k_ref, v_ref, bias_ref = refs[:4]
    pos = 4
    sink_ref = None
    if has_sink:
        sink_ref = refs[pos]
        pos += 1
    o_ref = refs[pos]
    lse_ref = refs[pos + 1] if want_lse else None
    qi = pl.program_id(2)
    rep = n_heads // n_kv
    for g in range(n_kv):
        ks, vs, pens = [], [], []
        for o in range(n_off):
            kb = qi - o
            start = pl.multiple_of(jnp.maximum(kb, 0) * BLK, BLK)
            ks.append(k_ref[pl.ds(start, BLK), k_off + g * HEAD_DIM:k_off + (g + 1) * HEAD_DIM])
            vs.append(v_ref[pl.ds(start, BLK), v_off + g * HEAD_DIM:v_off + (g + 1) * HEAD_DIM])
            pens.append(jnp.where(kb >= 0, 0.0, NEG_INF).astype(F32))
        for r in range(rep):
            h = g * rep + r
            q = q_ref[:, h * HEAD_DIM:(h + 1) * HEAD_DIM]
            ss = []
            m = None
            for o in range(n_off):
                s = _dot_nt(q, ks[o]) + bias_ref[h, o]
                if o > 0:
                    s = s + pens[o]
                ss.append(s)
                mo = jnp.max(s, axis=1, keepdims=True)
                m = mo if m is None else jnp.maximum(m, mo)
            if has_sink:
                sink = sink_ref[h]
                m = jnp.maximum(m, sink)
            den = jnp.zeros_like(m)
            acc = jnp.zeros((BLK, HEAD_DIM), F32)
            for o in range(n_off):
                p = jnp.exp(ss[o] - m)
                den = den + jnp.sum(p, axis=1, keepdims=True)
                acc = acc + jnp.dot(p.astype(BF16), vs[o], preferred_element_type=F32)
            if has_sink:
                den = den + jnp.exp(sink - m)
            o_ref[:, h * HEAD_DIM:(h + 1) * HEAD_DIM] = (acc / den).astype(o_ref.dtype)
            if want_lse:
                lse_ref[:, h * HEAD_DIM:(h + 1) * HEAD_DIM] = jnp.broadcast_to(m + jnp.log(den), (BLK, HEAD_DIM))


def _band(z, bias, *, dil, q_col, k_col, v_col, kv_width, n_kv, k_off=0, v_off=0, sink=None, want_lse=False,
          out_dtype=F32):
    bsz, seq, c = z.shape
    n_heads = 4
    qw = n_heads * HEAD_DIM
    length = seq // dil
    zv = z.reshape(bsz, length, dil * c)
    nq = length // BLK
    n_off = min(bias.shape[1], nq)
    in_specs = [
        pl.BlockSpec((None, BLK, qw), lambda b, r, i: (b, i, r * (c // qw) + q_col // qw)),
        pl.BlockSpec((None, length, kv_width), lambda b, r, i: (b, 0, r * (c // kv_width) + k_col // kv_width)),
        pl.BlockSpec((None, length, kv_width), lambda b, r, i: (b, 0, r * (c // kv_width) + v_col // kv_width)),
        pl.BlockSpec(bias.shape, lambda b, r, i: (0, 0, 0, 0)),
    ]
    args = [zv, zv, zv, bias]
    if sink is not None:
        in_specs.append(pl.BlockSpec(memory_space=pltpu.SMEM))
        args.append(sink)
    o_spec = pl.BlockSpec((None, BLK, qw), lambda b, r, i: (b, i, r))
    o_shape = jax.ShapeDtypeStruct((bsz, length, dil * qw), out_dtype)
    out_specs, out_shape = o_spec, o_shape
    if want_lse:
        out_specs = (o_spec, o_spec)
        out_shape = (o_shape, jax.ShapeDtypeStruct((bsz, length, dil * qw), F32))
    res = pl.pallas_call(
        functools.partial(_band_kernel, n_heads=n_heads, n_kv=n_kv, n_off=n_off, k_off=k_off, v_off=v_off,
                          has_sink=sink is not None, want_lse=want_lse),
        grid=(bsz, dil, nq),
        in_specs=in_specs,
        out_specs=out_specs,
        out_shape=out_shape,
        compiler_params=_params(("parallel", "parallel", "parallel")),
        name=f"band_d{dil}_o{n_off}",
    )(*args)
    if want_lse:
        return res[0].reshape(bsz, seq, qw), res[1].reshape(bsz, seq, qw)
    return res.reshape(bsz, seq, qw)


def _compress_kernel(ch_ref, wa_ref, wb_ref, pea_ref, peb_ref, w2_ref, o_ref):
    ch = ch_ref[...].astype(F32)
    wa = wa_ref[...]
    wb = wb_ref[...]
    first = jnp.dot(ch, wa, preferred_element_type=F32, precision=HIGHEST)
    second = jnp.dot(ch, wb, preferred_element_type=F32, precision=HIGHEST)
    pe = (jnp.dot(pea_ref[...], wa, preferred_element_type=F32, precision=HIGHEST)
          + jnp.dot(peb_ref[...], wb, preferred_element_type=F32, precision=HIGHEST))[0:1]
    n = second.shape[0]
    nxt = pltpu.roll(second, n - 1, 0)
    hid = jax.nn.gelu(first + nxt + pe)
    o_ref[...] = jnp.dot(hid, w2_ref[...], preferred_element_type=F32, precision=HIGHEST)


def _compress(chunks, cmp_pe, cmp_w1, cmp_w2):
    bsz, n_chunk, width = chunks.shape
    per = CMP_LEN // 2
    w1 = cmp_w1.reshape(2, CMP_LEN, HEAD_DIM, HEAD_DIM)
    wexp = jnp.einsum('cpdn,ce->pcden', w1, jnp.eye(2, dtype=F32)).reshape(CMP_LEN, 2 * HEAD_DIM, 2 * HEAD_DIM)
    wa = wexp[:per].reshape(width, 2 * HEAD_DIM)
    wb = wexp[per:].reshape(width, 2 * HEAD_DIM)
    pe = jnp.transpose(cmp_pe, (1, 0, 2)).reshape(CMP_LEN, 2 * HEAD_DIM)
    pea = jnp.broadcast_to(pe[:per].reshape(1, width), (8, width))
    peb = jnp.broadcast_to(pe[per:].reshape(1, width), (8, width))
    zero = jnp.zeros((HEAD_DIM, HEAD_DIM), F32)
    w2 = jnp.concatenate([jnp.concatenate([cmp_w2[0], zero], 1), jnp.concatenate([zero, cmp_w2[1]], 1)], 0)
    full = lambda shape: pl.BlockSpec(shape, lambda b: (0,) * len(shape))
    return pl.pallas_call(
        _compress_kernel,
        grid=(bsz,),
        in_specs=[pl.BlockSpec((None, n_chunk, width), lambda b: (b, 0, 0)),
                  full(wa.shape), full(wb.shape), full(pea.shape), full(peb.shape), full(w2.shape)],
        out_specs=pl.BlockSpec((None, n_chunk, 2 * HEAD_DIM), lambda b: (b, 0, 0)),
        out_shape=jax.ShapeDtypeStruct((bsz, n_chunk, 2 * HEAD_DIM), F32),
        compiler_params=_params(("parallel",)),
        name="nsa_compress",
    )(chunks, wa, wb, pea, peb, w2)


def _cmp_kernel(q_ref, kc_ref, ov_ref, o_ref, sel_ref, *, n_blk):
    tq = q_ref.shape[0]
    n_cmp = kc_ref.shape[0]
    t0 = pl.program_id(1) * tq
    kc = kc_ref[:, :HEAD_DIM].astype(BF16)
    vc = kc_ref[:, HEAD_DIM:].astype(BF16)
    t = t0 + lax.broadcasted_iota(jnp.int32, (tq, n_cmp), 0)
    i = lax.broadcasted_iota(jnp.int32, (tq, n_cmp), 1)
    ok = (i * CMP_STRIDE + (CMP_LEN - 1)) <= t
    psum = jnp.zeros((tq, n_cmp), F32)
    for h in range(NSA_HEADS):
        q = q_ref[:, h * HEAD_DIM:(h + 1) * HEAD_DIM]
        s = jnp.where(ok, _dot_nt(q, kc), NEG_INF)
        m = jnp.max(s, axis=1, keepdims=True)
        p = jnp.where(ok, jnp.exp(s - m), 0.0)
        p = p / jnp.maximum(jnp.sum(p, axis=1, keepdims=True), TINY)
        o_ref[:, h * HEAD_DIM:(h + 1) * HEAD_DIM] = jnp.dot(p.astype(BF16), vc, preferred_element_type=F32)
        psum = psum + p
    score = jnp.dot(psum, ov_ref[...], preferred_element_type=F32, precision=HIGHEST)
    j = lax.broadcasted_iota(jnp.int32, (tq, n_blk), 1)
    cur = (t0 + lax.broadcasted_iota(jnp.int32, (tq, n_blk), 0)) // SEL_LEN
    forced = (j == 0) | (j == cur) | (j == cur - 1)
    score = jnp.where(forced, FORCE_SCORE, jnp.where(j > cur, NEG_INF, score))
    rank = jnp.zeros((tq, n_blk), F32)
    for c in range(n_blk):
        sc = score[:, c:c + 1]
        beats = (sc > score) | ((sc == score) & (j > c))
        rank = rank + jnp.where(beats, 1.0, 0.0)
    sel_ref[...] = jnp.where(rank < N_SEL, 1.0, 0.0)


def _cmp_select(z, kcvc, tq=128):
    bsz, seq, c = z.shape
    n_cmp = kcvc.shape[1]
    n_blk = seq // SEL_LEN
    assert n_blk >= N_SEL
    ci = np.arange(n_cmp)[:, None]
    sj = np.arange(n_blk)[None, :]
    overlap = ((ci * CMP_STRIDE + CMP_LEN - 1 >= sj * SEL_LEN) & (ci * CMP_STRIDE < (sj + 1) * SEL_LEN)
               & (ci < n_cmp - 1)).astype(np.float32)
    qw = NSA_HEADS * HEAD_DIM
    return pl.pallas_call(
        functools.partial(_cmp_kernel, n_blk=n_blk),
        grid=(bsz, seq // tq),
        in_specs=[pl.BlockSpec((None, tq, qw), lambda b, i: (b, i, COL_NSA_Q // qw)),
                  pl.BlockSpec((None, n_cmp, 2 * HEAD_DIM), lambda b, i: (b, 0, 0)),
                  pl.BlockSpec((n_cmp, n_blk), lambda b, i: (0, 0))],
        out_specs=(pl.BlockSpec((None, tq, qw), lambda b, i: (b, i, 0)),
                   pl.BlockSpec((None, tq, n_blk), lambda b, i: (b, i, 0))),
        out_shape=(jax.ShapeDtypeStruct((bsz, seq, qw), F32),
                   jax.ShapeDtypeStruct((bsz, seq, n_blk), F32)),
        compiler_params=_params(("parallel", "parallel")),
        name="nsa_cmp_select",
    )(z, kcvc, jnp.asarray(overlap))


def _sel_kernel(q_ref, kv_ref, sel_ref, bias_ref, e_ref, o_ref):
    qi = pl.program_id(1)
    selb = sel_ref[...].astype(BF16)
    qs = [q_ref[:, h * HEAD_DIM:(h + 1) * HEAD_DIM] for h in range(NSA_HEADS)]

    def block(kb, carry):
        start = pl.multiple_of(kb * BLK, BLK)
        k = kv_ref[pl.ds(start, BLK), :HEAD_DIM]
        v = kv_ref[pl.ds(start, BLK), HEAD_DIM:]
        pen = (jnp.dot(selb, e_ref[kb], preferred_element_type=F32) - 1.0) * (-NEG_INF)
        off = qi - kb
        out = []
        for h in range(NSA_HEADS):
            m, l, acc = carry[h]
            s = _dot_nt(qs[h], k) + bias_ref[h, off] + pen
            m_new = jnp.maximum(m, jnp.max(s, axis=1, keepdims=True))
            alpha = jnp.exp(m - m_new)
            p = jnp.exp(s - m_new)
            l = alpha * l + jnp.sum(p, axis=1, keepdims=True)
            acc = alpha * acc + jnp.dot(p.astype(BF16), v, preferred_element_type=F32)
            out.append((m_new, l, acc))
        return tuple(out)

    init = tuple((jnp.full((BLK, 1), NEG_INF, F32), jnp.zeros((BLK, 1), F32), jnp.zeros((BLK, HEAD_DIM), F32))
                 for _ in range(NSA_HEADS))
    carry = block(qi, init)
    carry = lax.fori_loop(0, qi, block, carry)
    for h in range(NSA_HEADS):
        _, l, acc = carry[h]
        o_ref[:, h * HEAD_DIM:(h + 1) * HEAD_DIM] = acc / l


def _sel_attention(z, sel, bias):
    bsz, seq, c = z.shape
    n_blk = sel.shape[2]
    nq = seq // BLK
    qw = NSA_HEADS * HEAD_DIM
    key_blk = (np.arange(seq) // SEL_LEN).reshape(nq, 1, BLK)
    expand = (key_blk == np.arange(n_blk)[None, :, None]).astype(np.float32)
    return pl.pallas_call(
        _sel_kernel,
        grid=(bsz, nq),
        in_specs=[pl.BlockSpec((None, BLK, qw), lambda b, i: (b, i, COL_NSA_Q // qw)),
                  pl.BlockSpec((None, seq, 2 * HEAD_DIM), lambda b, i: (b, 0, COL_NSA_SEL // (2 * HEAD_DIM))),
                  pl.BlockSpec((None, BLK, n_blk), lambda b, i: (b, i, 0)),
                  pl.BlockSpec(bias.shape, lambda b, i: (0, 0, 0, 0)),
                  pl.BlockSpec((nq, n_blk, BLK), lambda b, i: (0, 0, 0))],
        out_specs=pl.BlockSpec((None, BLK, qw), lambda b, i: (b, i, 0)),
        out_shape=jax.ShapeDtypeStruct((bsz, seq, qw), F32),
        compiler_params=_params(("parallel", "parallel")),
        name="nsa_sel_attn",
    )(z, z, sel, bias, jnp.asarray(expand, dtype=BF16))


def _sb_kernel(q_ref, k_ref, v_ref, u_ref, o_ref):
    qi = pl.program_id(1)
    qs = [q_ref[:, h * HEAD_DIM:(h + 1) * HEAD_DIM] for h in range(SB_HEADS)]
    u = u_ref[...]
    row = lax.broadcasted_iota(jnp.int32, (BLK, BLK), 0)
    col = lax.broadcasted_iota(jnp.int32, (BLK, BLK), 1)
    strict = col < row

    def block(kb, carry, diag):
        start = pl.multiple_of(kb * BLK, BLK)
        out = []
        for h in range(SB_HEADS):
            run, acc = carry[h]
            k = k_ref[pl.ds(start, BLK), h * HEAD_DIM:(h + 1) * HEAD_DIM]
            v = v_ref[pl.ds(start, BLK), h * HEAD_DIM:(h + 1) * HEAD_DIM]
            zz = _dot_nt(qs[h], k)
            sp = jnp.maximum(zz, 0.0) + jnp.log1p(jnp.exp(-jnp.abs(zz)))
            lf = -sp
            if diag:
                lf = jnp.where(strict, lf, 0.0)
            hi = lf.astype(BF16)
            lo = (lf - hi.astype(F32)).astype(BF16)
            after = jnp.dot(hi, u, preferred_element_type=F32) + jnp.dot(lo, u, preferred_element_type=F32) + run
            a = jnp.exp((zz - sp) + after)
            if diag:
                a = jnp.where(strict, a, 0.0)
            acc = acc + jnp.dot(a.astype(BF16), v, preferred_element_type=F32)
            run = run + jnp.sum(lf, axis=1, keepdims=True)
            out.append((run, acc))
        return tuple(out)

    init = tuple((jnp.zeros((BLK, 1), F32), jnp.zeros((BLK, HEAD_DIM), F32)) for _ in range(SB_HEADS))
    carry = block(qi, init, True)
    carry = lax.fori_loop(0, qi, lambda t, cr: block(qi - 1 - t, cr, False), carry)
    for h in range(SB_HEADS):
        o_ref[:, h * HEAD_DIM:(h + 1) * HEAD_DIM] = carry[h][1].astype(o_ref.dtype)


def _stick_breaking(z, out_dtype=BF16):
    bsz, seq, c = z.shape
    nq = seq // BLK
    w = SB_HEADS * HEAD_DIM
    later = (np.arange(BLK)[:, None] > np.arange(BLK)[None, :]).astype(np.float32)
    return pl.pallas_call(
        _sb_kernel,
        grid=(bsz, nq),
        in_specs=[pl.BlockSpec((None, BLK, w), lambda b, i: (b, i, COL_SB_Q // w)),
                  pl.BlockSpec((None, seq, w), lambda b, i: (b, 0, COL_SB_K // w)),
                  pl.BlockSpec((None, seq, w), lambda b, i: (b, 0, COL_SB_V // w)),
                  pl.BlockSpec((BLK, BLK), lambda b, i: (0, 0))],
        out_specs=pl.BlockSpec((None, BLK, w), lambda b, i: (b, i, 0)),
        out_shape=jax.ShapeDtypeStruct((bsz, seq, w), out_dtype),
        compiler_params=_params(("parallel", "parallel")),
        name="stick_breaking",
    )(z, z, z, jnp.asarray(later, dtype=BF16))


def _merge_kernel(ocmp_ref, osel_ref, owin_ref, gate_ref, gexp_ref, ob_ref, oc0_ref, oc1_ref, oc2_ref,
                  l0_ref, l1_ref, l2_ref, od_ref, mg_ref, wb_ref, wo_ref, x_ref, g_ref, lng_ref, lnb_ref, o_ref,
                  *, alpha):
    d = x_ref.shape[1]
    bw = BRANCH_WIDTH
    gate = jax.nn.sigmoid(gate_ref[...].astype(F32))
    gx = jnp.dot(gate, gexp_ref[...], preferred_element_type=F32, precision=HIGHEST)
    oa = gx[:, :bw] * ocmp_ref[...] + gx[:, bw:2 * bw] * osel_ref[...] + gx[:, 2 * bw:] * owin_ref[...]
    l0, l1, l2 = l0_ref[...], l1_ref[...], l2_ref[...]
    m = jnp.maximum(jnp.maximum(l0, l1), l2)
    e0, e1, e2 = jnp.exp(l0 - m), jnp.exp(l1 - m), jnp.exp(l2 - m)
    den = e0 + e1 + e2
    oc = (e0 / den) * oc0_ref[...] + (e1 / den) * oc1_ref[...] + (e2 / den) * oc2_ref[...]
    branches = (oa.astype(BF16), ob_ref[...].astype(BF16), oc.astype(BF16), od_ref[...].astype(BF16))
    merged = jnp.zeros((x_ref.shape[0], d), F32)
    for b in range(N_BRANCHES):
        proj = jnp.dot(branches[b], wb_ref[b], preferred_element_type=F32)
        merged = merged + jax.nn.sigmoid(mg_ref[:, b * d:(b + 1) * d].astype(F32)) * proj
    y = jnp.dot(merged.astype(BF16), wo_ref[...], preferred_element_type=F32)
    v = alpha * x_ref[...] + g_ref[...] * y
    o_ref[...] = _ln(v) * lng_ref[...] + lnb_ref[...]


def _merge(parts, z2, wb, wo, x2, mod4, lng, lnb, seq, *, alpha, g_idx, tm=512):
    m, d = x2.shape
    c = z2.shape[1]
    bw = BRANCH_WIDTH
    per = seq // tm
    gexp = np.zeros((128, 3 * bw), np.float32)
    for h in range(NSA_HEADS):
        for br in range(3):
            gexp[h * 3 + br, br * bw + h * HEAD_DIM: br * bw + (h + 1) * HEAD_DIM] = 1.0
    row = lambda w: pl.BlockSpec((tm, w), lambda i: (i, 0))
    full = lambda shape: pl.BlockSpec(shape, lambda i: (0,) * len(shape))
    ocmp, osel, owin, ob, oc0, oc1, oc2, l0, l1, l2, od = parts
    return pl.pallas_call(
        functools.partial(_merge_kernel, alpha=alpha),
        grid=(m // tm,),
        in_specs=[row(bw), row(bw), row(bw),
                  pl.BlockSpec((tm, 128), lambda i: (i, COL_NSA_GATE // 128)),
                  full(gexp.shape),
                  row(bw), row(bw), row(bw), row(bw), row(bw), row(bw), row(bw), row(bw),
                  pl.BlockSpec((tm, N_BRANCHES * d), lambda i: (i, COL_MG // (N_BRANCHES * d))),
                  full(wb.shape), full(wo.shape),
                  row(d),
                  pl.BlockSpec((None, None, 1, d), lambda i: (i // per, g_idx, 0, 0)),
                  full((1, d)), full((1, d))],
        out_specs=row(d),
        out_shape=jax.ShapeDtypeStruct((m, d), F32),
        compiler_params=_params(("parallel",)),
        name="merge_out",
    )(ocmp, osel, owin, z2, jnp.asarray(gexp), ob, oc0, oc1, oc2, l0, l1, l2, od, z2, wb, wo, x2, mod4,
      lng.reshape(1, d), lnb.reshape(1, d))


def _ffn_kernel(x_ref, sc_ref, sh_ref, g_ref, lng_ref, lnb_ref, wg_ref, wu_ref, wd_ref, o_ref, h_ref, acc_ref,
                *, alpha):
    f = pl.program_id(1)

    @pl.when(f == 0)
    def _():
        h_ref[...] = (_ln(x_ref[...]) * (1.0 + sc_ref[...]) + sh_ref[...]).astype(BF16)
        acc_ref[...] = jnp.zeros_like(acc_ref)

    h = h_ref[...]
    a = jnp.dot(h, wg_ref[...], preferred_element_type=F32)
    u = jnp.dot(h, wu_ref[...], preferred_element_type=F32)
    act = (a * jax.nn.sigmoid(a)) * u
    acc_ref[...] += jnp.dot(act.astype(BF16), wd_ref[...], preferred_element_type=F32)

    @pl.when(f == pl.num_programs(1) - 1)
    def _():
        v = alpha * x_ref[...] + g_ref[...] * acc_ref[...]
        o_ref[...] = _ln(v) * lng_ref[...] + lnb_ref[...]


def _ffn(x2, mod4, wg, wu, wd, lng, lnb, seq, *, alpha, tm=512, tf=1408):
    m, d = x2.shape
    ff = wg.shape[1]
    per = seq // tm
    modspec = lambda k: pl.BlockSpec((None, None, 1, d), lambda i, f: (i // per, k, 0, 0))
    return pl.pallas_call(
        functools.partial(_ffn_kernel, alpha=alpha),
        grid=(m // tm, ff // tf),
        in_specs=[pl.BlockSpec((tm, d), lambda i, f: (i, 0)),
                  modspec(4), modspec(3), modspec(5),
                  pl.BlockSpec((1, d), lambda i, f: (0, 0)), pl.BlockSpec((1, d), lambda i, f: (0, 0)),
                  pl.BlockSpec((d, tf), lambda i, f: (0, f)),
                  pl.BlockSpec((d, tf), lambda i, f: (0, f)),
                  pl.BlockSpec((tf, d), lambda i, f: (f, 0))],
        out_specs=pl.BlockSpec((tm, d), lambda i, f: (i, 0)),
        out_shape=jax.ShapeDtypeStruct((m, d), F32),
        scratch_shapes=[pltpu.VMEM((tm, d), BF16), pltpu.VMEM((tm, d), F32)],
        compiler_params=_params(("parallel", "arbitrary")),
        name="ffn_dense",
    )(x2, mod4, mod4, mod4, lng.reshape(1, d), lnb.reshape(1, d), wg, wu, wd)


def _moe_kernel(x_ref, sc_ref, sh_ref, g_ref, lng_ref, lnb_ref, wr_ref, wg_ref, wu_ref, wd_ref, o_ref,
                h_ref, gates_ref, acc_ref, *, alpha):
    e = pl.program_id(1)
    f = pl.program_id(2)
    tm = x_ref.shape[0]
    lane = lax.broadcasted_iota(jnp.int32, (tm, 128), 1).astype(F32)

    @pl.when((e == 0) & (f == 0))
    def _():
        h32 = _ln(x_ref[...]) * (1.0 + sc_ref[...]) + sh_ref[...]
        h_ref[...] = h32.astype(BF16)
        acc_ref[...] = jnp.zeros_like(acc_ref)
        logits = jnp.dot(h32, wr_ref[...], preferred_element_type=F32, precision=HIGHEST)
        logits = jnp.where(lane < N_EXPERTS, logits, -jnp.inf)
        t1 = jnp.max(logits, axis=1, keepdims=True)
        i1 = jnp.min(jnp.where(logits == t1, lane, 128), axis=1, keepdims=True)
        rest = jnp.where(lane == i1, -jnp.inf, logits)
        t2 = jnp.max(rest, axis=1, keepdims=True)
        i2 = jnp.min(jnp.where(rest == t2, lane, 128), axis=1, keepdims=True)
        e2 = jnp.exp(t2 - t1)
        den = 1.0 + e2
        gates_ref[...] = jnp.where(lane == i1, 1.0 / den, jnp.where(lane == i2, e2 / den, 0.0))

    gate = jnp.sum(jnp.where(lane == e.astype(F32), gates_ref[...], 0.0), axis=1, keepdims=True)
    h = h_ref[...]
    a = jnp.dot(h, wg_ref[...], preferred_element_type=F32)
    u = jnp.dot(h, wu_ref[...], preferred_element_type=F32)
    act = (a * jax.nn.sigmoid(a)) * u
    acc_ref[...] += gate * jnp.dot(act.astype(BF16), wd_ref[...], preferred_element_type=F32)

    @pl.when((e == pl.num_programs(1) - 1) & (f == pl.num_programs(2) - 1))
    def _():
        v = alpha * x_ref[...] + g_ref[...] * acc_ref[...]
        o_ref[...] = _ln(v) * lng_ref[...] + lnb_ref[...]


def _moe(x2, mod4, wr, wg, wu, wd, lng, lnb, seq, *, alpha, tm=512, tf=1792):
    m, d = x2.shape
    n_exp, _, ff = wg.shape
    per = seq // tm
    wr_pad = jnp.pad(wr, ((0, 0), (0, 128 - n_exp)))
    modspec = lambda k: pl.BlockSpec((None, None, 1, d), lambda i, e, f: (i // per, k, 0, 0))
    return pl.pallas_call(
        functools.partial(_moe_kernel, alpha=alpha),
        grid=(m // tm, n_exp, ff // tf),
        in_specs=[pl.BlockSpec((tm, d), lambda i, e, f: (i, 0)),
                  modspec(4), modspec(3), modspec(5),
                  pl.BlockSpec((1, d), lambda i, e, f: (0, 0)), pl.BlockSpec((1, d), lambda i, e, f: (0, 0)),
                  pl.BlockSpec((d, 128), lambda i, e, f: (0, 0)),
                  pl.BlockSpec((None, d, tf), lambda i, e, f: (e, 0, f)),
                  pl.BlockSpec((None, d, tf), lambda i, e, f: (e, 0, f)),
                  pl.BlockSpec((None, tf, d), lambda i, e, f: (e, f, 0))],
        out_specs=pl.BlockSpec((tm, d), lambda i, e, f: (i, 0)),
        out_shape=jax.ShapeDtypeStruct((m, d), F32),
        scratch_shapes=[pltpu.VMEM((tm, d), BF16), pltpu.VMEM((tm, 128), F32), pltpu.VMEM((tm, d), F32)],
        compiler_params=_params(("parallel", "arbitrary", "arbitrary")),
        name="moe_dense",
    )(x2, mod4, mod4, mod4, lng.reshape(1, d), lnb.reshape(1, d), wr_pad, wg, wu, wd)


def _relayout_w_in(w_in):
    d = w_in.shape[0]
    n_mg = N_BRANCHES * d
    rest = w_in.shape[1] - n_mg
    split = COL_NSA_GATE - COL_NSA_Q + GATE_COLS
    w = jnp.concatenate([w_in[:, rest:], w_in[:, :split], jnp.zeros((d, GATE_PAD), w_in.dtype),
                         w_in[:, split:rest]], axis=1)
    scale = np.ones((w.shape[1],), np.float32)
    for start, width in Q_COL_RANGES:
        scale[start:start + width] = 1.0 / math.sqrt(HEAD_DIM)
    return (w * jnp.asarray(scale)).astype(BF16)


def _mixers(z, cmp_pe, cmp_w1, cmp_w2, sinks, biases):
    bsz, seq, c = z.shape
    bias_sel, bias_win, bias_swa, bias_dil = biases
    chunks = z[:, :, COL_NSA_CMP:COL_NSA_CMP + 2 * HEAD_DIM].reshape(bsz, seq // CMP_STRIDE, CMP_STRIDE * 2 * HEAD_DIM)
    kcvc = _compress(chunks, cmp_pe, cmp_w1, cmp_w2)
    o_cmp, sel = _cmp_select(z, kcvc)
    o_sel = _sel_attention(z, sel, bias_sel)
    o_win = _band(z, bias_win, dil=1, q_col=COL_NSA_Q, k_col=COL_NSA_WIN, v_col=COL_NSA_WIN, kv_width=128,
                  n_kv=1, k_off=0, v_off=HEAD_DIM)
    o_b = _band(z, bias_swa, dil=1, q_col=COL_SWA_Q, k_col=COL_SWA_K, v_col=COL_SWA_V, kv_width=128,
                n_kv=SWA_KV_HEADS, sink=sinks, out_dtype=BF16)
    o_c, lse_c = [], []
    for gi, (_, dil) in enumerate(DIL_PATTERNS):
        o, lse = _band(z, bias_dil[gi], dil=dil, q_col=COL_DIL_Q + gi * 256, k_col=COL_DIL_K, v_col=COL_DIL_V,
                       kv_width=256, n_kv=DIL_KV_HEADS, want_lse=True)
        o_c.append(o)
        lse_c.append(lse)
    o_d = _stick_breaking(z)
    m = bsz * seq
    flat = lambda a: a.reshape(m, a.shape[-1])
    return tuple(flat(a) for a in (o_cmp, o_sel, o_win, o_b, *o_c, *lse_c, o_d))


def kernel(x, c, rel_bias, w_ada, b_ada, w_in, w_branch, w_out, cmp_pe, cmp_w1, cmp_w2, swa_sinks, ln_g, ln_b,
           ffn_w_gate, ffn_w_up, ffn_w_down, moe_router, moe_w_gate, moe_w_up, moe_w_down):
    bsz, seq, d = x.shape
    depth = w_ada.shape[0]
    alpha = (2 * depth) ** 0.25
    m = bsz * seq
    nq = seq // BLK

    bias_sel = _bias_tiles(rel_bias, _bucket_tiles(nq, seq, 1), 0, NSA_HEADS)
    bias_win = _bias_tiles(rel_bias, _bucket_tiles(-(-(NSA_WINDOW - 1) // BLK) + 1, NSA_WINDOW - 1, 1), 0, NSA_HEADS)
    bias_swa = _bias_tiles(rel_bias, _bucket_tiles(-(-(SWA_WINDOW - 1) // BLK) + 1, SWA_WINDOW - 1, 1),
                           NSA_HEADS, SWA_HEADS)
    bias_dil = []
    for gi, (win, dil) in enumerate(DIL_PATTERNS):
        n_back = win // dil
        bias_dil.append(_bias_tiles(rel_bias, _bucket_tiles(-(-n_back // BLK) + 1, n_back, dil),
                                    NSA_HEADS + SWA_HEADS + gi * DIL_HEADS_PER_GROUP, DIL_HEADS_PER_GROUP))
    biases = (bias_sel, bias_win, bias_swa, bias_dil)

    x2 = x.reshape(m, d)
    for layer in range(depth):
        mod4 = _ada(c, w_ada[layer], b_ada[layer]).reshape(bsz, 6, 1, d)
        z2 = _inproj(x2, mod4, _relayout_w_in(w_in[layer]), seq, sc_idx=1, sh_idx=0)
        parts = _mixers(z2.reshape(bsz, seq, z2.shape[1]), cmp_pe[layer], cmp_w1[layer], cmp_w2[layer],
                        swa_sinks[layer], biases)
        x2 = _merge(parts, z2, w_branch[layer].astype(BF16), w_out[layer].astype(BF16), x2, mod4,
                    ln_g[layer, 0], ln_b[layer, 0], seq, alpha=alpha, g_idx=2)
        i = layer // 2
        if layer % 2 == 0:
            x2 = _ffn(x2, mod4, ffn_w_gate[i].astype(BF16), ffn_w_up[i].astype(BF16), ffn_w_down[i].astype(BF16),
                      ln_g[layer, 1], ln_b[layer, 1], seq, alpha=alpha)
        else:
            x2 = _moe(x2, mod4, moe_router[i], moe_w_gate[i].astype(BF16), moe_w_up[i].astype(BF16),
                      moe_w_down[i].astype(BF16), ln_g[layer, 1], ln_b[layer, 1], seq, alpha=alpha)
    return x2.reshape(bsz, seq, d)
```

```python
import functools
import math

import numpy as np
import jax
import jax.numpy as jnp
from jax import lax
from jax.experimental import pallas as pl
from jax.experimental.pallas import tpu as pltpu

F32 = jnp.float32
BF16 = jnp.bfloat16
HIGHEST = lax.Precision.HIGHEST

HEAD_DIM = 64
BLK = 128
N_SLOTS = 4
NSA_HEADS = 4
CMP_LEN = 32
CMP_STRIDE = 16
SEL_LEN = 64
N_SEL = 16
NSA_WINDOW = 512
SWA_HEADS = 4
SWA_WINDOW = 128
SWA_SLOT_HEADS = (0, 2, 1, 3)
DIL_PATTERNS = ((128, 1), (512, 4), (2048, 16))
DIL_HEADS_PER_GROUP = 4
SB_HEADS = 4
N_BRANCHES = 4
BRANCH_WIDTH = 4 * HEAD_DIM
NUM_BUCKETS = 32
MAX_DISTANCE = 2048
N_EXPERTS = 8
LN_EPS = 1e-5
NEG_INF = -1e30
TINY = 1e-30
FORCE_SCORE = 1e4

IN_NSA_Q, IN_K_CMP, IN_V_CMP, IN_K_SEL, IN_V_SEL, IN_K_WIN, IN_V_WIN, IN_GATE = 0, 256, 320, 384, 448, 512, 576, 640
IN_SWA_Q, IN_SWA_K, IN_SWA_V = 652, 908, 1036
IN_DIL_Q, IN_DIL_K, IN_DIL_V = 1164, 1932, 2188
IN_SB_Q, IN_SB_K, IN_SB_V = 2444, 2700, 2956
IN_MG = 3212
GATE_COLS = NSA_HEADS * 3

COL_MG = 0
COL_NSA_Q = 4096
COL_NSA_CMP = 4352
COL_NSA_SEL = 4480
COL_NSA_WIN = 4608
COL_NSA_GATE = 4736
COL_SWA_Q = 4864
COL_SWA_K = 5120
COL_DIL_Q = 5376
COL_DIL_K = 6144
COL_SB_Q = 6400
COL_SB_K = 6656
N_COLS = 6912
ROW_NSA_V = 0
ROW_SWA_V = 128
ROW_DIL_V = 256
ROW_SB_V = 512
ROW_GATE = 768
N_ROWS_T = 784

VMEM_LIMIT = 56 * 1024 * 1024


def _params(sem):
    return pltpu.CompilerParams(dimension_semantics=sem, vmem_limit_bytes=VMEM_LIMIT)


def _ln(x):
    mu = jnp.mean(x, axis=-1, keepdims=True)
    xc = x - mu
    var = jnp.mean(xc * xc, axis=-1, keepdims=True)
    return xc * lax.rsqrt(var + LN_EPS)


def _dot_nt(a, b, precision=None):
    return lax.dot_general(a, b, (((1,), (1,)), ((), ())), preferred_element_type=F32, precision=precision)


def _dot(a, b, precision=None):
    return jnp.dot(a, b, preferred_element_type=F32, precision=precision)


def _full(shape):
    return pl.BlockSpec(shape, lambda *_: (0,) * len(shape))


def _ada_kernel(c_ref, w_ref, b_ref, o_ref):
    c = c_ref[...]
    s = c * jax.nn.sigmoid(c)
    o_ref[...] = _dot(s, w_ref[...], HIGHEST) + b_ref[...]


def _ada(c, w, b):
    bsz, d = c.shape
    n = w.shape[1]
    tn = 1024
    return pl.pallas_call(
        _ada_kernel,
        grid=(n // tn,),
        in_specs=[pl.BlockSpec((bsz, d), lambda j: (0, 0)),
                  pl.BlockSpec((d, tn), lambda j: (0, j)),
                  pl.BlockSpec((1, tn), lambda j: (0, j))],
        out_specs=pl.BlockSpec((bsz, tn), lambda j: (0, j)),
        out_shape=jax.ShapeDtypeStruct((bsz, n), F32),
        compiler_params=_params(("parallel",)),
        name="ada",
    )(c, w, b.reshape(1, n))


def _inproj_kernel(x_ref, sc_ref, sh_ref, w_ref, o_ref):
    h = _ln(x_ref[...]) * (1.0 + sc_ref[...]) + sh_ref[...]
    o_ref[...] = _dot(h.astype(BF16), w_ref[...]).astype(o_ref.dtype)


def _inproj(x2, mod4, w, seq, *, sc_idx, sh_idx, tm=512):
    m, d = x2.shape
    n = w.shape[1]
    tn = n // 2
    per = seq // tm
    return pl.pallas_call(
        _inproj_kernel,
        grid=(n // tn, m // tm),
        in_specs=[pl.BlockSpec((tm, d), lambda j, i: (i, 0)),
                  pl.BlockSpec((None, None, 1, d), lambda j, i: (i // per, sc_idx, 0, 0)),
                  pl.BlockSpec((None, None, 1, d), lambda j, i: (i // per, sh_idx, 0, 0)),
                  pl.BlockSpec((d, tn), lambda j, i: (0, j))],
        out_specs=pl.BlockSpec((tm, tn), lambda j, i: (i, j)),
        out_shape=jax.ShapeDtypeStruct((m, n), BF16),
        compiler_params=_params(("parallel", "parallel")),
        name="inproj",
    )(x2, mod4, mod4, w)


def _inproj_t_kernel(x_ref, sc_ref, sh_ref, w_ref, o_ref):
    h = (_ln(x_ref[...]) * (1.0 + sc_ref[...]) + sh_ref[...]).astype(BF16)
    res = _dot_nt(w_ref[...], h)
    for j in range(o_ref.shape[0]):
        o_ref[j] = res[:, j * BLK:(j + 1) * BLK].astype(o_ref.dtype)


def _inproj_t(x2, mod4, wt, seq, *, sc_idx, sh_idx, tm=512):
    m, d = x2.shape
    rows = wt.shape[0]
    per = seq // tm
    return pl.pallas_call(
        _inproj_t_kernel,
        grid=(m // tm,),
        in_specs=[pl.BlockSpec((tm, d), lambda i: (i, 0)),
                  pl.BlockSpec((None, None, 1, d), lambda i: (i // per, sc_idx, 0, 0)),
                  pl.BlockSpec((None, None, 1, d), lambda i: (i // per, sh_idx, 0, 0)),
                  _full(wt.shape)],
        out_specs=pl.BlockSpec((tm // BLK, rows, BLK), lambda i: (i, 0, 0)),
        out_shape=jax.ShapeDtypeStruct((m // BLK, rows, BLK), BF16),
        compiler_params=_params(("parallel",)),
        name="inproj_t",
    )(x2, mod4, mod4, wt)


def _np_bucket(dist):
    dist = np.maximum(dist, 0)
    max_exact = NUM_BUCKETS // 2
    d_f = np.maximum(dist, 1).astype(np.float32)
    large = max_exact + (np.log(d_f / np.float32(max_exact)) / np.float32(math.log(MAX_DISTANCE / max_exact))
                         * np.float32(NUM_BUCKETS - max_exact)).astype(np.int32)
    large = np.minimum(large, NUM_BUCKETS - 1)
    return np.where(dist < max_exact, dist, large).astype(np.int32)


def _bucket_tiles(n_off, max_dist, dil):
    b = np.arange(BLK)[:, None]
    a = np.arange(BLK)[None, :]
    tiles = []
    for o in range(n_off):
        dist = o * BLK + a - b
        ok = (dist >= 0) & (dist <= max_dist) & (dist % dil == 0)
        tiles.append(np.where(ok, _np_bucket(dist), -1))
    return np.stack(tiles).astype(np.int32)


def _bias_kernel(tbl_ref, bk_ref, o_ref, *, heads):
    slot = pl.program_id(0)
    h = jnp.int32(heads[0])
    for j in range(1, len(heads)):
        h = jnp.where(slot == j, heads[j], h)
    b = bk_ref[...]
    out = jnp.where(b < 0, NEG_INF, 0.0).astype(F32)
    for k in range(NUM_BUCKETS):
        out = jnp.where(b == k, tbl_ref[k, h], out)
    o_ref[...] = out


def _bias_tiles(rel_bias, buckets, heads):
    n_off = buckets.shape[0]
    return pl.pallas_call(
        functools.partial(_bias_kernel, heads=tuple(heads)),
        grid=(len(heads),),
        in_specs=[pl.BlockSpec(memory_space=pltpu.SMEM), _full((n_off, BLK, BLK))],
        out_specs=pl.BlockSpec((n_off, BLK, BLK), lambda s: (0, 0, s)),
        out_shape=jax.ShapeDtypeStruct((n_off, BLK, len(heads) * BLK), F32),
        compiler_params=_params(("parallel",)),
        name="bias_tiles",
    )(rel_bias, jnp.asarray(buckets))


def _slot_queries(q, kw):
    lane = lax.broadcasted_iota(jnp.int32, (BLK, kw), 1)
    zero = jnp.zeros((BLK, kw), q.dtype)
    parts = []
    for j in range(N_SLOTS):
        lo = j * HEAD_DIM
        grp = q[:, (lo // kw) * kw:(lo // kw + 1) * kw]
        inside = (lane >= lo % kw) & (lane < lo % kw + HEAD_DIM)
        parts.append(jnp.where(inside, grp, zero))
    return jnp.concatenate(parts, axis=0)


def _softmax_step(s, vt, state, slot_rows):
    m, l = state[0], state[1]
    m_new = jnp.maximum(m, jnp.max(s, axis=0, keepdims=True))
    alpha = jnp.exp(m - m_new)
    p = jnp.exp(s - m_new)
    l = alpha * l + jnp.sum(p, axis=0, keepdims=True)
    res = _dot(vt, p.astype(BF16))
    accs = []
    for j in range(N_SLOTS):
        r0 = slot_rows[j]
        accs.append(alpha[:, j * BLK:(j + 1) * BLK] * state[2 + j]
                    + res[r0:r0 + HEAD_DIM, j * BLK:(j + 1) * BLK])
    return (m_new, l, *accs)


def _softmax_init():
    return (jnp.full((1, N_SLOTS * BLK), NEG_INF, F32), jnp.zeros((1, N_SLOTS * BLK), F32),
            *[jnp.zeros((HEAD_DIM, BLK), F32) for _ in range(N_SLOTS)])


def _store_heads(o_ref, outs):
    o_ref[...] = jnp.transpose(jnp.concatenate(outs, axis=0)).astype(o_ref.dtype)


def _key_rows(k_ref, kb):
    return k_ref[pl.ds(pl.multiple_of(kb * BLK, BLK), BLK), :]


def _compress_kernel(ch_ref, wa_ref, wb_ref, pea_ref, peb_ref, w2k_ref, w2vt_ref, kc_ref, vct_ref):
    ch = ch_ref[...].astype(F32)
    wa = wa_ref[...]
    wb = wb_ref[...]
    first = _dot(ch, wa, HIGHEST)
    second = _dot(ch, wb, HIGHEST)
    pe = (_dot(pea_ref[...], wa, HIGHEST) + _dot(peb_ref[...], wb, HIGHEST))[0:1]
    n = second.shape[0]
    nxt = pltpu.roll(second, n - 1, 0)
    hid = jax.nn.gelu(first + nxt + pe)
    kc_ref[...] = _dot(hid, w2k_ref[...], HIGHEST).astype(kc_ref.dtype)
    vct_ref[...] = _dot_nt(w2vt_ref[...], hid, HIGHEST).astype(vct_ref.dtype)


def _compress(chunks, cmp_pe, cmp_w1, cmp_w2):
    bsz, n_chunk, width = chunks.shape
    per = CMP_LEN // 2
    hd = HEAD_DIM
    w1 = cmp_w1.reshape(2, CMP_LEN, hd, hd)
    wexp = jnp.einsum('cpdn,ce->pcden', w1, jnp.eye(2, dtype=F32)).reshape(CMP_LEN, 2 * hd, 2 * hd)
    wa = wexp[:per].reshape(width, 2 * hd)
    wb = wexp[per:].reshape(width, 2 * hd)
    pe = jnp.transpose(cmp_pe, (1, 0, 2)).reshape(CMP_LEN, 2 * hd)
    pea = jnp.broadcast_to(pe[:per].reshape(1, width), (8, width))
    peb = jnp.broadcast_to(pe[per:].reshape(1, width), (8, width))
    zero = jnp.zeros((hd, hd), F32)
    w2k = jnp.concatenate([jnp.concatenate([cmp_w2[0], cmp_w2[0]], 1), jnp.concatenate([zero, zero], 1)], 0)
    w2vt = jnp.concatenate([zero, cmp_w2[1].T], 1)
    return pl.pallas_call(
        _compress_kernel,
        grid=(bsz,),
        in_specs=[pl.BlockSpec((None, n_chunk, width), lambda b: (b, 0, 0)),
                  _full(wa.shape), _full(wb.shape), _full(pea.shape), _full(peb.shape), _full(w2k.shape),
                  _full(w2vt.shape)],
        out_specs=(pl.BlockSpec((None, n_chunk, 2 * hd), lambda b: (b, 0, 0)),
                   pl.BlockSpec((None, hd, n_chunk), lambda b: (b, 0, 0))),
        out_shape=(jax.ShapeDtypeStruct((bsz, n_chunk, 2 * hd), BF16),
                   jax.ShapeDtypeStruct((bsz, hd, n_chunk), BF16)),
        compiler_params=_params(("parallel",)),
        name="nsa_compress",
    )(chunks, wa, wb, pea, peb, w2k, w2vt)


def _nsa_kernel(q_ref, kc_ref, vct_ref, ovt_ref, ksel_ref, kwin_ref, vt_ref, gate_ref, bsel_ref, bwin_ref, et_ref,
                o_ref, *, n_blk, n_win_off):
    qi = pl.program_id(1)
    t0 = qi * BLK
    n_cmp = kc_ref.shape[0]
    wq = _slot_queries(q_ref[...], 2 * HEAD_DIM)
    slot_rows = (0,) * N_SLOTS

    s = _dot_nt(kc_ref[...], wq)
    shape = (n_cmp, N_SLOTS * BLK)
    tq = t0 + lax.broadcasted_iota(jnp.int32, shape, 1) % BLK
    ci = lax.broadcasted_iota(jnp.int32, shape, 0)
    ok = (ci * CMP_STRIDE + (CMP_LEN - 1)) <= tq
    s = jnp.where(ok, s, NEG_INF)
    m = jnp.max(s, axis=0, keepdims=True)
    p = jnp.where(ok, jnp.exp(s - m), 0.0)
    p = p / jnp.maximum(jnp.sum(p, axis=0, keepdims=True), TINY)
    o_cmp = _dot(vct_ref[...], p.astype(BF16))
    psum = p[:, 0:BLK]
    for j in range(1, N_SLOTS):
        psum = psum + p[:, j * BLK:(j + 1) * BLK]

    hi = psum.astype(BF16)
    lo = (psum - hi.astype(F32)).astype(BF16)
    score = _dot(ovt_ref[...], hi) + _dot(ovt_ref[...], lo)
    jb = lax.broadcasted_iota(jnp.int32, (n_blk, BLK), 0)
    cur = (t0 + lax.broadcasted_iota(jnp.int32, (n_blk, BLK), 1)) // SEL_LEN
    forced = (jb == 0) | (jb == cur) | (jb == cur - 1)
    score = jnp.where(forced, FORCE_SCORE, jnp.where(jb > cur, NEG_INF, score))
    rank = jnp.zeros((n_blk, BLK), F32)
    for c in range(n_blk):
        row = score[c:c + 1, :]
        beats = (row > score) | ((row == score) & (jb > c))
        rank = rank + jnp.where(beats, 1.0, 0.0)
    sel = jnp.where(rank < N_SEL, 1.0, 0.0).astype(BF16)

    def sel_step(kb, state):
        pen = (_dot(et_ref[kb], sel) - 1.0) * (-NEG_INF)
        pen = jnp.concatenate([pen] * N_SLOTS, axis=1)
        sc = _dot_nt(_key_rows(ksel_ref, kb), wq) + bsel_ref[qi - kb] + pen
        return _softmax_step(sc, vt_ref[kb, 0:HEAD_DIM, :], state, slot_rows)

    st = sel_step(qi, _softmax_init())
    st = lax.fori_loop(0, qi, sel_step, st)
    o_sel = [st[2 + j] / st[1][:, j * BLK:(j + 1) * BLK] for j in range(N_SLOTS)]

    def win_step(kb, state):
        sc = _dot_nt(_key_rows(kwin_ref, kb), wq) + bwin_ref[qi - kb]
        return _softmax_step(sc, vt_ref[kb, HEAD_DIM:2 * HEAD_DIM, :], state, slot_rows)

    st = win_step(qi, _softmax_init())
    st = lax.fori_loop(jnp.maximum(qi - (n_win_off - 1), 0), qi, win_step, st)
    o_win = [st[2 + j] / st[1][:, j * BLK:(j + 1) * BLK] for j in range(N_SLOTS)]

    gate = jax.nn.sigmoid(gate_ref[...].astype(F32))
    outs = []
    for j in range(N_SLOTS):
        outs.append(gate[3 * j:3 * j + 1] * o_cmp[:, j * BLK:(j + 1) * BLK]
                    + gate[3 * j + 1:3 * j + 2] * o_sel[j] + gate[3 * j + 2:3 * j + 3] * o_win[j])
    _store_heads(o_ref, outs)


def _nsa(z, zt, kc, vct, bias_sel, bias_win):
    bsz, seq, _ = z.shape
    nq = seq // BLK
    n_cmp = kc.shape[1]
    n_blk = seq // SEL_LEN
    assert n_blk >= N_SEL
    ci = np.arange(n_cmp)[None, :]
    sj = np.arange(n_blk)[:, None]
    overlap_t = ((ci * CMP_STRIDE + CMP_LEN - 1 >= sj * SEL_LEN) & (ci * CMP_STRIDE < (sj + 1) * SEL_LEN)
                 & (ci < n_cmp - 1)).astype(np.float32)
    key_blk = (np.arange(seq) // SEL_LEN).reshape(nq, BLK, 1)
    expand_t = (key_blk == np.arange(n_blk)[None, None, :]).astype(np.float32)
    qw = NSA_HEADS * HEAD_DIM
    kw = 2 * HEAD_DIM
    return pl.pallas_call(
        functools.partial(_nsa_kernel, n_blk=n_blk, n_win_off=bias_win.shape[0]),
        grid=(bsz, nq),
        in_specs=[pl.BlockSpec((None, BLK, qw), lambda b, i: (b, i, COL_NSA_Q // qw)),
                  pl.BlockSpec((None, n_cmp, kw), lambda b, i: (b, 0, 0)),
                  pl.BlockSpec((None, HEAD_DIM, n_cmp), lambda b, i: (b, 0, 0)),
                  _full(overlap_t.shape),
                  pl.BlockSpec((None, seq, kw), lambda b, i: (b, 0, COL_NSA_SEL // kw)),
                  pl.BlockSpec((None, seq, kw), lambda b, i: (b, 0, COL_NSA_WIN // kw)),
                  pl.BlockSpec((None, nq, kw, BLK), lambda b, i: (b, 0, ROW_NSA_V // kw, 0)),
                  pl.BlockSpec((None, None, 16, BLK), lambda b, i: (b, i, ROW_GATE // 16, 0)),
                  _full(bias_sel.shape), _full(bias_win.shape), _full(expand_t.shape)],
        out_specs=pl.BlockSpec((None, BLK, qw), lambda b, i: (b, i, 0)),
        out_shape=jax.ShapeDtypeStruct((bsz, seq, qw), BF16),
        compiler_params=_params(("parallel", "parallel")),
        name="nsa_mixer",
    )(z, kc, vct, jnp.asarray(overlap_t, dtype=BF16), z, z, zt, zt, bias_sel, bias_win,
      jnp.asarray(expand_t, dtype=BF16))


def _swa_kernel(q_ref, k_ref, vt_ref, bias_ref, sink_ref, o_ref, *, n_off):
    qi = pl.program_id(1)
    wq = _slot_queries(q_ref[...], 2 * HEAD_DIM)
    slot_rows = tuple((j % 2) * HEAD_DIM for j in range(N_SLOTS))

    def step(kb, state):
        sc = _dot_nt(_key_rows(k_ref, kb), wq) + bias_ref[qi - kb]
        return _softmax_step(sc, vt_ref[kb], state, slot_rows)

    st = step(qi, _softmax_init())
    st = lax.fori_loop(jnp.maximum(qi - (n_off - 1), 0), qi, step, st)
    m, l = st[0], st[1]
    outs = []
    for j in range(N_SLOTS):
        sink = sink_ref[SWA_SLOT_HEADS[j]]
        mj = m[:, j * BLK:(j + 1) * BLK]
        lj = l[:, j * BLK:(j + 1) * BLK]
        m2 = jnp.maximum(mj, sink)
        scale = jnp.exp(mj - m2)
        den = lj * scale + jnp.exp(sink - m2)
        outs.append(st[2 + j] * (scale / den))
    _store_heads(o_ref, outs)


def _swa(z, zt, bias, sinks):
    bsz, seq, _ = z.shape
    nq = seq // BLK
    qw = SWA_HEADS * HEAD_DIM
    kw = 2 * HEAD_DIM
    return pl.pallas_call(
        functools.partial(_swa_kernel, n_off=bias.shape[0]),
        grid=(bsz, nq),
        in_specs=[pl.BlockSpec((None, BLK, qw), lambda b, i: (b, i, COL_SWA_Q // qw)),
                  pl.BlockSpec((None, seq, kw), lambda b, i: (b, 0, COL_SWA_K // kw)),
                  pl.BlockSpec((None, nq, kw, BLK), lambda b, i: (b, 0, ROW_SWA_V // kw, 0)),
                  _full(bias.shape),
                  pl.BlockSpec(memory_space=pltpu.SMEM)],
        out_specs=pl.BlockSpec((None, BLK, qw), lambda b, i: (b, i, 0)),
        out_shape=jax.ShapeDtypeStruct((bsz, seq, qw), BF16),
        compiler_params=_params(("parallel", "parallel")),
        name="swa_mixer",
    )(z, z, zt, bias, sinks)


def _dil_kernel(q0_ref, q1_ref, q2_ref, k_ref, vt_ref, b0_ref, b1_ref, b2_ref, o_ref):
    qi = pl.program_id(1)
    slot_rows = tuple(j * HEAD_DIM for j in range(N_SLOTS))
    groups = []
    for q_ref, bias_ref in ((q0_ref, b0_ref), (q1_ref, b1_ref), (q2_ref, b2_ref)):
        wq = _slot_queries(q_ref[...], N_SLOTS * HEAD_DIM)
        n_off = bias_ref.shape[0]

        def step(kb, state, wq=wq, bias_ref=bias_ref):
            sc = _dot_nt(_key_rows(k_ref, kb), wq) + bias_ref[qi - kb]
            return _softmax_step(sc, vt_ref[kb], state, slot_rows)

        st = step(qi, _softmax_init())
        st = lax.fori_loop(jnp.maximum(qi - (n_off - 1), 0), qi, step, st)
        groups.append(st)
    lses = [st[0] + jnp.log(st[1]) for st in groups]
    top = jnp.maximum(jnp.maximum(lses[0], lses[1]), lses[2])
    es = [jnp.exp(v - top) for v in lses]
    den = es[0] + es[1] + es[2]
    outs = []
    for j in range(N_SLOTS):
        sl = slice(j * BLK, (j + 1) * BLK)
        acc = jnp.zeros((HEAD_DIM, BLK), F32)
        for st, e in zip(groups, es):
            acc = acc + (e[:, sl] / den[:, sl]) * (st[2 + j] / st[1][:, sl])
        outs.append(acc)
    _store_heads(o_ref, outs)


def _dilated(z, zt, biases):
    bsz, seq, _ = z.shape
    nq = seq // BLK
    qw = DIL_HEADS_PER_GROUP * HEAD_DIM
    qspec = lambda g: pl.BlockSpec((None, BLK, qw), lambda b, i: (b, i, COL_DIL_Q // qw + g))
    return pl.pallas_call(
        _dil_kernel,
        grid=(bsz, nq),
        in_specs=[qspec(0), qspec(1), qspec(2),
                  pl.BlockSpec((None, seq, qw), lambda b, i: (b, 0, COL_DIL_K // qw)),
                  pl.BlockSpec((None, nq, qw, BLK), lambda b, i: (b, 0, ROW_DIL_V // qw, 0)),
                  _full(biases[0].shape), _full(biases[1].shape), _full(biases[2].shape)],
        out_specs=pl.BlockSpec((None, BLK, qw), lambda b, i: (b, i, 0)),
        out_shape=jax.ShapeDtypeStruct((bsz, seq, qw), BF16),
        compiler_params=_params(("parallel", "parallel")),
        name="dilated_mixer",
    )(z, z, z, z, zt, *biases)


def _sb_kernel(q_ref, k_ref, vt_ref, u_ref, o_ref):
    qi = pl.program_id(1)
    wq = _slot_queries(q_ref[...], SB_HEADS * HEAD_DIM)
    u = u_ref[...]
    shape = (BLK, N_SLOTS * BLK)
    strict = lax.broadcasted_iota(jnp.int32, shape, 0) < lax.broadcasted_iota(jnp.int32, shape, 1) % BLK

    def step(kb, state, diag):
        run = state[0]
        zz = _dot_nt(_key_rows(k_ref, kb), wq)
        sp = jnp.maximum(zz, 0.0) + jnp.log(1.0 + jnp.exp(-jnp.abs(zz)))
        lf = -sp
        if diag:
            lf = jnp.where(strict, lf, 0.0)
        hi = lf.astype(BF16)
        lo = (lf - hi.astype(F32)).astype(BF16)
        after = _dot(u, hi) + _dot(u, lo) + run
        a = jnp.exp((zz - sp) + after)
        if diag:
            a = jnp.where(strict, a, 0.0)
        res = _dot(vt_ref[kb], a.astype(BF16))
        accs = [state[1 + j] + res[j * HEAD_DIM:(j + 1) * HEAD_DIM, j * BLK:(j + 1) * BLK] for j in range(N_SLOTS)]
        return (run + jnp.sum(lf, axis=0, keepdims=True), *accs)

    init = (jnp.zeros((1, N_SLOTS * BLK), F32), *[jnp.zeros((HEAD_DIM, BLK), F32) for _ in range(N_SLOTS)])
    st = step(qi, init, True)
    st = lax.fori_loop(0, qi, lambda t, s: step(qi - 1 - t, s, False), st)
    _store_heads(o_ref, list(st[1:]))


def _stick_breaking(z, zt):
    bsz, seq, _ = z.shape
    nq = seq // BLK
    w = SB_HEADS * HEAD_DIM
    later = (np.arange(BLK)[None, :] > np.arange(BLK)[:, None]).astype(np.float32)
    return pl.pallas_call(
        _sb_kernel,
        grid=(bsz, nq),
        in_specs=[pl.BlockSpec((None, BLK, w), lambda b, i: (b, i, COL_SB_Q // w)),
                  pl.BlockSpec((None, seq, w), lambda b, i: (b, 0, COL_SB_K // w)),
                  pl.BlockSpec((None, nq, w, BLK), lambda b, i: (b, 0, ROW_SB_V // w, 0)),
                  _full((BLK, BLK))],
        out_specs=pl.BlockSpec((None, BLK, w), lambda b, i: (b, i, 0)),
        out_shape=jax.ShapeDtypeStruct((bsz, seq, w), BF16),
        compiler_params=_params(("parallel", "parallel")),
        name="stick_breaking",
    )(z, z, zt, jnp.asarray(later, dtype=BF16))


def _merge_kernel(oa_ref, ob_ref, oc_ref, od_ref, mg_ref, wb_ref, wo_ref, x_ref, g_ref, lng_ref, lnb_ref, o_ref,
                  *, alpha):
    d = x_ref.shape[1]
    merged = jnp.zeros((x_ref.shape[0], d), F32)
    for b, ref in enumerate((oa_ref, ob_ref, oc_ref, od_ref)):
        proj = _dot(ref[...], wb_ref[b])
        merged = merged + jax.nn.sigmoid(mg_ref[:, b * d:(b + 1) * d].astype(F32)) * proj
    y = _dot(merged.astype(BF16), wo_ref[...])
    v = alpha * x_ref[...] + g_ref[...] * y
    o_ref[...] = _ln(v) * lng_ref[...] + lnb_ref[...]


def _merge(branches, z2, wb, wo, x2, mod4, lng, lnb, seq, *, alpha, g_idx, tm=512):
    m, d = x2.shape
    bw = BRANCH_WIDTH
    per = seq // tm
    row = lambda w: pl.BlockSpec((tm, w), lambda i: (i, 0))
    return pl.pallas_call(
        functools.partial(_merge_kernel, alpha=alpha),
        grid=(m // tm,),
        in_specs=[row(bw), row(bw), row(bw), row(bw),
                  pl.BlockSpec((tm, N_BRANCHES * d), lambda i: (i, COL_MG // (N_BRANCHES * d))),
                  _full(wb.shape), _full(wo.shape),
                  row(d),
                  pl.BlockSpec((None, None, 1, d), lambda i: (i // per, g_idx, 0, 0)),
                  _full((1, d)), _full((1, d))],
        out_specs=row(d),
        out_shape=jax.ShapeDtypeStruct((m, d), F32),
        compiler_params=_params(("parallel",)),
        name="merge_out",
    )(*branches, z2, wb, wo, x2, mod4, lng.reshape(1, d), lnb.reshape(1, d))


def _ffn_kernel(x_ref, sc_ref, sh_ref, g_ref, lng_ref, lnb_ref, wg_ref, wu_ref, wd_ref, o_ref, h_ref, acc_ref,
                *, alpha):
    f = pl.program_id(1)

    @pl.when(f == 0)
    def _():
        h_ref[...] = (_ln(x_ref[...]) * (1.0 + sc_ref[...]) + sh_ref[...]).astype(BF16)
        acc_ref[...] = jnp.zeros_like(acc_ref)

    h = h_ref[...]
    a = _dot(h, wg_ref[...])
    u = _dot(h, wu_ref[...])
    act = (a * jax.nn.sigmoid(a)) * u
    acc_ref[...] += _dot(act.astype(BF16), wd_ref[...])

    @pl.when(f == pl.num_programs(1) - 1)
    def _():
        v = alpha * x_ref[...] + g_ref[...] * acc_ref[...]
        o_ref[...] = _ln(v) * lng_ref[...] + lnb_ref[...]


def _ffn(x2, mod4, wg, wu, wd, lng, lnb, seq, *, alpha, tm=512, tf=1408):
    m, d = x2.shape
    ff = wg.shape[1]
    per = seq // tm
    modspec = lambda k: pl.BlockSpec((None, None, 1, d), lambda i, f: (i // per, k, 0, 0))
    return pl.pallas_call(
        functools.partial(_ffn_kernel, alpha=alpha),
        grid=(m // tm, ff // tf),
        in_specs=[pl.BlockSpec((tm, d), lambda i, f: (i, 0)),
                  modspec(4), modspec(3), modspec(5),
                  pl.BlockSpec((1, d), lambda i, f: (0, 0)), pl.BlockSpec((1, d), lambda i, f: (0, 0)),
                  pl.BlockSpec((d, tf), lambda i, f: (0, f)),
                  pl.BlockSpec((d, tf), lambda i, f: (0, f)),
                  pl.BlockSpec((tf, d), lambda i, f: (f, 0))],
        out_specs=pl.BlockSpec((tm, d), lambda i, f: (i, 0)),
        out_shape=jax.ShapeDtypeStruct((m, d), F32),
        scratch_shapes=[pltpu.VMEM((tm, d), BF16), pltpu.VMEM((tm, d), F32)],
        compiler_params=_params(("parallel", "arbitrary")),
        name="ffn_dense",
    )(x2, mod4, mod4, mod4, lng.reshape(1, d), lnb.reshape(1, d), wg, wu, wd)


def _moe_kernel(x_ref, sc_ref, sh_ref, g_ref, lng_ref, lnb_ref, wr_ref, wg_ref, wu_ref, wd_ref, o_ref,
                h_ref, gates_ref, acc_ref, *, alpha):
    e = pl.program_id(1)
    f = pl.program_id(2)
    tm = x_ref.shape[0]
    lane = lax.broadcasted_iota(jnp.int32, (tm, 128), 1).astype(F32)

    @pl.when((e == 0) & (f == 0))
    def _():
        h32 = _ln(x_ref[...]) * (1.0 + sc_ref[...]) + sh_ref[...]
        h_ref[...] = h32.astype(BF16)
        acc_ref[...] = jnp.zeros_like(acc_ref)
        logits = _dot(h32, wr_ref[...], HIGHEST)
        logits = jnp.where(lane < N_EXPERTS, logits, -jnp.inf)
        t1 = jnp.max(logits, axis=1, keepdims=True)
        i1 = jnp.min(jnp.where(logits == t1, lane, 128.0), axis=1, keepdims=True)
        rest = jnp.where(lane == i1, -jnp.inf, logits)
        t2 = jnp.max(rest, axis=1, keepdims=True)
        i2 = jnp.min(jnp.where(rest == t2, lane, 128.0), axis=1, keepdims=True)
        e2 = jnp.exp(t2 - t1)
        den = 1.0 + e2
        gates_ref[...] = jnp.where(lane == i1, 1.0 / den, jnp.where(lane == i2, e2 / den, 0.0))

    gate = jnp.sum(jnp.where(lane == e.astype(F32), gates_ref[...], 0.0), axis=1, keepdims=True)
    h = h_ref[...]
    a = _dot(h, wg_ref[...])
    u = _dot(h, wu_ref[...])
    act = (a * jax.nn.sigmoid(a)) * u
    acc_ref[...] += gate * _dot(act.astype(BF16), wd_ref[...])

    @pl.when((e == pl.num_programs(1) - 1) & (f == pl.num_programs(2) - 1))
    def _():
        v = alpha * x_ref[...] + g_ref[...] * acc_ref[...]
        o_ref[...] = _ln(v) * lng_ref[...] + lnb_ref[...]


def _moe(x2, mod4, wr, wg, wu, wd, lng, lnb, seq, *, alpha, tm=512, tf=1792):
    m, d = x2.shape
    n_exp, _, ff = wg.shape
    per = seq // tm
    wr_pad = jnp.pad(wr, ((0, 0), (0, 128 - n_exp)))
    modspec = lambda k: pl.BlockSpec((None, None, 1, d), lambda i, e, f: (i // per, k, 0, 0))
    return pl.pallas_call(
        functools.partial(_moe_kernel, alpha=alpha),
        grid=(m // tm, n_exp, ff // tf),
        in_specs=[pl.BlockSpec((tm, d), lambda i, e, f: (i, 0)),
                  modspec(4), modspec(3), modspec(5),
                  pl.BlockSpec((1, d), lambda i, e, f: (0, 0)), pl.BlockSpec((1, d), lambda i, e, f: (0, 0)),
                  pl.BlockSpec((d, 128), lambda i, e, f: (0, 0)),
                  pl.BlockSpec((None, d, tf), lambda i, e, f: (e, 0, f)),
                  pl.BlockSpec((None, d, tf), lambda i, e, f: (e, 0, f)),
                  pl.BlockSpec((None, tf, d), lambda i, e, f: (e, f, 0))],
        out_specs=pl.BlockSpec((tm, d), lambda i, e, f: (i, 0)),
        out_shape=jax.ShapeDtypeStruct((m, d), F32),
        scratch_shapes=[pltpu.VMEM((tm, d), BF16), pltpu.VMEM((tm, 128), F32), pltpu.VMEM((tm, d), F32)],
        compiler_params=_params(("parallel", "arbitrary", "arbitrary")),
        name="moe_dense",
    )(x2, mod4, mod4, mod4, lng.reshape(1, d), lnb.reshape(1, d), wr_pad, wg, wu, wd)


def _in_columns():
    cols = np.full((N_COLS,), -1, np.int64)
    scale = np.ones((N_COLS,), np.float32)
    qs = 1.0 / math.sqrt(HEAD_DIM)

    def put(dst, src, width, s=1.0):
        cols[dst:dst + width] = np.arange(src, src + width)
        scale[dst:dst + width] = s

    put(COL_MG, IN_MG, COL_NSA_Q - COL_MG)
    put(COL_NSA_Q, IN_NSA_Q, 256, qs)
    put(COL_NSA_CMP, IN_K_CMP, 64)
    put(COL_NSA_CMP + 64, IN_V_CMP, 64)
    put(COL_NSA_SEL, IN_K_SEL, 64)
    put(COL_NSA_SEL + 64, IN_K_SEL, 64)
    put(COL_NSA_WIN, IN_K_WIN, 64)
    put(COL_NSA_WIN + 64, IN_K_WIN, 64)
    for j, h in enumerate(SWA_SLOT_HEADS):
        put(COL_SWA_Q + j * HEAD_DIM, IN_SWA_Q + h * HEAD_DIM, HEAD_DIM, qs)
    put(COL_SWA_K, IN_SWA_K, 128)
    put(COL_DIL_Q, IN_DIL_Q, 768, qs)
    put(COL_DIL_K, IN_DIL_K, 256)
    put(COL_SB_Q, IN_SB_Q, 256, qs)
    put(COL_SB_K, IN_SB_K, 256)
    return cols, scale


def _in_rows_t():
    rows = np.full((N_ROWS_T,), -1, np.int64)

    def put(dst, src, width):
        rows[dst:dst + width] = np.arange(src, src + width)

    put(ROW_NSA_V, IN_V_SEL, 64)
    put(ROW_NSA_V + 64, IN_V_WIN, 64)
    put(ROW_SWA_V, IN_SWA_V, 128)
    put(ROW_DIL_V, IN_DIL_V, 256)
    put(ROW_SB_V, IN_SB_V, 256)
    put(ROW_GATE, IN_GATE, GATE_COLS)
    return rows


def _relayout_w_in(w_in):
    cols, scale = _in_columns()
    w = jnp.take(w_in, jnp.asarray(np.maximum(cols, 0)), axis=1) * jnp.asarray(np.where(cols >= 0, scale, 0.0))
    rows = _in_rows_t()
    wt = jnp.take(w_in, jnp.asarray(np.maximum(rows, 0)), axis=1) * jnp.asarray((rows >= 0).astype(np.float32))
    return w.astype(BF16), wt.T.astype(BF16)


def kernel(x, c, rel_bias, w_ada, b_ada, w_in, w_branch, w_out, cmp_pe, cmp_w1, cmp_w2, swa_sinks, ln_g, ln_b,
           ffn_w_gate, ffn_w_up, ffn_w_down, moe_router, moe_w_gate, moe_w_up, moe_w_down):
    bsz, seq, d = x.shape
    depth = w_ada.shape[0]
    alpha = (2 * depth) ** 0.25
    m = bsz * seq
    nq = seq // BLK

    nsa_heads = tuple(range(NSA_HEADS))
    bias_sel = _bias_tiles(rel_bias, _bucket_tiles(nq, seq, 1), nsa_heads)
    bias_win = _bias_tiles(rel_bias, _bucket_tiles(-(-(NSA_WINDOW - 1) // BLK) + 1, NSA_WINDOW - 1, 1), nsa_heads)
    bias_swa = _bias_tiles(rel_bias, _bucket_tiles(-(-(SWA_WINDOW - 1) // BLK) + 1, SWA_WINDOW - 1, 1),
                           tuple(NSA_HEADS + h for h in SWA_SLOT_HEADS))
    bias_dil = []
    for gi, (win, dil) in enumerate(DIL_PATTERNS):
        h0 = NSA_HEADS + SWA_HEADS + gi * DIL_HEADS_PER_GROUP
        bias_dil.append(_bias_tiles(rel_bias, _bucket_tiles(min(win // BLK + 1, nq), win, dil),
                                    tuple(range(h0, h0 + DIL_HEADS_PER_GROUP))))
    swa_rows = np.concatenate([np.arange(h * HEAD_DIM, (h + 1) * HEAD_DIM) for h in SWA_SLOT_HEADS])

    x2 = x.reshape(m, d)
    for layer in range(depth):
        mod4 = _ada(c, w_ada[layer], b_ada[layer]).reshape(bsz, 6, 1, d)
        w_nat, w_t = _relayout_w_in(w_in[layer])
        z2 = _inproj(x2, mod4, w_nat, seq, sc_idx=1, sh_idx=0)
        zt = _inproj_t(x2, mod4, w_t, seq, sc_idx=1, sh_idx=0).reshape(bsz, nq, N_ROWS_T, BLK)
        z = z2.reshape(bsz, seq, N_COLS)
        chunks = z[:, :, COL_NSA_CMP:COL_NSA_CMP + 2 * HEAD_DIM].reshape(bsz, seq // CMP_STRIDE,
                                                                        CMP_STRIDE * 2 * HEAD_DIM)
        kc, vct = _compress(chunks, cmp_pe[layer], cmp_w1[layer], cmp_w2[layer])
        o_a = _nsa(z, zt, kc, vct, bias_sel, bias_win)
        o_b = _swa(z, zt, bias_swa, swa_sinks[layer])
        o_c = _dilated(z, zt, bias_dil)
        o_d = _stick_breaking(z, zt)
        wb = w_branch[layer]
        wb = jnp.stack([wb[0], wb[1][swa_rows], wb[2], wb[3]]).astype(BF16)
        branches = [a.reshape(m, BRANCH_WIDTH) for a in (o_a, o_b, o_c, o_d)]
        x2 = _merge(branches, z2, wb, w_out[layer].astype(BF16), x2, mod4, ln_g[layer, 0], ln_b[layer, 0], seq,
                    alpha=alpha, g_idx=2)
        i = layer // 2
        if layer % 2 == 0:
            x2 = _ffn(x2, mod4, ffn_w_gate[i].astype(BF16), ffn_w_up[i].astype(BF16), ffn_w_down[i].astype(BF16),
                      ln_g[layer, 1], ln_b[layer, 1], seq, alpha=alpha)
        else:
            x2 = _moe(x2, mod4, moe_router[i], moe_w_gate[i].astype(BF16), moe_w_up[i].astype(BF16),
                      moe_w_down[i].astype(BF16), ln_g[layer, 1], ln_b[layer, 1], seq, alpha=alpha)
    return x2.reshape(bsz, seq, d)
```

```python
import functools
import math

import numpy as np
import jax
import jax.numpy as jnp
from jax import lax
from jax.experimental import pallas as pl
from jax.experimental.pallas import tpu as pltpu

F32 = jnp.float32
BF16 = jnp.bfloat16
HIGHEST = lax.Precision.HIGHEST

HEAD_DIM = 64
BLK = 128
N_SLOTS = 4
NSA_HEADS = 4
CMP_LEN = 32
CMP_STRIDE = 16
SEL_LEN = 64
N_SEL = 16
NSA_WINDOW = 512
SWA_HEADS = 4
SWA_WINDOW = 128
SWA_SLOT_HEADS = (0, 2, 1, 3)
DIL_PATTERNS = ((128, 1), (512, 4), (2048, 16))
DIL_HEADS_PER_GROUP = 4
SB_HEADS = 4
N_BRANCHES = 4
BRANCH_WIDTH = 4 * HEAD_DIM
NUM_BUCKETS = 32
MAX_DISTANCE = 2048
N_EXPERTS = 8
LN_EPS = 1e-5
NEG_INF = -1e30
TINY = 1e-30
FORCE_SCORE = 1e4

IN_NSA_Q, IN_K_CMP, IN_V_CMP, IN_K_SEL, IN_V_SEL, IN_K_WIN, IN_V_WIN, IN_GATE = 0, 256, 320, 384, 448, 512, 576, 640
IN_SWA_Q, IN_SWA_K, IN_SWA_V = 652, 908, 1036
IN_DIL_Q, IN_DIL_K, IN_DIL_V = 1164, 1932, 2188
IN_SB_Q, IN_SB_K, IN_SB_V = 2444, 2700, 2956
IN_MG = 3212
GATE_COLS = NSA_HEADS * 3

COL_MG = 0
COL_NSA_Q = 4096
COL_NSA_CMP = 4352
COL_NSA_SEL = 4480
COL_NSA_WIN = 4608
COL_NSA_GATE = 4736
COL_SWA_Q = 4864
COL_SWA_K = 5120
COL_DIL_Q = 5376
COL_DIL_K = 6144
COL_SB_Q = 6400
COL_SB_K = 6656
N_COLS = 6912
ROW_NSA_V = 0
ROW_SWA_V = 128
ROW_DIL_V = 256
ROW_SB_V = 512
ROW_GATE = 768
N_ROWS_T = 784

VMEM_LIMIT = 56 * 1024 * 1024
ATTN_BATCH = 4


def _params(sem):
    return pltpu.CompilerParams(dimension_semantics=sem, vmem_limit_bytes=VMEM_LIMIT)


def _ln(x):
    mu = jnp.mean(x, axis=-1, keepdims=True)
    xc = x - mu
    var = jnp.mean(xc * xc, axis=-1, keepdims=True)
    return xc * lax.rsqrt(var + LN_EPS)


def _dot_nt(a, b, precision=None):
    return lax.dot_general(a, b, (((1,), (1,)), ((), ())), preferred_element_type=F32, precision=precision)


def _dot(a, b, precision=None):
    return jnp.dot(a, b, preferred_element_type=F32, precision=precision)


def _full(shape):
    return pl.BlockSpec(shape, lambda *_: (0,) * len(shape))


def _ada_kernel(c_ref, w_ref, b_ref, o_ref):
    c = c_ref[...]
    s = c * jax.nn.sigmoid(c)
    o_ref[...] = _dot(s, w_ref[...], HIGHEST) + b_ref[...]


def _ada(c, w, b):
    bsz, d = c.shape
    n = w.shape[1]
    tn = 1024
    return pl.pallas_call(
        _ada_kernel,
        grid=(n // tn,),
        in_specs=[pl.BlockSpec((bsz, d), lambda j: (0, 0)),
                  pl.BlockSpec((d, tn), lambda j: (0, j)),
                  pl.BlockSpec((1, tn), lambda j: (0, j))],
        out_specs=pl.BlockSpec((bsz, tn), lambda j: (0, j)),
        out_shape=jax.ShapeDtypeStruct((bsz, n), F32),
        compiler_params=_params(("parallel",)),
        name="ada",
    )(c, w, b.reshape(1, n))


def _inproj_kernel(x_ref, sc_ref, sh_ref, w_ref, o_ref):
    h = _ln(x_ref[...]) * (1.0 + sc_ref[...]) + sh_ref[...]
    o_ref[...] = _dot(h.astype(BF16), w_ref[...]).astype(o_ref.dtype)


def _inproj(x2, mod4, w, seq, *, sc_idx, sh_idx, tm=512):
    m, d = x2.shape
    n = w.shape[1]
    tn = n // 2
    per = seq // tm
    return pl.pallas_call(
        _inproj_kernel,
        grid=(n // tn, m // tm),
        in_specs=[pl.BlockSpec((tm, d), lambda j, i: (i, 0)),
                  pl.BlockSpec((None, None, 1, d), lambda j, i: (i // per, sc_idx, 0, 0)),
                  pl.BlockSpec((None, None, 1, d), lambda j, i: (i // per, sh_idx, 0, 0)),
                  pl.BlockSpec((d, tn), lambda j, i: (0, j))],
        out_specs=pl.BlockSpec((tm, tn), lambda j, i: (i, j)),
        out_shape=jax.ShapeDtypeStruct((m, n), BF16),
        compiler_params=_params(("parallel", "parallel")),
        name="inproj",
    )(x2, mod4, mod4, w)


def _inproj_t_kernel(x_ref, sc_ref, sh_ref, w_ref, o_ref):
    h = (_ln(x_ref[...]) * (1.0 + sc_ref[...]) + sh_ref[...]).astype(BF16)
    res = _dot_nt(w_ref[...], h)
    for j in range(o_ref.shape[0]):
        o_ref[j] = res[:, j * BLK:(j + 1) * BLK].astype(o_ref.dtype)


def _inproj_t(x2, mod4, wt, seq, *, sc_idx, sh_idx, tm=512):
    m, d = x2.shape
    rows = wt.shape[0]
    per = seq // tm
    return pl.pallas_call(
        _inproj_t_kernel,
        grid=(m // tm,),
        in_specs=[pl.BlockSpec((tm, d), lambda i: (i, 0)),
                  pl.BlockSpec((None, None, 1, d), lambda i: (i // per, sc_idx, 0, 0)),
                  pl.BlockSpec((None, None, 1, d), lambda i: (i // per, sh_idx, 0, 0)),
                  _full(wt.shape)],
        out_specs=pl.BlockSpec((tm // BLK, rows, BLK), lambda i: (i, 0, 0)),
        out_shape=jax.ShapeDtypeStruct((m // BLK, rows, BLK), BF16),
        compiler_params=_params(("parallel",)),
        name="inproj_t",
    )(x2, mod4, mod4, wt)


def _np_bucket(dist):
    dist = np.maximum(dist, 0)
    max_exact = NUM_BUCKETS // 2
    d_f = np.maximum(dist, 1).astype(np.float32)
    large = max_exact + (np.log(d_f / np.float32(max_exact)) / np.float32(math.log(MAX_DISTANCE / max_exact))
                         * np.float32(NUM_BUCKETS - max_exact)).astype(np.int32)
    large = np.minimum(large, NUM_BUCKETS - 1)
    return np.where(dist < max_exact, dist, large).astype(np.int32)


def _bucket_tiles(n_off, max_dist, dil):
    b = np.arange(BLK)[:, None]
    a = np.arange(BLK)[None, :]
    tiles = []
    for o in range(n_off):
        dist = o * BLK + a - b
        ok = (dist >= 0) & (dist <= max_dist) & (dist % dil == 0)
        tiles.append(np.where(ok, _np_bucket(dist), -1))
    return np.stack(tiles).astype(np.int32)


def _bias_kernel(tbl_ref, bk_ref, o_ref, *, heads):
    slot = pl.program_id(0)
    h = jnp.int32(heads[0])
    for j in range(1, len(heads)):
        h = jnp.where(slot == j, heads[j], h)
    b = bk_ref[...]
    out = jnp.where(b < 0, NEG_INF, 0.0).astype(F32)
    for k in range(NUM_BUCKETS):
        out = jnp.where(b == k, tbl_ref[k, h], out)
    o_ref[...] = out


def _bias_tiles(rel_bias, buckets, heads):
    n_off = buckets.shape[0]
    return pl.pallas_call(
        functools.partial(_bias_kernel, heads=tuple(heads)),
        grid=(len(heads),),
        in_specs=[pl.BlockSpec(memory_space=pltpu.SMEM), _full((n_off, BLK, BLK))],
        out_specs=pl.BlockSpec((n_off, BLK, BLK), lambda s: (0, 0, s)),
        out_shape=jax.ShapeDtypeStruct((n_off, BLK, len(heads) * BLK), F32),
        compiler_params=_params(("parallel",)),
        name="bias_tiles",
    )(rel_bias, jnp.asarray(buckets))


def _slot_queries(q, kw):
    lane = lax.broadcasted_iota(jnp.int32, (BLK, kw), 1)
    zero = jnp.zeros((BLK, kw), q.dtype)
    parts = []
    for j in range(N_SLOTS):
        lo = j * HEAD_DIM
        grp = q[:, (lo // kw) * kw:(lo // kw + 1) * kw]
        inside = (lane >= lo % kw) & (lane < lo % kw + HEAD_DIM)
        parts.append(jnp.where(inside, grp, zero))
    return jnp.concatenate(parts, axis=0)


def _softmax_step(s, vt, state, slot_rows):
    m, l = state[0], state[1]
    m_new = jnp.maximum(m, jnp.max(s, axis=0, keepdims=True))
    alpha = jnp.exp(m - m_new)
    p = jnp.exp(s - m_new)
    l = alpha * l + jnp.sum(p, axis=0, keepdims=True)
    yield
    res = _dot(vt, p.astype(BF16))
    yield
    accs = []
    for j in range(N_SLOTS):
        r0 = slot_rows[j]
        accs.append(alpha[:, j * BLK:(j + 1) * BLK] * state[2 + j]
                    + res[r0:r0 + HEAD_DIM, j * BLK:(j + 1) * BLK])
    return (m_new, l, *accs)


def _softmax_init():
    return (jnp.full((1, N_SLOTS * BLK), NEG_INF, F32), jnp.zeros((1, N_SLOTS * BLK), F32),
            *[jnp.zeros((HEAD_DIM, BLK), F32) for _ in range(N_SLOTS)])


def _store_heads(o_ref, outs):
    o_ref[...] = jnp.transpose(jnp.concatenate(outs, axis=0)).astype(o_ref.dtype)


def _key_rows(k_ref, kb):
    return k_ref[pl.ds(pl.multiple_of(kb * BLK, BLK), BLK), :]


def _slot_split(x, slot_rows):
    return [x[slot_rows[j]:slot_rows[j] + HEAD_DIM, j * BLK:(j + 1) * BLK] for j in range(N_SLOTS)]


def _window_softmax(qi, wq, k_ref, vt_of, bias_ref, n_off, slot_rows):
    tiles = []
    m = None
    for o in range(n_off):
        kb = qi - o
        kbc = jnp.maximum(kb, 0)
        s = _dot_nt(_key_rows(k_ref, kbc), wq) + bias_ref[o]
        if o > 0:
            s = s + jnp.where(kb >= 0, 0.0, NEG_INF)
        tiles.append((kbc, s))
        mo = jnp.max(s, axis=0, keepdims=True)
        m = mo if m is None else jnp.maximum(m, mo)
    l = None
    res = None
    for kbc, s in tiles:
        p = jnp.exp(s - m)
        lo = jnp.sum(p, axis=0, keepdims=True)
        ro = _dot(vt_of(kbc), p.astype(BF16))
        l = lo if l is None else l + lo
        res = ro if res is None else res + ro
    return (m, l, *_slot_split(res, slot_rows))


def _run_interleaved(gens):
    results = [None] * len(gens)
    live = list(range(len(gens)))
    while live:
        for i in tuple(live):
            try:
                next(gens[i])
            except StopIteration as done:
                results[i] = done.value
                live.remove(i)
    return results


def _chained_loop(lo, hi, steps, states):
    def body(kb, sts):
        return tuple(_run_interleaved([f(kb, st) for f, st in zip(steps, sts)]))
    return lax.fori_loop(lo, hi, body, tuple(states))


def _compress_kernel(ch_ref, wa_ref, wb_ref, pea_ref, peb_ref, w2k_ref, w2vt_ref, kc_ref, vct_ref):
    ch = ch_ref[...].astype(F32)
    wa = wa_ref[...]
    wb = wb_ref[...]
    first = _dot(ch, wa, HIGHEST)
    second = _dot(ch, wb, HIGHEST)
    pe = (_dot(pea_ref[...], wa, HIGHEST) + _dot(peb_ref[...], wb, HIGHEST))[0:1]
    n = second.shape[0]
    nxt = pltpu.roll(second, n - 1, 0)
    hid = jax.nn.gelu(first + nxt + pe)
    kc_ref[...] = _dot(hid, w2k_ref[...], HIGHEST).astype(kc_ref.dtype)
    vct_ref[...] = _dot_nt(w2vt_ref[...], hid, HIGHEST).astype(vct_ref.dtype)


def _compress(chunks, cmp_pe, cmp_w1, cmp_w2):
    bsz, n_chunk, width = chunks.shape
    per = CMP_LEN // 2
    hd = HEAD_DIM
    w1 = cmp_w1.reshape(2, CMP_LEN, hd, hd)
    wexp = jnp.einsum('cpdn,ce->pcden', w1, jnp.eye(2, dtype=F32)).reshape(CMP_LEN, 2 * hd, 2 * hd)
    wa = wexp[:per].reshape(width, 2 * hd)
    wb = wexp[per:].reshape(width, 2 * hd)
    pe = jnp.transpose(cmp_pe, (1, 0, 2)).reshape(CMP_LEN, 2 * hd)
    pea = jnp.broadcast_to(pe[:per].reshape(1, width), (8, width))
    peb = jnp.broadcast_to(pe[per:].reshape(1, width), (8, width))
    zero = jnp.zeros((hd, hd), F32)
    w2k = jnp.concatenate([jnp.concatenate([cmp_w2[0], cmp_w2[0]], 1), jnp.concatenate([zero, zero], 1)], 0)
    w2vt = jnp.concatenate([zero, cmp_w2[1].T], 1)
    return pl.pallas_call(
        _compress_kernel,
        grid=(bsz,),
        in_specs=[pl.BlockSpec((None, n_chunk, width), lambda b: (b, 0, 0)),
                  _full(wa.shape), _full(wb.shape), _full(pea.shape), _full(peb.shape), _full(w2k.shape),
                  _full(w2vt.shape)],
        out_specs=(pl.BlockSpec((None, n_chunk, 2 * hd), lambda b: (b, 0, 0)),
                   pl.BlockSpec((None, hd, n_chunk), lambda b: (b, 0, 0))),
        out_shape=(jax.ShapeDtypeStruct((bsz, n_chunk, 2 * hd), BF16),
                   jax.ShapeDtypeStruct((bsz, hd, n_chunk), BF16)),
        compiler_params=_params(("parallel",)),
        name="nsa_compress",
    )(chunks, wa, wb, pea, peb, w2k, w2vt)


def _nsa_kernel(q_ref, kc_ref, vct_ref, ovt_ref, ksel_ref, kwin_ref, vt_ref, gate_ref, bsel_ref, bwin_ref, et_ref,
                o_ref, *, n_blk, n_win_off):
    qi = pl.program_id(1)
    t0 = qi * BLK
    nb = q_ref.shape[0]
    n_cmp = kc_ref.shape[1]
    slot_rows = (0,) * N_SLOTS
    shape = (n_cmp, N_SLOTS * BLK)
    tq = t0 + lax.broadcasted_iota(jnp.int32, shape, 1) % BLK
    ci = lax.broadcasted_iota(jnp.int32, shape, 0)
    ok = (ci * CMP_STRIDE + (CMP_LEN - 1)) <= tq
    jb = lax.broadcasted_iota(jnp.int32, (n_blk, BLK), 0)
    cur = (t0 + lax.broadcasted_iota(jnp.int32, (n_blk, BLK), 1)) // SEL_LEN
    forced = (jb == 0) | (jb == cur) | (jb == cur - 1)

    o_cmps, o_wins, sel_steps = [], [], []
    for bb in range(nb):
        wq = _slot_queries(q_ref[bb], 2 * HEAD_DIM)

        s = jnp.where(ok, _dot_nt(kc_ref[bb], wq), NEG_INF)
        m = jnp.max(s, axis=0, keepdims=True)
        p = jnp.where(ok, jnp.exp(s - m), 0.0)
        p = p / jnp.maximum(jnp.sum(p, axis=0, keepdims=True), TINY)
        o_cmps.append(_dot(vct_ref[bb], p.astype(BF16)))
        psum = p[:, 0:BLK]
        for j in range(1, N_SLOTS):
            psum = psum + p[:, j * BLK:(j + 1) * BLK]

        hi = psum.astype(BF16)
        lo = (psum - hi.astype(F32)).astype(BF16)
        score = _dot(ovt_ref[...], hi) + _dot(ovt_ref[...], lo)
        score = jnp.where(forced, FORCE_SCORE, jnp.where(jb > cur, NEG_INF, score))
        rank = jnp.zeros((n_blk, BLK), F32)
        for c in range(n_blk):
            row = score[c:c + 1, :]
            beats = (row > score) | ((row == score) & (jb > c))
            rank = rank + jnp.where(beats, 1.0, 0.0)
        sel = jnp.where(rank < N_SEL, 1.0, 0.0).astype(BF16)

        st = _window_softmax(qi, wq, kwin_ref.at[bb], lambda kb, bb=bb: vt_ref[bb, kb, HEAD_DIM:2 * HEAD_DIM, :],
                             bwin_ref, n_win_off, slot_rows)
        o_wins.append([st[2 + j] / st[1][:, j * BLK:(j + 1) * BLK] for j in range(N_SLOTS)])

        def sel_step(kb, state, bb=bb, wq=wq, sel=sel):
            pen = (_dot(et_ref[kb], sel) - 1.0) * (-NEG_INF)
            pen = jnp.concatenate([pen] * N_SLOTS, axis=1)
            sc = _dot_nt(_key_rows(ksel_ref.at[bb], kb), wq) + bsel_ref[qi - kb] + pen
            yield
            return (yield from _softmax_step(sc, vt_ref[bb, kb, 0:HEAD_DIM, :], state, slot_rows))

        sel_steps.append(sel_step)

    sel_states = _run_interleaved([f(qi, _softmax_init()) for f in sel_steps])
    sel_states = _chained_loop(0, qi, sel_steps, sel_states)
    for bb in range(nb):
        st = sel_states[bb]
        gate = jax.nn.sigmoid(gate_ref[bb].astype(F32))
        outs = []
        for j in range(N_SLOTS):
            o_sel = st[2 + j] / st[1][:, j * BLK:(j + 1) * BLK]
            outs.append(gate[3 * j:3 * j + 1] * o_cmps[bb][:, j * BLK:(j + 1) * BLK]
                        + gate[3 * j + 1:3 * j + 2] * o_sel + gate[3 * j + 2:3 * j + 3] * o_wins[bb][j])
        _store_heads(o_ref.at[bb], outs)


def _nsa(z, zt, kc, vct, bias_sel, bias_win, nb=ATTN_BATCH):
    bsz, seq, _ = z.shape
    nq = seq // BLK
    n_cmp = kc.shape[1]
    n_blk = seq // SEL_LEN
    assert n_blk >= N_SEL
    ci = np.arange(n_cmp)[None, :]
    sj = np.arange(n_blk)[:, None]
    overlap_t = ((ci * CMP_STRIDE + CMP_LEN - 1 >= sj * SEL_LEN) & (ci * CMP_STRIDE < (sj + 1) * SEL_LEN)
                 & (ci < n_cmp - 1)).astype(np.float32)
    key_blk = (np.arange(seq) // SEL_LEN).reshape(nq, BLK, 1)
    expand_t = (key_blk == np.arange(n_blk)[None, None, :]).astype(np.float32)
    qw = NSA_HEADS * HEAD_DIM
    kw = 2 * HEAD_DIM
    return pl.pallas_call(
        functools.partial(_nsa_kernel, n_blk=n_blk, n_win_off=bias_win.shape[0]),
        grid=(bsz // nb, nq),
        in_specs=[pl.BlockSpec((nb, BLK, qw), lambda b, i: (b, i, COL_NSA_Q // qw)),
                  pl.BlockSpec((nb, n_cmp, kw), lambda b, i: (b, 0, 0)),
                  pl.BlockSpec((nb, HEAD_DIM, n_cmp), lambda b, i: (b, 0, 0)),
                  _full(overlap_t.shape),
                  pl.BlockSpec((nb, seq, kw), lambda b, i: (b, 0, COL_NSA_SEL // kw)),
                  pl.BlockSpec((nb, seq, kw), lambda b, i: (b, 0, COL_NSA_WIN // kw)),
                  pl.BlockSpec((nb, nq, kw, BLK), lambda b, i: (b, 0, ROW_NSA_V // kw, 0)),
                  pl.BlockSpec((nb, None, 16, BLK), lambda b, i: (b, i, ROW_GATE // 16, 0)),
                  _full(bias_sel.shape), _full(bias_win.shape), _full(expand_t.shape)],
        out_specs=pl.BlockSpec((nb, BLK, qw), lambda b, i: (b, i, 0)),
        out_shape=jax.ShapeDtypeStruct((bsz, seq, qw), BF16),
        compiler_params=_params(("parallel", "parallel")),
        name="nsa_mixer",
    )(z, kc, vct, jnp.asarray(overlap_t, dtype=BF16), z, z, zt, zt, bias_sel, bias_win,
      jnp.asarray(expand_t, dtype=BF16))


def _swa_kernel(q_ref, k_ref, vt_ref, bias_ref, sink_ref, o_ref, *, n_off):
    qi = pl.program_id(1)
    slot_rows = tuple((j % 2) * HEAD_DIM for j in range(N_SLOTS))
    for bb in range(q_ref.shape[0]):
        wq = _slot_queries(q_ref[bb], 2 * HEAD_DIM)
        st = _window_softmax(qi, wq, k_ref.at[bb], lambda kb, bb=bb: vt_ref[bb, kb], bias_ref, n_off, slot_rows)
        m, l = st[0], st[1]
        outs = []
        for j in range(N_SLOTS):
            sink = sink_ref[SWA_SLOT_HEADS[j]]
            mj = m[:, j * BLK:(j + 1) * BLK]
            lj = l[:, j * BLK:(j + 1) * BLK]
            m2 = jnp.maximum(mj, sink)
            scale = jnp.exp(mj - m2)
            den = lj * scale + jnp.exp(sink - m2)
            outs.append(st[2 + j] * (scale / den))
        _store_heads(o_ref.at[bb], outs)


def _swa(z, zt, bias, sinks, nb=ATTN_BATCH):
    bsz, seq, _ = z.shape
    nq = seq // BLK
    qw = SWA_HEADS * HEAD_DIM
    kw = 2 * HEAD_DIM
    return pl.pallas_call(
        functools.partial(_swa_kernel, n_off=bias.shape[0]),
        grid=(bsz // nb, nq),
        in_specs=[pl.BlockSpec((nb, BLK, qw), lambda b, i: (b, i, COL_SWA_Q // qw)),
                  pl.BlockSpec((nb, seq, kw), lambda b, i: (b, 0, COL_SWA_K // kw)),
                  pl.BlockSpec((nb, nq, kw, BLK), lambda b, i: (b, 0, ROW_SWA_V // kw, 0)),
                  _full(bias.shape),
                  pl.BlockSpec(memory_space=pltpu.SMEM)],
        out_specs=pl.BlockSpec((nb, BLK, qw), lambda b, i: (b, i, 0)),
        out_shape=jax.ShapeDtypeStruct((bsz, seq, qw), BF16),
        compiler_params=_params(("parallel", "parallel")),
        name="swa_mixer",
    )(z, z, zt, bias, sinks)


def _dil_kernel(q0_ref, q1_ref, q2_ref, k_ref, vt_ref, b0_ref, b1_ref, b2_ref, o_ref):
    qi = pl.program_id(1)
    nb = q0_ref.shape[0]
    slot_rows = tuple(j * HEAD_DIM for j in range(N_SLOTS))
    kw = N_SLOTS * HEAD_DIM
    short, far_steps = [], []
    for bb in range(nb):
        vt_of = lambda kb, bb=bb: vt_ref[bb, kb]
        short.append([_window_softmax(qi, _slot_queries(q_ref[bb], kw), k_ref.at[bb], vt_of, bias_ref,
                                      bias_ref.shape[0], slot_rows)
                      for q_ref, bias_ref in ((q0_ref, b0_ref), (q1_ref, b1_ref))])
        wq2 = _slot_queries(q2_ref[bb], kw)

        def far_step(kb, state, bb=bb, wq2=wq2, vt_of=vt_of):
            sc = _dot_nt(_key_rows(k_ref.at[bb], kb), wq2) + b2_ref[qi - kb]
            yield
            return (yield from _softmax_step(sc, vt_of(kb), state, slot_rows))

        far_steps.append(far_step)
    far_states = _run_interleaved([f(qi, _softmax_init()) for f in far_steps])
    far_states = _chained_loop(jnp.maximum(qi - (b2_ref.shape[0] - 1), 0), qi, far_steps, far_states)
    for bb in range(nb):
        groups = short[bb] + [far_states[bb]]
        lses = [st[0] + jnp.log(st[1]) for st in groups]
        top = jnp.maximum(jnp.maximum(lses[0], lses[1]), lses[2])
        es = [jnp.exp(v - top) for v in lses]
        den = es[0] + es[1] + es[2]
        outs = []
        for j in range(N_SLOTS):
            sl = slice(j * BLK, (j + 1) * BLK)
            acc = jnp.zeros((HEAD_DIM, BLK), F32)
            for st, e in zip(groups, es):
                acc = acc + (e[:, sl] / den[:, sl]) * (st[2 + j] / st[1][:, sl])
            outs.append(acc)
        _store_heads(o_ref.at[bb], outs)


def _dilated(z, zt, biases, nb=ATTN_BATCH):
    bsz, seq, _ = z.shape
    nq = seq // BLK
    qw = DIL_HEADS_PER_GROUP * HEAD_DIM
    qspec = lambda g: pl.BlockSpec((nb, BLK, qw), lambda b, i: (b, i, COL_DIL_Q // qw + g))
    return pl.pallas_call(
        _dil_kernel,
        grid=(bsz // nb, nq),
        in_specs=[qspec(0), qspec(1), qspec(2),
                  pl.BlockSpec((nb, seq, qw), lambda b, i: (b, 0, COL_DIL_K // qw)),
                  pl.BlockSpec((nb, nq, qw, BLK), lambda b, i: (b, 0, ROW_DIL_V // qw, 0)),
                  _full(biases[0].shape), _full(biases[1].shape), _full(biases[2].shape)],
        out_specs=pl.BlockSpec((nb, BLK, qw), lambda b, i: (b, i, 0)),
        out_shape=jax.ShapeDtypeStruct((bsz, seq, qw), BF16),
        compiler_params=_params(("parallel", "parallel")),
        name="dilated_mixer",
    )(z, z, z, z, zt, *biases)


def _sb_kernel(q_ref, k_ref, vt_ref, u_ref, o_ref):
    qi = pl.program_id(1)
    nb = q_ref.shape[0]
    u = u_ref[...]
    shape = (BLK, N_SLOTS * BLK)
    strict = lax.broadcasted_iota(jnp.int32, shape, 0) < lax.broadcasted_iota(jnp.int32, shape, 1) % BLK

    def step(bb, wq, kb, state, diag):
        run = state[0]
        zz = _dot_nt(_key_rows(k_ref.at[bb], kb), wq)
        yield
        sp = jnp.maximum(zz, 0.0) + jnp.log(1.0 + jnp.exp(-jnp.abs(zz)))
        lf = -sp
        if diag:
            lf = jnp.where(strict, lf, 0.0)
        hi = lf.astype(BF16)
        lo = (lf - hi.astype(F32)).astype(BF16)
        after = _dot(u, hi) + _dot(u, lo) + run
        yield
        a = jnp.exp((zz - sp) + after)
        if diag:
            a = jnp.where(strict, a, 0.0)
        res = _dot(vt_ref[bb, kb], a.astype(BF16))
        yield
        accs = [state[1 + j] + res[j * HEAD_DIM:(j + 1) * HEAD_DIM, j * BLK:(j + 1) * BLK] for j in range(N_SLOTS)]
        return (run + jnp.sum(lf, axis=0, keepdims=True), *accs)

    init = (jnp.zeros((1, N_SLOTS * BLK), F32), *[jnp.zeros((HEAD_DIM, BLK), F32) for _ in range(N_SLOTS)])
    wqs = [_slot_queries(q_ref[bb], SB_HEADS * HEAD_DIM) for bb in range(nb)]
    steps = [lambda t, st, bb=bb: step(bb, wqs[bb], qi - 1 - t, st, False) for bb in range(nb)]
    states = _run_interleaved([step(bb, wqs[bb], qi, init, True) for bb in range(nb)])
    states = _chained_loop(0, qi, steps, states)
    for bb in range(nb):
        _store_heads(o_ref.at[bb], list(states[bb][1:]))


def _stick_breaking(z, zt, nb=ATTN_BATCH):
    bsz, seq, _ = z.shape
    nq = seq // BLK
    w = SB_HEADS * HEAD_DIM
    later = (np.arange(BLK)[None, :] > np.arange(BLK)[:, None]).astype(np.float32)
    return pl.pallas_call(
        _sb_kernel,
        grid=(bsz // nb, nq),
        in_specs=[pl.BlockSpec((nb, BLK, w), lambda b, i: (b, i, COL_SB_Q // w)),
                  pl.BlockSpec((nb, seq, w), lambda b, i: (b, 0, COL_SB_K // w)),
                  pl.BlockSpec((nb, nq, w, BLK), lambda b, i: (b, 0, ROW_SB_V // w, 0)),
                  _full((BLK, BLK))],
        out_specs=pl.BlockSpec((nb, BLK, w), lambda b, i: (b, i, 0)),
        out_shape=jax.ShapeDtypeStruct((bsz, seq, w), BF16),
        compiler_params=_params(("parallel", "parallel")),
        name="stick_breaking",
    )(z, z, zt, jnp.asarray(later, dtype=BF16))


def _merge_kernel(oa_ref, ob_ref, oc_ref, od_ref, mg_ref, wb_ref, wo_ref, x_ref, g_ref, lng_ref, lnb_ref, o_ref,
                  *, alpha):
    d = x_ref.shape[1]
    merged = jnp.zeros((x_ref.shape[0], d), F32)
    for b, ref in enumerate((oa_ref, ob_ref, oc_ref, od_ref)):
        proj = _dot(ref[...], wb_ref[b])
        merged = merged + jax.nn.sigmoid(mg_ref[:, b * d:(b + 1) * d].astype(F32)) * proj
    y = _dot(merged.astype(BF16), wo_ref[...])
    v = alpha * x_ref[...] + g_ref[...] * y
    o_ref[...] = _ln(v) * lng_ref[...] + lnb_ref[...]


def _merge(branches, z2, wb, wo, x2, mod4, lng, lnb, seq, *, alpha, g_idx, tm=512):
    m, d = x2.shape
    bw = BRANCH_WIDTH
    per = seq // tm
    row = lambda w: pl.BlockSpec((tm, w), lambda i: (i, 0))
    return pl.pallas_call(
        functools.partial(_merge_kernel, alpha=alpha),
        grid=(m // tm,),
        in_specs=[row(bw), row(bw), row(bw), row(bw),
                  pl.BlockSpec((tm, N_BRANCHES * d), lambda i: (i, COL_MG // (N_BRANCHES * d))),
                  _full(wb.shape), _full(wo.shape),
                  row(d),
                  pl.BlockSpec((None, None, 1, d), lambda i: (i // per, g_idx, 0, 0)),
                  _full((1, d)), _full((1, d))],
        out_specs=row(d),
        out_shape=jax.ShapeDtypeStruct((m, d), F32),
        compiler_params=_params(("parallel",)),
        name="merge_out",
    )(*branches, z2, wb, wo, x2, mod4, lng.reshape(1, d), lnb.reshape(1, d))


def _ffn_kernel(x_ref, sc_ref, sh_ref, g_ref, lng_ref, lnb_ref, wg_ref, wu_ref, wd_ref, o_ref, h_ref, acc_ref,
                *, alpha):
    f = pl.program_id(1)

    @pl.when(f == 0)
    def _():
        h_ref[...] = (_ln(x_ref[...]) * (1.0 + sc_ref[...]) + sh_ref[...]).astype(BF16)
        acc_ref[...] = jnp.zeros_like(acc_ref)

    h = h_ref[...]
    a = _dot(h, wg_ref[...])
    u = _dot(h, wu_ref[...])
    act = (a * jax.nn.sigmoid(a)) * u
    acc_ref[...] += _dot(act.astype(BF16), wd_ref[...])

    @pl.when(f == pl.num_programs(1) - 1)
    def _():
        v = alpha * x_ref[...] + g_ref[...] * acc_ref[...]
        o_ref[...] = _ln(v) * lng_ref[...] + lnb_ref[...]


def _ffn(x2, mod4, wg, wu, wd, lng, lnb, seq, *, alpha, tm=512, tf=1408):
    m, d = x2.shape
    ff = wg.shape[1]
    per = seq // tm
    modspec = lambda k: pl.BlockSpec((None, None, 1, d), lambda i, f: (i // per, k, 0, 0))
    return pl.pallas_call(
        functools.partial(_ffn_kernel, alpha=alpha),
        grid=(m // tm, ff // tf),
        in_specs=[pl.BlockSpec((tm, d), lambda i, f: (i, 0)),
                  modspec(4), modspec(3), modspec(5),
                  pl.BlockSpec((1, d), lambda i, f: (0, 0)), pl.BlockSpec((1, d), lambda i, f: (0, 0)),
                  pl.BlockSpec((d, tf), lambda i, f: (0, f)),
                  pl.BlockSpec((d, tf), lambda i, f: (0, f)),
                  pl.BlockSpec((tf, d), lambda i, f: (f, 0))],
        out_specs=pl.BlockSpec((tm, d), lambda i, f: (i, 0)),
        out_shape=jax.ShapeDtypeStruct((m, d), F32),
        scratch_shapes=[pltpu.VMEM((tm, d), BF16), pltpu.VMEM((tm, d), F32)],
        compiler_params=_params(("parallel", "arbitrary")),
        name="ffn_dense",
    )(x2, mod4, mod4, mod4, lng.reshape(1, d), lnb.reshape(1, d), wg, wu, wd)


def _moe_kernel(x_ref, sc_ref, sh_ref, g_ref, lng_ref, lnb_ref, wr_ref, wg_ref, wu_ref, wd_ref, o_ref,
                h_ref, gates_ref, acc_ref, *, alpha):
    e = pl.program_id(1)
    f = pl.program_id(2)
    tm = x_ref.shape[0]
    lane = lax.broadcasted_iota(jnp.int32, (tm, 128), 1).astype(F32)

    @pl.when((e == 0) & (f == 0))
    def _():
        h32 = _ln(x_ref[...]) * (1.0 + sc_ref[...]) + sh_ref[...]
        h_ref[...] = h32.astype(BF16)
        acc_ref[...] = jnp.zeros_like(acc_ref)
        logits = _dot(h32, wr_ref[...], HIGHEST)
        logits = jnp.where(lane < N_EXPERTS, logits, -jnp.inf)
        t1 = jnp.max(logits, axis=1, keepdims=True)
        i1 = jnp.min(jnp.where(logits == t1, lane, 128.0), axis=1, keepdims=True)
        rest = jnp.where(lane == i1, -jnp.inf, logits)
        t2 = jnp.max(rest, axis=1, keepdims=True)
        i2 = jnp.min(jnp.where(rest == t2, lane, 128.0), axis=1, keepdims=True)
        e2 = jnp.exp(t2 - t1)
        den = 1.0 + e2
        gates_ref[...] = jnp.where(lane == i1, 1.0 / den, jnp.where(lane == i2, e2 / den, 0.0))

    gate = jnp.sum(jnp.where(lane == e.astype(F32), gates_ref[...], 0.0), axis=1, keepdims=True)
    h = h_ref[...]
    a = _dot(h, wg_ref[...])
    u = _dot(h, wu_ref[...])
    act = (a * jax.nn.sigmoid(a)) * u
    acc_ref[...] += gate * _dot(act.astype(BF16), wd_ref[...])

    @pl.when((e == pl.num_programs(1) - 1) & (f == pl.num_programs(2) - 1))
    def _():
        v = alpha * x_ref[...] + g_ref[...] * acc_ref[...]
        o_ref[...] = _ln(v) * lng_ref[...] + lnb_ref[...]


def _moe(x2, mod4, wr, wg, wu, wd, lng, lnb, seq, *, alpha, tm=512, tf=1792):
    m, d = x2.shape
    n_exp, _, ff = wg.shape
    per = seq // tm
    wr_pad = jnp.pad(wr, ((0, 0), (0, 128 - n_exp)))
    modspec = lambda k: pl.BlockSpec((None, None, 1, d), lambda i, e, f: (i // per, k, 0, 0))
    return pl.pallas_call(
        functools.partial(_moe_kernel, alpha=alpha),
        grid=(m // tm, n_exp, ff // tf),
        in_specs=[pl.BlockSpec((tm, d), lambda i, e, f: (i, 0)),
                  modspec(4), modspec(3), modspec(5),
                  pl.BlockSpec((1, d), lambda i, e, f: (0, 0)), pl.BlockSpec((1, d), lambda i, e, f: (0, 0)),
                  pl.BlockSpec((d, 128), lambda i, e, f: (0, 0)),
                  pl.BlockSpec((None, d, tf), lambda i, e, f: (e, 0, f)),
                  pl.BlockSpec((None, d, tf), lambda i, e, f: (e, 0, f)),
                  pl.BlockSpec((None, tf, d), lambda i, e, f: (e, f, 0))],
        out_specs=pl.BlockSpec((tm, d), lambda i, e, f: (i, 0)),
        out_shape=jax.ShapeDtypeStruct((m, d), F32),
        scratch_shapes=[pltpu.VMEM((tm, d), BF16), pltpu.VMEM((tm, 128), F32), pltpu.VMEM((tm, d), F32)],
        compiler_params=_params(("parallel", "arbitrary", "arbitrary")),
        name="moe_dense",
    )(x2, mod4, mod4, mod4, lng.reshape(1, d), lnb.reshape(1, d), wr_pad, wg, wu, wd)


def _in_columns():
    cols = np.full((N_COLS,), -1, np.int64)
    scale = np.ones((N_COLS,), np.float32)
    qs = 1.0 / math.sqrt(HEAD_DIM)

    def put(dst, src, width, s=1.0):
        cols[dst:dst + width] = np.arange(src, src + width)
        scale[dst:dst + width] = s

    put(COL_MG, IN_MG, COL_NSA_Q - COL_MG)
    put(COL_NSA_Q, IN_NSA_Q, 256, qs)
    put(COL_NSA_CMP, IN_K_CMP, 64)
    put(COL_NSA_CMP + 64, IN_V_CMP, 64)
    put(COL_NSA_SEL, IN_K_SEL, 64)
    put(COL_NSA_SEL + 64, IN_K_SEL, 64)
    put(COL_NSA_WIN, IN_K_WIN, 64)
    put(COL_NSA_WIN + 64, IN_K_WIN, 64)
    for j, h in enumerate(SWA_SLOT_HEADS):
        put(COL_SWA_Q + j * HEAD_DIM, IN_SWA_Q + h * HEAD_DIM, HEAD_DIM, qs)
    put(COL_SWA_K, IN_SWA_K, 128)
    put(COL_DIL_Q, IN_DIL_Q, 768, qs)
    put(COL_DIL_K, IN_DIL_K, 256)
    put(COL_SB_Q, IN_SB_Q, 256, qs)
    put(COL_SB_K, IN_SB_K, 256)
    return cols, scale


def _in_rows_t():
    rows = np.full((N_ROWS_T,), -1, np.int64)

    def put(dst, src, width):
        rows[dst:dst + width] = np.arange(src, src + width)

    put(ROW_NSA_V, IN_V_SEL, 64)
    put(ROW_NSA_V + 64, IN_V_WIN, 64)
    put(ROW_SWA_V, IN_SWA_V, 128)
    put(ROW_DIL_V, IN_DIL_V, 256)
    put(ROW_SB_V, IN_SB_V, 256)
    put(ROW_GATE, IN_GATE, GATE_COLS)
    return rows


def _relayout_w_in(w_in):
    cols, scale = _in_columns()
    w = jnp.take(w_in, jnp.asarray(np.maximum(cols, 0)), axis=1) * jnp.asarray(np.where(cols >= 0, scale, 0.0))
    rows = _in_rows_t()
    wt = jnp.take(w_in, jnp.asarray(np.maximum(rows, 0)), axis=1) * jnp.asarray((rows >= 0).astype(np.float32))
    return w.astype(BF16), wt.T.astype(BF16)


def kernel(x, c, rel_bias, w_ada, b_ada, w_in, w_branch, w_out, cmp_pe, cmp_w1, cmp_w2, swa_sinks, ln_g, ln_b,
           ffn_w_gate, ffn_w_up, ffn_w_down, moe_router, moe_w_gate, moe_w_up, moe_w_down):
    bsz, seq, d = x.shape
    depth = w_ada.shape[0]
    alpha = (2 * depth) ** 0.25
    m = bsz * seq
    nq = seq // BLK

    nsa_heads = tuple(range(NSA_HEADS))
    bias_sel = _bias_tiles(rel_bias, _bucket_tiles(nq, seq, 1), nsa_heads)
    bias_win = _bias_tiles(rel_bias, _bucket_tiles(-(-(NSA_WINDOW - 1) // BLK) + 1, NSA_WINDOW - 1, 1), nsa_heads)
    bias_swa = _bias_tiles(rel_bias, _bucket_tiles(-(-(SWA_WINDOW - 1) // BLK) + 1, SWA_WINDOW - 1, 1),
                           tuple(NSA_HEADS + h for h in SWA_SLOT_HEADS))
    bias_dil = []
    for gi, (win, dil) in enumerate(DIL_PATTERNS):
        h0 = NSA_HEADS + SWA_HEADS + gi * DIL_HEADS_PER_GROUP
        bias_dil.append(_bias_tiles(rel_bias, _bucket_tiles(min(win // BLK + 1, nq), win, dil),
                                    tuple(range(h0, h0 + DIL_HEADS_PER_GROUP))))
    swa_rows = np.concatenate([np.arange(h * HEAD_DIM, (h + 1) * HEAD_DIM) for h in SWA_SLOT_HEADS])

    x2 = x.reshape(m, d)
    for layer in range(depth):
        mod4 = _ada(c, w_ada[layer], b_ada[layer]).reshape(bsz, 6, 1, d)
        w_nat, w_t = _relayout_w_in(w_in[layer])
        z2 = _inproj(x2, mod4, w_nat, seq, sc_idx=1, sh_idx=0)
        zt = _inproj_t(x2, mod4, w_t, seq, sc_idx=1, sh_idx=0).reshape(bsz, nq, N_ROWS_T, BLK)
        z = z2.reshape(bsz, seq, N_COLS)
        chunks = z[:, :, COL_NSA_CMP:COL_NSA_CMP + 2 * HEAD_DIM].reshape(bsz, seq // CMP_STRIDE,
                                                                        CMP_STRIDE * 2 * HEAD_DIM)
        kc, vct = _compress(chunks, cmp_pe[layer], cmp_w1[layer], cmp_w2[layer])
        o_a = _nsa(z, zt, kc, vct, bias_sel, bias_win)
        o_b = _swa(z, zt, bias_swa, swa_sinks[layer])
        o_c = _dilated(z, zt, bias_dil)
        o_d = _stick_breaking(z, zt)
        wb = w_branch[layer]
        wb = jnp.stack([wb[0], wb[1][swa_rows], wb[2], wb[3]]).astype(BF16)
        branches = [a.reshape(m, BRANCH_WIDTH) for a in (o_a, o_b, o_c, o_d)]
        x2 = _merge(branches, z2, wb, w_out[layer].astype(BF16), x2, mod4, ln_g[layer, 0], ln_b[layer, 0], seq,
                    alpha=alpha, g_idx=2)
        i = layer // 2
        if layer % 2 == 0:
            x2 = _ffn(x2, mod4, ffn_w_gate[i].astype(BF16), ffn_w_up[i].astype(BF16), ffn_w_down[i].astype(BF16),
                      ln_g[layer, 1], ln_b[layer, 1], seq, alpha=alpha)
        else:
            x2 = _moe(x2, mod4, moe_router[i], moe_w_gate[i].astype(BF16), moe_w_up[i].astype(BF16),
                      moe_w_down[i].astype(BF16), ln_g[layer, 1], ln_b[layer, 1], seq, alpha=alpha)
    return x2.reshape(bsz, seq, d)
```

```python
import functools
import math

import numpy as np
import jax
import jax.numpy as jnp
from jax import lax
from jax.experimental import pallas as pl
from jax.experimental.pallas import tpu as pltpu

F32 = jnp.float32
BF16 = jnp.bfloat16
HIGHEST = lax.Precision.HIGHEST

HEAD_DIM = 64
BLK = 128
N_SLOTS = 4
NSA_HEADS = 4
CMP_LEN = 32
CMP_STRIDE = 16
SEL_LEN = 64
N_SEL = 16
NSA_WINDOW = 512
SWA_HEADS = 4
SWA_WINDOW = 128
SWA_SLOT_HEADS = (0, 2, 1, 3)
DIL_PATTERNS = ((128, 1), (512, 4), (2048, 16))
DIL_HEADS_PER_GROUP = 4
SB_HEADS = 4
N_BRANCHES = 4
BRANCH_WIDTH = 4 * HEAD_DIM
NUM_BUCKETS = 32
MAX_DISTANCE = 2048
N_EXPERTS = 8
LN_EPS = 1e-5
NEG_INF = -1e30
TINY = 1e-30
FORCE_SCORE = 1e4

IN_NSA_Q, IN_K_CMP, IN_V_CMP, IN_K_SEL, IN_V_SEL, IN_K_WIN, IN_V_WIN, IN_GATE = 0, 256, 320, 384, 448, 512, 576, 640
IN_SWA_Q, IN_SWA_K, IN_SWA_V = 652, 908, 1036
IN_DIL_Q, IN_DIL_K, IN_DIL_V = 1164, 1932, 2188
IN_SB_Q, IN_SB_K, IN_SB_V = 2444, 2700, 2956
IN_MG = 3212
GATE_COLS = NSA_HEADS * 3

COL_MG = 0
COL_NSA_Q = 4096
COL_NSA_CMP = 4352
COL_NSA_SEL = 4480
COL_NSA_WIN = 4608
COL_NSA_GATE = 4736
COL_SWA_Q = 4864
COL_SWA_K = 5120
COL_DIL_Q = 5376
COL_DIL_K = 6144
COL_SB_Q = 6400
COL_SB_K = 6656
N_COLS = 6912
ROW_NSA_V = 0
ROW_SWA_V = 128
ROW_DIL_V = 256
ROW_SB_V = 512
ROW_GATE = 768
N_ROWS_T = 784

VMEM_LIMIT = 56 * 1024 * 1024
ATTN_BATCH = 4


def _params(sem):
    return pltpu.CompilerParams(dimension_semantics=sem, vmem_limit_bytes=VMEM_LIMIT)


def _ln(x):
    mu = jnp.mean(x, axis=-1, keepdims=True)
    xc = x - mu
    var = jnp.mean(xc * xc, axis=-1, keepdims=True)
    return xc * lax.rsqrt(var + LN_EPS)


def _dot_nt(a, b, precision=None):
    return lax.dot_general(a, b, (((1,), (1,)), ((), ())), preferred_element_type=F32, precision=precision)


def _dot(a, b, precision=None):
    return jnp.dot(a, b, preferred_element_type=F32, precision=precision)


def _full(shape):
    return pl.BlockSpec(shape, lambda *_: (0,) * len(shape))


def _ada_kernel(c_ref, w_ref, b_ref, o_ref):
    c = c_ref[...]
    s = c * jax.nn.sigmoid(c)
    o_ref[...] = _dot(s, w_ref[...], HIGHEST) + b_ref[...]


def _ada(c, w, b):
    bsz, d = c.shape
    n = w.shape[1]
    tn = 1024
    return pl.pallas_call(
        _ada_kernel,
        grid=(n // tn,),
        in_specs=[pl.BlockSpec((bsz, d), lambda j: (0, 0)),
                  pl.BlockSpec((d, tn), lambda j: (0, j)),
                  pl.BlockSpec((1, tn), lambda j: (0, j))],
        out_specs=pl.BlockSpec((bsz, tn), lambda j: (0, j)),
        out_shape=jax.ShapeDtypeStruct((bsz, n), F32),
        compiler_params=_params(("parallel",)),
        name="ada",
    )(c, w, b.reshape(1, n))


def _inproj_kernel(x_ref, sc_ref, sh_ref, w_ref, o_ref):
    h = _ln(x_ref[...]) * (1.0 + sc_ref[...]) + sh_ref[...]
    o_ref[...] = _dot(h.astype(BF16), w_ref[...]).astype(o_ref.dtype)


def _inproj(x2, mod4, w, seq, *, sc_idx, sh_idx, tm=512):
    m, d = x2.shape
    n = w.shape[1]
    tn = n // 2
    per = seq // tm
    return pl.pallas_call(
        _inproj_kernel,
        grid=(n // tn, m // tm),
        in_specs=[pl.BlockSpec((tm, d), lambda j, i: (i, 0)),
                  pl.BlockSpec((None, None, 1, d), lambda j, i: (i // per, sc_idx, 0, 0)),
                  pl.BlockSpec((None, None, 1, d), lambda j, i: (i // per, sh_idx, 0, 0)),
                  pl.BlockSpec((d, tn), lambda j, i: (0, j))],
        out_specs=pl.BlockSpec((tm, tn), lambda j, i: (i, j)),
        out_shape=jax.ShapeDtypeStruct((m, n), BF16),
        compiler_params=_params(("parallel", "parallel")),
        name="inproj",
    )(x2, mod4, mod4, w)


def _inproj_t_kernel(x_ref, sc_ref, sh_ref, w_ref, o_ref):
    h = (_ln(x_ref[...]) * (1.0 + sc_ref[...]) + sh_ref[...]).astype(BF16)
    res = _dot_nt(w_ref[...], h)
    for j in range(o_ref.shape[0]):
        o_ref[j] = res[:, j * BLK:(j + 1) * BLK].astype(o_ref.dtype)


def _inproj_t(x2, mod4, wt, seq, *, sc_idx, sh_idx, tm=512):
    m, d = x2.shape
    rows = wt.shape[0]
    per = seq // tm
    return pl.pallas_call(
        _inproj_t_kernel,
        grid=(m // tm,),
        in_specs=[pl.BlockSpec((tm, d), lambda i: (i, 0)),
                  pl.BlockSpec((None, None, 1, d), lambda i: (i // per, sc_idx, 0, 0)),
                  pl.BlockSpec((None, None, 1, d), lambda i: (i // per, sh_idx, 0, 0)),
                  _full(wt.shape)],
        out_specs=pl.BlockSpec((tm // BLK, rows, BLK), lambda i: (i, 0, 0)),
        out_shape=jax.ShapeDtypeStruct((m // BLK, rows, BLK), BF16),
        compiler_params=_params(("parallel",)),
        name="inproj_t",
    )(x2, mod4, mod4, wt)


def _np_bucket(dist):
    dist = np.maximum(dist, 0)
    max_exact = NUM_BUCKETS // 2
    d_f = np.maximum(dist, 1).astype(np.float32)
    large = max_exact + (np.log(d_f / np.float32(max_exact)) / np.float32(math.log(MAX_DISTANCE / max_exact))
                         * np.float32(NUM_BUCKETS - max_exact)).astype(np.int32)
    large = np.minimum(large, NUM_BUCKETS - 1)
    return np.where(dist < max_exact, dist, large).astype(np.int32)


def _bucket_tiles(n_off, max_dist, dil):
    b = np.arange(BLK)[:, None]
    a = np.arange(BLK)[None, :]
    tiles = []
    for o in range(n_off):
        dist = o * BLK + a - b
        ok = (dist >= 0) & (dist <= max_dist) & (dist % dil == 0)
        tiles.append(np.where(ok, _np_bucket(dist), -1))
    return np.stack(tiles).astype(np.int32)


def _bias_kernel(tbl_ref, bk_ref, o_ref, *, heads):
    slot = pl.program_id(0)
    h = jnp.int32(heads[0])
    for j in range(1, len(heads)):
        h = jnp.where(slot == j, heads[j], h)
    b = bk_ref[...]
    out = jnp.where(b < 0, NEG_INF, 0.0).astype(F32)
    for k in range(NUM_BUCKETS):
        out = jnp.where(b == k, tbl_ref[k, h], out)
    o_ref[...] = out


def _bias_tiles(rel_bias, buckets, heads):
    n_off = buckets.shape[0]
    return pl.pallas_call(
        functools.partial(_bias_kernel, heads=tuple(heads)),
        grid=(len(heads),),
        in_specs=[pl.BlockSpec(memory_space=pltpu.SMEM), _full((n_off, BLK, BLK))],
        out_specs=pl.BlockSpec((n_off, BLK, BLK), lambda s: (0, 0, s)),
        out_shape=jax.ShapeDtypeStruct((n_off, BLK, len(heads) * BLK), F32),
        compiler_params=_params(("parallel",)),
        name="bias_tiles",
    )(rel_bias, jnp.asarray(buckets))


def _slot_queries(q, kw):
    lane = lax.broadcasted_iota(jnp.int32, (BLK, kw), 1)
    zero = jnp.zeros((BLK, kw), q.dtype)
    parts = []
    for j in range(N_SLOTS):
        lo = j * HEAD_DIM
        grp = q[:, (lo // kw) * kw:(lo // kw + 1) * kw]
        inside = (lane >= lo % kw) & (lane < lo % kw + HEAD_DIM)
        parts.append(jnp.where(inside, grp, zero))
    return jnp.concatenate(parts, axis=0)


def _softmax_step(s, vt, state, slot_rows):
    m, l = state[0], state[1]
    m_new = jnp.maximum(m, jnp.max(s, axis=0, keepdims=True))
    alpha = jnp.exp(m - m_new)
    p = jnp.exp(s - m_new)
    l = alpha * l + jnp.sum(p, axis=0, keepdims=True)
    yield
    res = _dot(vt, p.astype(BF16))
    yield
    accs = []
    for j in range(N_SLOTS):
        r0 = slot_rows[j]
        accs.append(alpha[:, j * BLK:(j + 1) * BLK] * state[2 + j]
                    + res[r0:r0 + HEAD_DIM, j * BLK:(j + 1) * BLK])
    return (m_new, l, *accs)


def _softmax_init():
    return (jnp.full((1, N_SLOTS * BLK), NEG_INF, F32), jnp.zeros((1, N_SLOTS * BLK), F32),
            *[jnp.zeros((HEAD_DIM, BLK), F32) for _ in range(N_SLOTS)])


def _store_heads(o_ref, outs):
    o_ref[...] = jnp.transpose(jnp.concatenate(outs, axis=0)).astype(o_ref.dtype)


def _key_rows(k_ref, kb):
    return k_ref[pl.ds(pl.multiple_of(kb * BLK, BLK), BLK), :]


def _slot_split(x, slot_rows):
    return [x[slot_rows[j]:slot_rows[j] + HEAD_DIM, j * BLK:(j + 1) * BLK] for j in range(N_SLOTS)]


def _window_softmax(qi, wq, k_ref, vt_of, bias_ref, n_off, slot_rows):
    tiles = []
    m = None
    for o in range(n_off):
        kb = qi - o
        kbc = jnp.maximum(kb, 0)
        s = _dot_nt(_key_rows(k_ref, kbc), wq) + bias_ref[o]
        if o > 0:
            s = s + jnp.where(kb >= 0, 0.0, NEG_INF)
        tiles.append((kbc, s))
        mo = jnp.max(s, axis=0, keepdims=True)
        m = mo if m is None else jnp.maximum(m, mo)
    l = None
    res = None
    for kbc, s in tiles:
        p = jnp.exp(s - m)
        lo = jnp.sum(p, axis=0, keepdims=True)
        ro = _dot(vt_of(kbc), p.astype(BF16))
        l = lo if l is None else l + lo
        res = ro if res is None else res + ro
    return (m, l, *_slot_split(res, slot_rows))


def _run_interleaved(gens):
    results = [None] * len(gens)
    live = list(range(len(gens)))
    while live:
        for i in tuple(live):
            try:
                next(gens[i])
            except StopIteration as done:
                results[i] = done.value
                live.remove(i)
    return results


def _chained_loop(lo, hi, steps, states):
    def body(kb, sts):
        return tuple(_run_interleaved([f(kb, st) for f, st in zip(steps, sts)]))
    return lax.fori_loop(lo, hi, body, tuple(states))


def _compress_kernel(ch_ref, wa_ref, wb_ref, pea_ref, peb_ref, w2k_ref, w2vt_ref, kc_ref, vct_ref):
    ch = ch_ref[...].astype(F32)
    wa = wa_ref[...]
    wb = wb_ref[...]
    first = _dot(ch, wa, HIGHEST)
    second = _dot(ch, wb, HIGHEST)
    pe = (_dot(pea_ref[...], wa, HIGHEST) + _dot(peb_ref[...], wb, HIGHEST))[0:1]
    n = second.shape[0]
    nxt = pltpu.roll(second, n - 1, 0)
    hid = jax.nn.gelu(first + nxt + pe)
    kc_ref[...] = _dot(hid, w2k_ref[...], HIGHEST).astype(kc_ref.dtype)
    vct_ref[...] = _dot_nt(w2vt_ref[...], hid, HIGHEST).astype(vct_ref.dtype)


def _compress(chunks, cmp_pe, cmp_w1, cmp_w2):
    bsz, n_chunk, width = chunks.shape
    per = CMP_LEN // 2
    hd = HEAD_DIM
    w1 = cmp_w1.reshape(2, CMP_LEN, hd, hd)
    wexp = jnp.einsum('cpdn,ce->pcden', w1, jnp.eye(2, dtype=F32)).reshape(CMP_LEN, 2 * hd, 2 * hd)
    wa = wexp[:per].reshape(width, 2 * hd)
    wb = wexp[per:].reshape(width, 2 * hd)
    pe = jnp.transpose(cmp_pe, (1, 0, 2)).reshape(CMP_LEN, 2 * hd)
    pea = jnp.broadcast_to(pe[:per].reshape(1, width), (8, width))
    peb = jnp.broadcast_to(pe[per:].reshape(1, width), (8, width))
    zero = jnp.zeros((hd, hd), F32)
    w2k = jnp.concatenate([jnp.concatenate([cmp_w2[0], cmp_w2[0]], 1), jnp.concatenate([zero, zero], 1)], 0)
    w2vt = jnp.concatenate([zero, cmp_w2[1].T], 1)
    return pl.pallas_call(
        _compress_kernel,
        grid=(bsz,),
        in_specs=[pl.BlockSpec((None, n_chunk, width), lambda b: (b, 0, 0)),
                  _full(wa.shape), _full(wb.shape), _full(pea.shape), _full(peb.shape), _full(w2k.shape),
                  _full(w2vt.shape)],
        out_specs=(pl.BlockSpec((None, n_chunk, 2 * hd), lambda b: (b, 0, 0)),
                   pl.BlockSpec((None, hd, n_chunk), lambda b: (b, 0, 0))),
        out_shape=(jax.ShapeDtypeStruct((bsz, n_chunk, 2 * hd), BF16),
                   jax.ShapeDtypeStruct((bsz, hd, n_chunk), BF16)),
        compiler_params=_params(("parallel",)),
        name="nsa_compress",
    )(chunks, wa, wb, pea, peb, w2k, w2vt)


def _nsa_kernel(q_ref, kc_ref, vct_ref, ovt_ref, ksel_ref, kwin_ref, vt_ref, gate_ref, bsel_ref, bwin_ref, et_ref,
                o_ref, *, n_blk, n_win_off):
    qi = pl.program_id(1)
    t0 = qi * BLK
    nb = q_ref.shape[0]
    n_cmp = kc_ref.shape[1]
    slot_rows = (0,) * N_SLOTS
    shape = (n_cmp, N_SLOTS * BLK)
    tq = t0 + lax.broadcasted_iota(jnp.int32, shape, 1) % BLK
    ci = lax.broadcasted_iota(jnp.int32, shape, 0)
    ok = (ci * CMP_STRIDE + (CMP_LEN - 1)) <= tq
    jb = lax.broadcasted_iota(jnp.int32, (n_blk, BLK), 0)
    cur = (t0 + lax.broadcasted_iota(jnp.int32, (n_blk, BLK), 1)) // SEL_LEN
    forced = (jb == 0) | (jb == cur) | (jb == cur - 1)

    o_cmps, o_wins, sel_steps = [], [], []
    for bb in range(nb):
        wq = _slot_queries(q_ref[bb], 2 * HEAD_DIM)

        s = jnp.where(ok, _dot_nt(kc_ref[bb], wq), NEG_INF)
        m = jnp.max(s, axis=0, keepdims=True)
        p = jnp.where(ok, jnp.exp(s - m), 0.0)
        p = p / jnp.maximum(jnp.sum(p, axis=0, keepdims=True), TINY)
        o_cmps.append(_dot(vct_ref[bb], p.astype(BF16)))
        psum = p[:, 0:BLK]
        for j in range(1, N_SLOTS):
            psum = psum + p[:, j * BLK:(j + 1) * BLK]

        hi = psum.astype(BF16)
        lo = (psum - hi.astype(F32)).astype(BF16)
        score = _dot(ovt_ref[...], hi) + _dot(ovt_ref[...], lo)
        score = jnp.where(forced, FORCE_SCORE, jnp.where(jb > cur, NEG_INF, score))
        rank = jnp.zeros((n_blk, BLK), F32)
        for c in range(n_blk):
            row = score[c:c + 1, :]
            beats = (row > score) | ((row == score) & (jb > c))
            rank = rank + jnp.where(beats, 1.0, 0.0)
        sel = jnp.where(rank < N_SEL, 1.0, 0.0).astype(BF16)

        st = _window_softmax(qi, wq, kwin_ref.at[bb], lambda kb, bb=bb: vt_ref[bb, kb, HEAD_DIM:2 * HEAD_DIM, :],
                             bwin_ref, n_win_off, slot_rows)
        o_wins.append([st[2 + j] / st[1][:, j * BLK:(j + 1) * BLK] for j in range(N_SLOTS)])

        def sel_step(kb, state, bb=bb, wq=wq, sel=sel):
            pen = (_dot(et_ref[kb], sel) - 1.0) * (-NEG_INF)
            pen = jnp.concatenate([pen] * N_SLOTS, axis=1)
            sc = _dot_nt(_key_rows(ksel_ref.at[bb], kb), wq) + bsel_ref[qi - kb] + pen
            yield
            return (yield from _softmax_step(sc, vt_ref[bb, kb, 0:HEAD_DIM, :], state, slot_rows))

        sel_steps.append(sel_step)

    sel_states = _run_interleaved([f(qi, _softmax_init()) for f in sel_steps])
    sel_states = _chained_loop(0, qi, sel_steps, sel_states)
    for bb in range(nb):
        st = sel_states[bb]
        gate = jax.nn.sigmoid(gate_ref[bb].astype(F32))
        outs = []
        for j in range(N_SLOTS):
            o_sel = st[2 + j] / st[1][:, j * BLK:(j + 1) * BLK]
            outs.append(gate[3 * j:3 * j + 1] * o_cmps[bb][:, j * BLK:(j + 1) * BLK]
                        + gate[3 * j + 1:3 * j + 2] * o_sel + gate[3 * j + 2:3 * j + 3] * o_wins[bb][j])
        _store_heads(o_ref.at[bb], outs)


def _nsa(z, zt, kc, vct, bias_sel, bias_win, nb=ATTN_BATCH):
    bsz, seq, _ = z.shape
    nq = seq // BLK
    n_cmp = kc.shape[1]
    n_blk = seq // SEL_LEN
    assert n_blk >= N_SEL
    ci = np.arange(n_cmp)[None, :]
    sj = np.arange(n_blk)[:, None]
    overlap_t = ((ci * CMP_STRIDE + CMP_LEN - 1 >= sj * SEL_LEN) & (ci * CMP_STRIDE < (sj + 1) * SEL_LEN)
                 & (ci < n_cmp - 1)).astype(np.float32)
    key_blk = (np.arange(seq) // SEL_LEN).reshape(nq, BLK, 1)
    expand_t = (key_blk == np.arange(n_blk)[None, None, :]).astype(np.float32)
    qw = NSA_HEADS * HEAD_DIM
    kw = 2 * HEAD_DIM
    return pl.pallas_call(
        functools.partial(_nsa_kernel, n_blk=n_blk, n_win_off=bias_win.shape[0]),
        grid=(bsz // nb, nq),
        in_specs=[pl.BlockSpec((nb, BLK, qw), lambda b, i: (b, i, COL_NSA_Q // qw)),
                  pl.BlockSpec((nb, n_cmp, kw), lambda b, i: (b, 0, 0)),
                  pl.BlockSpec((nb, HEAD_DIM, n_cmp), lambda b, i: (b, 0, 0)),
                  _full(overlap_t.shape),
                  pl.BlockSpec((nb, seq, kw), lambda b, i: (b, 0, COL_NSA_SEL // kw)),
                  pl.BlockSpec((nb, seq, kw), lambda b, i: (b, 0, COL_NSA_WIN // kw)),
                  pl.BlockSpec((nb, nq, kw, BLK), lambda b, i: (b, 0, ROW_NSA_V // kw, 0)),
                  pl.BlockSpec((nb, None, 16, BLK), lambda b, i: (b, i, ROW_GATE // 16, 0)),
                  _full(bias_sel.shape), _full(bias_win.shape), _full(expand_t.shape)],
        out_specs=pl.BlockSpec((nb, BLK, qw), lambda b, i: (b, i, 0)),
        out_shape=jax.ShapeDtypeStruct((bsz, seq, qw), BF16),
        compiler_params=_params(("parallel", "parallel")),
        name="nsa_mixer",
    )(z, kc, vct, jnp.asarray(overlap_t, dtype=BF16), z, z, zt, zt, bias_sel, bias_win,
      jnp.asarray(expand_t, dtype=BF16))


def _swa_kernel(q_ref, k_ref, vt_ref, bias_ref, sink_ref, o_ref, *, n_off):
    qi = pl.program_id(1)
    slot_rows = tuple((j % 2) * HEAD_DIM for j in range(N_SLOTS))
    for bb in range(q_ref.shape[0]):
        wq = _slot_queries(q_ref[bb], 2 * HEAD_DIM)
        st = _window_softmax(qi, wq, k_ref.at[bb], lambda kb, bb=bb: vt_ref[bb, kb], bias_ref, n_off, slot_rows)
        m, l = st[0], st[1]
        outs = []
        for j in range(N_SLOTS):
            sink = sink_ref[SWA_SLOT_HEADS[j]]
            mj = m[:, j * BLK:(j + 1) * BLK]
            lj = l[:, j * BLK:(j + 1) * BLK]
            m2 = jnp.maximum(mj, sink)
            scale = jnp.exp(mj - m2)
            den = lj * scale + jnp.exp(sink - m2)
            outs.append(st[2 + j] * (scale / den))
        _store_heads(o_ref.at[bb], outs)


def _swa(z, zt, bias, sinks, nb=ATTN_BATCH):
    bsz, seq, _ = z.shape
    nq = seq // BLK
    qw = SWA_HEADS * HEAD_DIM
    kw = 2 * HEAD_DIM
    return pl.pallas_call(
        functools.partial(_swa_kernel, n_off=bias.shape[0]),
        grid=(bsz // nb, nq),
        in_specs=[pl.BlockSpec((nb, BLK, qw), lambda b, i: (b, i, COL_SWA_Q // qw)),
                  pl.BlockSpec((nb, seq, kw), lambda b, i: (b, 0, COL_SWA_K // kw)),
                  pl.BlockSpec((nb, nq, kw, BLK), lambda b, i: (b, 0, ROW_SWA_V // kw, 0)),
                  _full(bias.shape),
                  pl.BlockSpec(memory_space=pltpu.SMEM)],
        out_specs=pl.BlockSpec((nb, BLK, qw), lambda b, i: (b, i, 0)),
        out_shape=jax.ShapeDtypeStruct((bsz, seq, qw), BF16),
        compiler_params=_params(("parallel", "parallel")),
        name="swa_mixer",
    )(z, z, zt, bias, sinks)


def _dil_kernel(q0_ref, q1_ref, q2_ref, k_ref, vt_ref, b0_ref, b1_ref, b2_ref, o_ref):
    qi = pl.program_id(1)
    nb = q0_ref.shape[0]
    slot_rows = tuple(j * HEAD_DIM for j in range(N_SLOTS))
    kw = N_SLOTS * HEAD_DIM
    short, far_steps = [], []
    for bb in range(nb):
        vt_of = lambda kb, bb=bb: vt_ref[bb, kb]
        short.append([_window_softmax(qi, _slot_queries(q_ref[bb], kw), k_ref.at[bb], vt_of, bias_ref,
                                      bias_ref.shape[0], slot_rows)
                      for q_ref, bias_ref in ((q0_ref, b0_ref), (q1_ref, b1_ref))])
        wq2 = _slot_queries(q2_ref[bb], kw)

        def far_step(kb, state, bb=bb, wq2=wq2, vt_of=vt_of):
            sc = _dot_nt(_key_rows(k_ref.at[bb], kb), wq2) + b2_ref[qi - kb]
            yield
            return (yield from _softmax_step(sc, vt_of(kb), state, slot_rows))

        far_steps.append(far_step)
    far_states = _run_interleaved([f(qi, _softmax_init()) for f in far_steps])
    far_states = _chained_loop(jnp.maximum(qi - (b2_ref.shape[0] - 1), 0), qi, far_steps, far_states)
    for bb in range(nb):
        groups = short[bb] + [far_states[bb]]
        lses = [st[0] + jnp.log(st[1]) for st in groups]
        top = jnp.maximum(jnp.maximum(lses[0], lses[1]), lses[2])
        es = [jnp.exp(v - top) for v in lses]
        den = es[0] + es[1] + es[2]
        outs = []
        for j in range(N_SLOTS):
            sl = slice(j * BLK, (j + 1) * BLK)
            acc = jnp.zeros((HEAD_DIM, BLK), F32)
            for st, e in zip(groups, es):
                acc = acc + (e[:, sl] / den[:, sl]) * (st[2 + j] / st[1][:, sl])
            outs.append(acc)
        _store_heads(o_ref.at[bb], outs)


def _dilated(z, zt, biases, nb=ATTN_BATCH):
    bsz, seq, _ = z.shape
    nq = seq // BLK
    qw = DIL_HEADS_PER_GROUP * HEAD_DIM
    qspec = lambda g: pl.BlockSpec((nb, BLK, qw), lambda b, i: (b, i, COL_DIL_Q // qw + g))
    return pl.pallas_call(
        _dil_kernel,
        grid=(bsz // nb, nq),
        in_specs=[qspec(0), qspec(1), qspec(2),
                  pl.BlockSpec((nb, seq, qw), lambda b, i: (b, 0, COL_DIL_K // qw)),
                  pl.BlockSpec((nb, nq, qw, BLK), lambda b, i: (b, 0, ROW_DIL_V // qw, 0)),
                  _full(biases[0].shape), _full(biases[1].shape), _full(biases[2].shape)],
        out_specs=pl.BlockSpec((nb, BLK, qw), lambda b, i: (b, i, 0)),
        out_shape=jax.ShapeDtypeStruct((bsz, seq, qw), BF16),
        compiler_params=_params(("parallel", "parallel")),
        name="dilated_mixer",
    )(z, z, z, z, zt, *biases)


def _sb_kernel(q_ref, k_ref, vt_ref, u_ref, o_ref):
    qi = pl.program_id(1)
    nb = q_ref.shape[0]
    u = u_ref[...]
    shape = (BLK, N_SLOTS * BLK)
    strict = lax.broadcasted_iota(jnp.int32, shape, 0) < lax.broadcasted_iota(jnp.int32, shape, 1) % BLK

    def step(bb, wq, kb, state, diag):
        run = state[0]
        zz = _dot_nt(_key_rows(k_ref.at[bb], kb), wq)
        yield
        sp = jnp.maximum(zz, 0.0) + jnp.log(1.0 + jnp.exp(-jnp.abs(zz)))
        lf = -sp
        if diag:
            lf = jnp.where(strict, lf, 0.0)
        hi = lf.astype(BF16)
        lo = (lf - hi.astype(F32)).astype(BF16)
        after = _dot(u, hi) + _dot(u, lo) + run
        yield
        a = jnp.exp((zz - sp) + after)
        if diag:
            a = jnp.where(strict, a, 0.0)
        res = _dot(vt_ref[bb, kb], a.astype(BF16))
        yield
        accs = [state[1 + j] + res[j * HEAD_DIM:(j + 1) * HEAD_DIM, j * BLK:(j + 1) * BLK] for j in range(N_SLOTS)]
        return (run + jnp.sum(lf, axis=0, keepdims=True), *accs)

    init = (jnp.zeros((1, N_SLOTS * BLK), F32), *[jnp.zeros((HEAD_DIM, BLK), F32) for _ in range(N_SLOTS)])
    wqs = [_slot_queries(q_ref[bb], SB_HEADS * HEAD_DIM) for bb in range(nb)]
    steps = [lambda t, st, bb=bb: step(bb, wqs[bb], qi - 1 - t, st, False) for bb in range(nb)]
    states = _run_interleaved([step(bb, wqs[bb], qi, init, True) for bb in range(nb)])
    states = _chained_loop(0, qi, steps, states)
    for bb in range(nb):
        _store_heads(o_ref.at[bb], list(states[bb][1:]))


def _stick_breaking(z, zt, nb=ATTN_BATCH):
    bsz, seq, _ = z.shape
    nq = seq // BLK
    w = SB_HEADS * HEAD_DIM
    later = (np.arange(BLK)[None, :] > np.arange(BLK)[:, None]).astype(np.float32)
    return pl.pallas_call(
        _sb_kernel,
        grid=(bsz // nb, nq),
        in_specs=[pl.BlockSpec((nb, BLK, w), lambda b, i: (b, i, COL_SB_Q // w)),
                  pl.BlockSpec((nb, seq, w), lambda b, i: (b, 0, COL_SB_K // w)),
                  pl.BlockSpec((nb, nq, w, BLK), lambda b, i: (b, 0, ROW_SB_V // w, 0)),
                  _full((BLK, BLK))],
        out_specs=pl.BlockSpec((nb, BLK, w), lambda b, i: (b, i, 0)),
        out_shape=jax.ShapeDtypeStruct((bsz, seq, w), BF16),
        compiler_params=_params(("parallel", "parallel")),
        name="stick_breaking",
    )(z, z, zt, jnp.asarray(later, dtype=BF16))


def _merge_kernel(oa_ref, ob_ref, oc_ref, od_ref, mg_ref, wb_ref, wo_ref, x_ref, g_ref, lng_ref, lnb_ref, o_ref,
                  *, alpha):
    d = x_ref.shape[1]
    merged = jnp.zeros((x_ref.shape[0], d), F32)
    for b, ref in enumerate((oa_ref, ob_ref, oc_ref, od_ref)):
        proj = _dot(ref[...], wb_ref[b])
        merged = merged + jax.nn.sigmoid(mg_ref[:, b * d:(b + 1) * d].astype(F32)) * proj
    y = _dot(merged.astype(BF16), wo_ref[...])
    v = alpha * x_ref[...] + g_ref[...] * y
    o_ref[...] = _ln(v) * lng_ref[...] + lnb_ref[...]


def _merge(branches, z2, wb, wo, x2, mod4, lng, lnb, seq, *, alpha, g_idx, tm=512):
    m, d = x2.shape
    bw = BRANCH_WIDTH
    per = seq // tm
    row = lambda w: pl.BlockSpec((tm, w), lambda i: (i, 0))
    return pl.pallas_call(
        functools.partial(_merge_kernel, alpha=alpha),
        grid=(m // tm,),
        in_specs=[row(bw), row(bw), row(bw), row(bw),
                  pl.BlockSpec((tm, N_BRANCHES * d), lambda i: (i, COL_MG // (N_BRANCHES * d))),
                  _full(wb.shape), _full(wo.shape),
                  row(d),
                  pl.BlockSpec((None, None, 1, d), lambda i: (i // per, g_idx, 0, 0)),
                  _full((1, d)), _full((1, d))],
        out_specs=row(d),
        out_shape=jax.ShapeDtypeStruct((m, d), F32),
        compiler_params=_params(("parallel",)),
        name="merge_out",
    )(*branches, z2, wb, wo, x2, mod4, lng.reshape(1, d), lnb.reshape(1, d))


def _ffn_kernel(x_ref, sc_ref, sh_ref, g_ref, lng_ref, lnb_ref, wg_ref, wu_ref, wd_ref, o_ref, h_ref, acc_ref,
                *, alpha):
    f = pl.program_id(1)

    @pl.when(f == 0)
    def _():
        h_ref[...] = (_ln(x_ref[...]) * (1.0 + sc_ref[...]) + sh_ref[...]).astype(BF16)
        acc_ref[...] = jnp.zeros_like(acc_ref)

    h = h_ref[...]
    a = _dot(h, wg_ref[...])
    u = _dot(h, wu_ref[...])
    act = (a * jax.nn.sigmoid(a)) * u
    acc_ref[...] += _dot(act.astype(BF16), wd_ref[...])

    @pl.when(f == pl.num_programs(1) - 1)
    def _():
        v = alpha * x_ref[...] + g_ref[...] * acc_ref[...]
        o_ref[...] = _ln(v) * lng_ref[...] + lnb_ref[...]


def _ffn(x2, mod4, wg, wu, wd, lng, lnb, seq, *, alpha, tm=512, tf=1408):
    m, d = x2.shape
    ff = wg.shape[1]
    per = seq // tm
    modspec = lambda k: pl.BlockSpec((None, None, 1, d), lambda i, f: (i // per, k, 0, 0))
    return pl.pallas_call(
        functools.partial(_ffn_kernel, alpha=alpha),
        grid=(m // tm, ff // tf),
        in_specs=[pl.BlockSpec((tm, d), lambda i, f: (i, 0)),
                  modspec(4), modspec(3), modspec(5),
                  pl.BlockSpec((1, d), lambda i, f: (0, 0)), pl.BlockSpec((1, d), lambda i, f: (0, 0)),
                  pl.BlockSpec((d, tf), lambda i, f: (0, f)),
                  pl.BlockSpec((d, tf), lambda i, f: (0, f)),
                  pl.BlockSpec((tf, d), lambda i, f: (f, 0))],
        out_specs=pl.BlockSpec((tm, d), lambda i, f: (i, 0)),
        out_shape=jax.ShapeDtypeStruct((m, d), F32),
        scratch_shapes=[pltpu.VMEM((tm, d), BF16), pltpu.VMEM((tm, d), F32)],
        compiler_params=_params(("parallel", "arbitrary")),
        name="ffn_dense",
    )(x2, mod4, mod4, mod4, lng.reshape(1, d), lnb.reshape(1, d), wg, wu, wd)


MOE_WIN = 512
MOE_TILE = 256
MOE_FFN_TILE = 512
INFO_E1, INFO_E2, INFO_W1, INFO_W2, INFO_R1, INFO_R2 = range(6)


def _router_kernel(x_ref, sc_ref, sh_ref, wr_ref, ls_ref, h_ref, col_ref, row_ref, cum_ref, cnt_ref, carry_ref):
    @pl.when(pl.program_id(0) == 0)
    def _():
        carry_ref[...] = jnp.zeros_like(carry_ref)

    tm = x_ref.shape[0]
    lane = lax.broadcasted_iota(jnp.int32, (tm, 128), 1).astype(F32)
    h32 = _ln(x_ref[...]) * (1.0 + sc_ref[...]) + sh_ref[...]
    h_ref[...] = h32.astype(BF16)
    logits = _dot(h32, wr_ref[...], HIGHEST)
    logits = jnp.where(lane < N_EXPERTS, logits, -jnp.inf)
    t1 = jnp.max(logits, axis=1, keepdims=True)
    i1 = jnp.min(jnp.where(logits == t1, lane, 128.0), axis=1, keepdims=True)
    rest = jnp.where(lane == i1, -jnp.inf, logits)
    t2 = jnp.max(rest, axis=1, keepdims=True)
    i2 = jnp.min(jnp.where(rest == t2, lane, 128.0), axis=1, keepdims=True)
    e2 = jnp.exp(t2 - t1)
    den = 1.0 + e2
    w1 = 1.0 / den
    w2 = e2 / den
    onehot = jnp.where(lane == i1, 1.0, jnp.where(lane == i2, 1.0, 0.0))
    before = _dot(ls_ref[...], onehot.astype(BF16)) + carry_ref[0:1, :]
    r1 = jnp.sum(jnp.where(lane == i1, before, 0.0), axis=1, keepdims=True)
    r2 = jnp.sum(jnp.where(lane == i2, before, 0.0), axis=1, keepdims=True)
    info = jnp.zeros((tm, 128), F32)
    for k, v in ((INFO_E1, i1), (INFO_E2, i2), (INFO_W1, w1), (INFO_W2, w2), (INFO_R1, r1), (INFO_R2, r2)):
        info = jnp.where(lane == k, v, info)
    col_ref[...] = info
    row_ref[...] = jnp.transpose(info)[0:8, :]
    cnt = jnp.sum(onehot, axis=0, keepdims=True)
    cum_ref[...] = carry_ref[...]
    cnt_ref[...] = jnp.broadcast_to(cnt, cnt_ref.shape)
    carry_ref[...] = carry_ref[...] + cnt


def _router(x2, mod4, wr, seq):
    m, d = x2.shape
    tm = MOE_WIN
    per = seq // tm
    n_win = m // tm
    wr_pad = jnp.pad(wr, ((0, 0), (0, 128 - wr.shape[1])))
    earlier = (np.arange(tm)[:, None] > np.arange(tm)[None, :]).astype(np.float32)
    modspec = lambda k: pl.BlockSpec((None, None, 1, d), lambda i: (i // per, k, 0, 0))
    return pl.pallas_call(
        _router_kernel,
        grid=(n_win,),
        in_specs=[pl.BlockSpec((tm, d), lambda i: (i, 0)), modspec(4), modspec(3),
                  _full((d, 128)), _full((tm, tm))],
        out_specs=(pl.BlockSpec((tm, d), lambda i: (i, 0)),
                   pl.BlockSpec((tm, 128), lambda i: (i, 0)),
                   pl.BlockSpec((None, 8, tm), lambda i: (i, 0, 0)),
                   pl.BlockSpec((None, 8, 128), lambda i: (i, 0, 0)),
                   pl.BlockSpec((None, 8, 128), lambda i: (i, 0, 0))),
        out_shape=(jax.ShapeDtypeStruct((m, d), BF16),
                   jax.ShapeDtypeStruct((m, 128), F32),
                   jax.ShapeDtypeStruct((n_win, 8, tm), F32),
                   jax.ShapeDtypeStruct((n_win, 8, 128), F32),
                   jax.ShapeDtypeStruct((n_win, 8, 128), F32)),
        scratch_shapes=[pltpu.VMEM((8, 128), F32)],
        compiler_params=_params(("arbitrary",)),
        name="moe_router",
    )(x2, mod4, mod4, wr_pad, jnp.asarray(earlier, dtype=BF16))


def _expert_offset(e, start_ref):
    off = jnp.zeros_like(e)
    for k in range(N_EXPERTS):
        off = jnp.where(e == k, start_ref[k].astype(F32), off)
    return off


def _pair_flags(s, n, key_ref):
    last_idx = key_ref.shape[0] - 1
    key = key_ref[s]
    first = (s == 0) | (key != key_ref[jnp.maximum(s - 1, 0)])
    last = (s == n - 1) | (key != key_ref[jnp.minimum(s + 1, last_idx)])
    return s < n, first, last


def _gather_kernel(pt_ref, pw_ref, n_ref, start_ref, h_ref, row_ref, hs_ref, ws_ref, acc_ref, wacc_ref):
    s = pl.program_id(0)
    active, first, last = _pair_flags(s, n_ref[0], pt_ref)

    @pl.when(active & first)
    def _():
        acc_ref[...] = jnp.zeros_like(acc_ref)
        wacc_ref[...] = jnp.zeros_like(wacc_ref)

    @pl.when(active)
    def _():
        info = row_ref[...]
        win = info.shape[1]
        p1 = info[INFO_R1:INFO_R1 + 1] + _expert_offset(info[INFO_E1:INFO_E1 + 1], start_ref)
        p2 = info[INFO_R2:INFO_R2 + 1] + _expert_offset(info[INFO_E2:INFO_E2 + 1], start_ref)
        rows = (pt_ref[s] * MOE_TILE + lax.broadcasted_iota(jnp.int32, (MOE_TILE, win), 0)).astype(F32)
        m1 = rows == p1
        m2 = rows == p2
        perm = jnp.where(m1, 1.0, jnp.where(m2, 1.0, 0.0)).astype(BF16)
        acc_ref[...] += _dot(perm, h_ref[...])
        wrow = jnp.where(m1, info[INFO_W1:INFO_W1 + 1], jnp.where(m2, info[INFO_W2:INFO_W2 + 1], 0.0))
        wacc_ref[...] += jnp.broadcast_to(jnp.sum(wrow, axis=1, keepdims=True), wacc_ref.shape)

    @pl.when(active & last)
    def _():
        hs_ref[...] = acc_ref[...].astype(hs_ref.dtype)
        ws_ref[...] = wacc_ref[...]


def _moe_ffn_kernel(te_ref, used_ref, hs_ref, ws_ref, wg_ref, wu_ref, wd_ref, ys_ref, acc_ref):
    j = pl.program_id(0)
    f = pl.program_id(1)
    used = used_ref[j] > 0

    @pl.when(used & (f == 0))
    def _():
        acc_ref[...] = jnp.zeros_like(acc_ref)

    @pl.when(used)
    def _():
        h = hs_ref[...]
        a = _dot(h, wg_ref[...])
        u = _dot(h, wu_ref[...])
        act = (a * jax.nn.sigmoid(a)) * u
        acc_ref[...] += _dot(act.astype(BF16), wd_ref[...])

    @pl.when(f == pl.num_programs(1) - 1)
    def _():
        @pl.when(used)
        def _():
            ys_ref[...] = (ws_ref[:, 0:1] * acc_ref[...]).astype(ys_ref.dtype)

        @pl.when(jnp.logical_not(used))
        def _():
            ys_ref[...] = jnp.zeros_like(ys_ref)


def _combine_kernel(pw_ref, pt_ref, n_ref, start_ref, ys_ref, col_ref, x_ref, g_ref, lng_ref, lnb_ref, o_ref,
                    acc_ref, *, alpha):
    s = pl.program_id(0)
    active, first, last = _pair_flags(s, n_ref[0], pw_ref)

    @pl.when(active & first)
    def _():
        acc_ref[...] = jnp.zeros_like(acc_ref)

    @pl.when(active)
    def _():
        info = col_ref[...]
        win = info.shape[0]
        p1 = info[:, INFO_R1:INFO_R1 + 1] + _expert_offset(info[:, INFO_E1:INFO_E1 + 1], start_ref)
        p2 = info[:, INFO_R2:INFO_R2 + 1] + _expert_offset(info[:, INFO_E2:INFO_E2 + 1], start_ref)
        cols = (pt_ref[s] * MOE_TILE + lax.broadcasted_iota(jnp.int32, (win, MOE_TILE), 1)).astype(F32)
        perm = jnp.where(cols == p1, 1.0, jnp.where(cols == p2, 1.0, 0.0)).astype(BF16)
        acc_ref[...] += _dot(perm, ys_ref[...])

    @pl.when(active & last)
    def _():
        v = alpha * x_ref[...] + g_ref[...] * acc_ref[...]
        o_ref[...] = _ln(v) * lng_ref[...] + lnb_ref[...]


def _pair_list(mask, n_pairs):
    flat = mask.reshape(-1)
    n = jnp.sum(flat.astype(jnp.int32))
    idx = jnp.nonzero(flat, size=n_pairs, fill_value=0)[0].astype(jnp.int32)
    idx = jnp.where(jnp.arange(n_pairs) < n, idx, idx[jnp.maximum(n - 1, 0)])
    return idx // mask.shape[1], idx % mask.shape[1], n.reshape(1)


def _moe(x2, mod4, wr, wg, wu, wd, lng, lnb, seq, *, alpha, tf=1792):
    m, d = x2.shape
    n_exp, _, ff = wg.shape
    n_win = m // MOE_WIN
    per = seq // MOE_WIN
    n_rows = 2 * m + n_exp * MOE_FFN_TILE
    n_tiles = n_rows // MOE_TILE
    n_ffn = n_rows // MOE_FFN_TILE
    n_pairs = n_exp * n_win + n_tiles + 3 * n_exp

    h, col, row, cum, cnt = _router(x2, mod4, wr, seq)

    cum = cum[:, 0, :n_exp].astype(jnp.int32)
    cnt = cnt[:, 0, :n_exp].astype(jnp.int32)
    total = cum[-1] + cnt[-1]
    padded = (total + MOE_FFN_TILE - 1) // MOE_FFN_TILE * MOE_FFN_TILE
    end = jnp.cumsum(padded)
    start = (end - padded).astype(jnp.int32)

    def tile_expert(tile_rows):
        row0 = jnp.arange(n_rows // tile_rows, dtype=jnp.int32) * tile_rows
        te = jnp.minimum(jnp.sum((row0[:, None] >= end[None, :]).astype(jnp.int32), axis=1), n_exp - 1)
        return row0, te, row0 < end[-1]

    row0, te, used = tile_expert(MOE_TILE)
    local0 = row0 - start[te]
    lo = jnp.take(cum.T, te, axis=0)
    hi = lo + jnp.take(cnt.T, te, axis=0)
    meet = (lo < local0[:, None] + MOE_TILE) & (hi > local0[:, None]) & (hi > lo) & used[:, None]
    visit = meet.at[:, 0].set(meet[:, 0] | jnp.logical_not(jnp.any(meet, axis=1)))
    g_tile, g_win, g_n = _pair_list(visit, n_pairs)
    c_win, c_tile, c_n = _pair_list(meet.T, n_pairs)
    _, ffn_te, ffn_used = tile_expert(MOE_FFN_TILE)

    hs, ws = pl.pallas_call(
        _gather_kernel,
        grid_spec=pltpu.PrefetchScalarGridSpec(
            num_scalar_prefetch=4,
            grid=(n_pairs,),
            in_specs=[pl.BlockSpec((MOE_WIN, d), lambda s, pt, pw, n, st: (pw[s], 0)),
                      pl.BlockSpec((None, 8, MOE_WIN), lambda s, pt, pw, n, st: (pw[s], 0, 0))],
            out_specs=(pl.BlockSpec((MOE_TILE, d), lambda s, pt, pw, n, st: (pt[s], 0)),
                       pl.BlockSpec((MOE_TILE, 128), lambda s, pt, pw, n, st: (pt[s], 0))),
            scratch_shapes=[pltpu.VMEM((MOE_TILE, d), F32), pltpu.VMEM((MOE_TILE, 128), F32)]),
        out_shape=(jax.ShapeDtypeStruct((n_rows, d), BF16), jax.ShapeDtypeStruct((n_rows, 128), F32)),
        compiler_params=_params(("arbitrary",)),
        name="moe_gather",
    )(g_tile, g_win, g_n, start, h, row)

    ys = pl.pallas_call(
        _moe_ffn_kernel,
        grid_spec=pltpu.PrefetchScalarGridSpec(
            num_scalar_prefetch=2,
            grid=(n_ffn, ff // tf),
            in_specs=[pl.BlockSpec((MOE_FFN_TILE, d), lambda j, f, te, us: (j, 0)),
                      pl.BlockSpec((MOE_FFN_TILE, 128), lambda j, f, te, us: (j, 0)),
                      pl.BlockSpec((None, d, tf), lambda j, f, te, us: (te[j], 0, f)),
                      pl.BlockSpec((None, d, tf), lambda j, f, te, us: (te[j], 0, f)),
                      pl.BlockSpec((None, tf, d), lambda j, f, te, us: (te[j], f, 0))],
            out_specs=pl.BlockSpec((MOE_FFN_TILE, d), lambda j, f, te, us: (j, 0)),
            scratch_shapes=[pltpu.VMEM((MOE_FFN_TILE, d), F32)]),
        out_shape=jax.ShapeDtypeStruct((n_rows, d), BF16),
        compiler_params=_params(("arbitrary", "arbitrary")),
        name="moe_experts",
    )(ffn_te, ffn_used.astype(jnp.int32), hs, ws, wg, wu, wd)

    return pl.pallas_call(
        functools.partial(_combine_kernel, alpha=alpha),
        grid_spec=pltpu.PrefetchScalarGridSpec(
            num_scalar_prefetch=4,
            grid=(n_pairs,),
            in_specs=[pl.BlockSpec((MOE_TILE, d), lambda s, pw, pt, n, st: (pt[s], 0)),
                      pl.BlockSpec((MOE_WIN, 128), lambda s, pw, pt, n, st: (pw[s], 0)),
                      pl.BlockSpec((MOE_WIN, d), lambda s, pw, pt, n, st: (pw[s], 0)),
                      pl.BlockSpec((None, None, 1, d), lambda s, pw, pt, n, st: (pw[s] // per, 5, 0, 0)),
                      pl.BlockSpec((1, d), lambda s, pw, pt, n, st: (0, 0)),
                      pl.BlockSpec((1, d), lambda s, pw, pt, n, st: (0, 0))],
            out_specs=pl.BlockSpec((MOE_WIN, d), lambda s, pw, pt, n, st: (pw[s], 0)),
            scratch_shapes=[pltpu.VMEM((MOE_WIN, d), F32)]),
        out_shape=jax.ShapeDtypeStruct((m, d), F32),
        compiler_params=_params(("arbitrary",)),
        name="moe_combine",
    )(c_win, c_tile, c_n, start, ys, col, x2, mod4, lng.reshape(1, d), lnb.reshape(1, d))


def _in_columns():
    cols = np.full((N_COLS,), -1, np.int64)
    scale = np.ones((N_COLS,), np.float32)
    qs = 1.0 / math.sqrt(HEAD_DIM)

    def put(dst, src, width, s=1.0):
        cols[dst:dst + width] = np.arange(src, src + width)
        scale[dst:dst + width] = s

    put(COL_MG, IN_MG, COL_NSA_Q - COL_MG)
    put(COL_NSA_Q, IN_NSA_Q, 256, qs)
    put(COL_NSA_CMP, IN_K_CMP, 64)
    put(COL_NSA_CMP + 64, IN_V_CMP, 64)
    put(COL_NSA_SEL, IN_K_SEL, 64)
    put(COL_NSA_SEL + 64, IN_K_SEL, 64)
    put(COL_NSA_WIN, IN_K_WIN, 64)
    put(COL_NSA_WIN + 64, IN_K_WIN, 64)
    for j, h in enumerate(SWA_SLOT_HEADS):
        put(COL_SWA_Q + j * HEAD_DIM, IN_SWA_Q + h * HEAD_DIM, HEAD_DIM, qs)
    put(COL_SWA_K, IN_SWA_K, 128)
    put(COL_DIL_Q, IN_DIL_Q, 768, qs)
    put(COL_DIL_K, IN_DIL_K, 256)
    put(COL_SB_Q, IN_SB_Q, 256, qs)
    put(COL_SB_K, IN_SB_K, 256)
    return cols, scale


def _in_rows_t():
    rows = np.full((N_ROWS_T,), -1, np.int64)

    def put(dst, src, width):
        rows[dst:dst + width] = np.arange(src, src + width)

    put(ROW_NSA_V, IN_V_SEL, 64)
    put(ROW_NSA_V + 64, IN_V_WIN, 64)
    put(ROW_SWA_V, IN_SWA_V, 128)
    put(ROW_DIL_V, IN_DIL_V, 256)
    put(ROW_SB_V, IN_SB_V, 256)
    put(ROW_GATE, IN_GATE, GATE_COLS)
    return rows


def _relayout_w_in(w_in):
    cols, scale = _in_columns()
    w = jnp.take(w_in, jnp.asarray(np.maximum(cols, 0)), axis=1) * jnp.asarray(np.where(cols >= 0, scale, 0.0))
    rows = _in_rows_t()
    wt = jnp.take(w_in, jnp.asarray(np.maximum(rows, 0)), axis=1) * jnp.asarray((rows >= 0).astype(np.float32))
    return w.astype(BF16), wt.T.astype(BF16)


def kernel(x, c, rel_bias, w_ada, b_ada, w_in, w_branch, w_out, cmp_pe, cmp_w1, cmp_w2, swa_sinks, ln_g, ln_b,
           ffn_w_gate, ffn_w_up, ffn_w_down, moe_router, moe_w_gate, moe_w_up, moe_w_down):
    bsz, seq, d = x.shape
    depth = w_ada.shape[0]
    alpha = (2 * depth) ** 0.25
    m = bsz * seq
    nq = seq // BLK

    nsa_heads = tuple(range(NSA_HEADS))
    bias_sel = _bias_tiles(rel_bias, _bucket_tiles(nq, seq, 1), nsa_heads)
    bias_win = _bias_tiles(rel_bias, _bucket_tiles(-(-(NSA_WINDOW - 1) // BLK) + 1, NSA_WINDOW - 1, 1), nsa_heads)
    bias_swa = _bias_tiles(rel_bias, _bucket_tiles(-(-(SWA_WINDOW - 1) // BLK) + 1, SWA_WINDOW - 1, 1),
                           tuple(NSA_HEADS + h for h in SWA_SLOT_HEADS))
    bias_dil = []
    for gi, (win, dil) in enumerate(DIL_PATTERNS):
        h0 = NSA_HEADS + SWA_HEADS + gi * DIL_HEADS_PER_GROUP
        bias_dil.append(_bias_tiles(rel_bias, _bucket_tiles(min(win // BLK + 1, nq), win, dil),
                                    tuple(range(h0, h0 + DIL_HEADS_PER_GROUP))))
    swa_rows = np.concatenate([np.arange(h * HEAD_DIM, (h + 1) * HEAD_DIM) for h in SWA_SLOT_HEADS])

    x2 = x.reshape(m, d)
    for layer in range(depth):
        mod4 = _ada(c, w_ada[layer], b_ada[layer]).reshape(bsz, 6, 1, d)
        w_nat, w_t = _relayout_w_in(w_in[layer])
        z2 = _inproj(x2, mod4, w_nat, seq, sc_idx=1, sh_idx=0)
        zt = _inproj_t(x2, mod4, w_t, seq, sc_idx=1, sh_idx=0).reshape(bsz, nq, N_ROWS_T, BLK)
        z = z2.reshape(bsz, seq, N_COLS)
        chunks = z[:, :, COL_NSA_CMP:COL_NSA_CMP + 2 * HEAD_DIM].reshape(bsz, seq // CMP_STRIDE,
                                                                        CMP_STRIDE * 2 * HEAD_DIM)
        kc, vct = _compress(chunks, cmp_pe[layer], cmp_w1[layer], cmp_w2[layer])
        o_a = _nsa(z, zt, kc, vct, bias_sel, bias_win)
        o_b = _swa(z, zt, bias_swa, swa_sinks[layer])
        o_c = _dilated(z, zt, bias_dil)
        o_d = _stick_breaking(z, zt)
        wb = w_branch[layer]
        wb = jnp.stack([wb[0], wb[1][swa_rows], wb[2], wb[3]]).astype(BF16)
        branches = [a.reshape(m, BRANCH_WIDTH) for a in (o_a, o_b, o_c, o_d)]
        x2 = _merge(branches, z2, wb, w_out[layer].astype(BF16), x2, mod4, ln_g[layer, 0], ln_b[layer, 0], seq,
                    alpha=alpha, g_idx=2)
        i = layer // 2
        if layer % 2 == 0:
            x2 = _ffn(x2, mod4, ffn_w_gate[i].astype(BF16), ffn_w_up[i].astype(BF16), ffn_w_down[i].astype(BF16),
                      ln_g[layer, 1], ln_b[layer, 1], seq, alpha=alpha)
        else:
            x2 = _moe(x2, mod4, moe_router[i], moe_w_gate[i].astype(BF16), moe_w_up[i].astype(BF16),
                      moe_w_down[i].astype(BF16), ln_g[layer, 1], ln_b[layer, 1], seq, alpha=alpha)
    return x2.reshape(bsz, seq, d)
```

```python
import functools
import math

import numpy as np
import jax
import jax.numpy as jnp
from jax import lax
from jax.experimental import pallas as pl
from jax.experimental.pallas import tpu as pltpu

F32 = jnp.float32
BF16 = jnp.bfloat16
HIGHEST = lax.Precision.HIGHEST

HEAD_DIM = 64
BLK = 128
N_SLOTS = 4
NSA_HEADS = 4
CMP_LEN = 32
CMP_STRIDE = 16
SEL_LEN = 64
N_SEL = 16
NSA_WINDOW = 512
SWA_HEADS = 4
SWA_WINDOW = 128
SWA_SLOT_HEADS = (0, 2, 1, 3)
DIL_PATTERNS = ((128, 1), (512, 4), (2048, 16))
DIL_HEADS_PER_GROUP = 4
SB_HEADS = 4
N_BRANCHES = 4
BRANCH_WIDTH = 4 * HEAD_DIM
NUM_BUCKETS = 32
MAX_DISTANCE = 2048
N_EXPERTS = 8
LN_EPS = 1e-5
NEG_INF = -1e30
TINY = 1e-30
FORCE_SCORE = 1e4

IN_NSA_Q, IN_K_CMP, IN_V_CMP, IN_K_SEL, IN_V_SEL, IN_K_WIN, IN_V_WIN, IN_GATE = 0, 256, 320, 384, 448, 512, 576, 640
IN_SWA_Q, IN_SWA_K, IN_SWA_V = 652, 908, 1036
IN_DIL_Q, IN_DIL_K, IN_DIL_V = 1164, 1932, 2188
IN_SB_Q, IN_SB_K, IN_SB_V = 2444, 2700, 2956
IN_MG = 3212
GATE_COLS = NSA_HEADS * 3

COL_MG = 0
COL_NSA_Q = 4096
COL_NSA_CMP = 4352
COL_NSA_SEL = 4480
COL_NSA_WIN = 4608
COL_NSA_GATE = 4736
COL_SWA_Q = 4864
COL_SWA_K = 5120
COL_DIL_Q = 5376
COL_DIL_K = 6144
COL_SB_Q = 6400
COL_SB_K = 6656
N_COLS = 6912
ROW_NSA_V = 0
ROW_SWA_V = 128
ROW_DIL_V = 256
ROW_SB_V = 512
ROW_GATE = 768
N_ROWS_T = 784

VMEM_LIMIT = 56 * 1024 * 1024
ATTN_BATCH = 4


def _params(sem):
    return pltpu.CompilerParams(dimension_semantics=sem, vmem_limit_bytes=VMEM_LIMIT)


def _ln(x):
    mu = jnp.mean(x, axis=-1, keepdims=True)
    xc = x - mu
    var = jnp.mean(xc * xc, axis=-1, keepdims=True)
    return xc * lax.rsqrt(var + LN_EPS)


def _dot_nt(a, b, precision=None):
    return lax.dot_general(a, b, (((1,), (1,)), ((), ())), preferred_element_type=F32, precision=precision)


def _dot(a, b, precision=None):
    return jnp.dot(a, b, preferred_element_type=F32, precision=precision)


def _full(shape):
    return pl.BlockSpec(shape, lambda *_: (0,) * len(shape))


def _ada_kernel(c_ref, w_ref, b_ref, o_ref):
    c = c_ref[...]
    s = c * jax.nn.sigmoid(c)
    o_ref[...] = _dot(s, w_ref[...], HIGHEST) + b_ref[...]


def _ada(c, w, b):
    bsz, d = c.shape
    n = w.shape[1]
    tn = 1024
    return pl.pallas_call(
        _ada_kernel,
        grid=(n // tn,),
        in_specs=[pl.BlockSpec((bsz, d), lambda j: (0, 0)),
                  pl.BlockSpec((d, tn), lambda j: (0, j)),
                  pl.BlockSpec((1, tn), lambda j: (0, j))],
        out_specs=pl.BlockSpec((bsz, tn), lambda j: (0, j)),
        out_shape=jax.ShapeDtypeStruct((bsz, n), F32),
        compiler_params=_params(("parallel",)),
        name="ada",
    )(c, w, b.reshape(1, n))


def _inproj_kernel(x_ref, sc_ref, sh_ref, w_ref, wt_ref, o_ref, ot_ref, *, tn):
    h = (_ln(x_ref[...]) * (1.0 + sc_ref[...]) + sh_ref[...]).astype(BF16)
    for n0 in range(0, w_ref.shape[1], tn):
        o_ref[:, n0:n0 + tn] = _dot(h, w_ref[:, n0:n0 + tn]).astype(o_ref.dtype)
    res = _dot_nt(wt_ref[...], h)
    for j in range(ot_ref.shape[0]):
        ot_ref[j] = res[:, j * BLK:(j + 1) * BLK].astype(ot_ref.dtype)


def _inproj(x2, mod4, w, wt, seq, *, sc_idx, sh_idx, tm=512, n_chunks=3):
    m, d = x2.shape
    n = w.shape[1]
    rows = wt.shape[0]
    per = seq // tm
    resident = lambda shape: pl.BlockSpec(shape, lambda i: (0,) * len(shape), pipeline_mode=pl.Buffered(1))
    return pl.pallas_call(
        functools.partial(_inproj_kernel, tn=n // n_chunks),
        grid=(m // tm,),
        in_specs=[pl.BlockSpec((tm, d), lambda i: (i, 0)),
                  pl.BlockSpec((None, None, 1, d), lambda i: (i // per, sc_idx, 0, 0)),
                  pl.BlockSpec((None, None, 1, d), lambda i: (i // per, sh_idx, 0, 0)),
                  resident(w.shape), resident(wt.shape)],
        out_specs=(pl.BlockSpec((tm, n), lambda i: (i, 0)),
                   pl.BlockSpec((tm // BLK, rows, BLK), lambda i: (i, 0, 0))),
        out_shape=(jax.ShapeDtypeStruct((m, n), BF16),
                   jax.ShapeDtypeStruct((m // BLK, rows, BLK), BF16)),
        compiler_params=_params(("parallel",)),
        name="inproj",
    )(x2, mod4, mod4, w, wt)


def _np_bucket(dist):
    dist = np.maximum(dist, 0)
    max_exact = NUM_BUCKETS // 2
    d_f = np.maximum(dist, 1).astype(np.float32)
    large = max_exact + (np.log(d_f / np.float32(max_exact)) / np.float32(math.log(MAX_DISTANCE / max_exact))
                         * np.float32(NUM_BUCKETS - max_exact)).astype(np.int32)
    large = np.minimum(large, NUM_BUCKETS - 1)
    return np.where(dist < max_exact, dist, large).astype(np.int32)


def _bucket_tiles(n_off, max_dist, dil):
    b = np.arange(BLK)[:, None]
    a = np.arange(BLK)[None, :]
    tiles = []
    for o in range(n_off):
        dist = o * BLK + a - b
        ok = (dist >= 0) & (dist <= max_dist) & (dist % dil == 0)
        tiles.append(np.where(ok, _np_bucket(dist), -1))
    return np.stack(tiles).astype(np.int32)


def _bias_kernel(tbl_ref, bk_ref, o_ref, *, heads):
    slot = pl.program_id(0)
    h = jnp.int32(heads[0])
    for j in range(1, len(heads)):
        h = jnp.where(slot == j, heads[j], h)
    b = bk_ref[...]
    out = jnp.where(b < 0, NEG_INF, 0.0).astype(F32)
    for k in range(NUM_BUCKETS):
        out = jnp.where(b == k, tbl_ref[k, h], out)
    o_ref[...] = out


def _bias_tiles(rel_bias, buckets, heads):
    n_off = buckets.shape[0]
    return pl.pallas_call(
        functools.partial(_bias_kernel, heads=tuple(heads)),
        grid=(len(heads),),
        in_specs=[pl.BlockSpec(memory_space=pltpu.SMEM), _full((n_off, BLK, BLK))],
        out_specs=pl.BlockSpec((n_off, BLK, BLK), lambda s: (0, 0, s)),
        out_shape=jax.ShapeDtypeStruct((n_off, BLK, len(heads) * BLK), F32),
        compiler_params=_params(("parallel",)),
        name="bias_tiles",
    )(rel_bias, jnp.asarray(buckets))


def _slot_queries(q, kw):
    lane = lax.broadcasted_iota(jnp.int32, (BLK, kw), 1)
    zero = jnp.zeros((BLK, kw), q.dtype)
    parts = []
    for j in range(N_SLOTS):
        lo = j * HEAD_DIM
        grp = q[:, (lo // kw) * kw:(lo // kw + 1) * kw]
        inside = (lane >= lo % kw) & (lane < lo % kw + HEAD_DIM)
        parts.append(jnp.where(inside, grp, zero))
    return jnp.concatenate(parts, axis=0)


def _softmax_step(s, vt, state, slot_rows):
    m, l = state[0], state[1]
    m_new = jnp.maximum(m, jnp.max(s, axis=0, keepdims=True))
    alpha = jnp.exp(m - m_new)
    p = jnp.exp(s - m_new)
    l = alpha * l + jnp.sum(p, axis=0, keepdims=True)
    yield
    res = _slot_values(vt, p.astype(BF16), slot_rows)
    yield
    accs = [alpha[:, j * BLK:(j + 1) * BLK] * state[2 + j] + res[j] for j in range(N_SLOTS)]
    return (m_new, l, *accs)


def _softmax_init():
    return (jnp.full((1, N_SLOTS * BLK), NEG_INF, F32), jnp.zeros((1, N_SLOTS * BLK), F32),
            *[jnp.zeros((HEAD_DIM, BLK), F32) for _ in range(N_SLOTS)])


def _store_heads(o_ref, outs):
    o_ref[...] = jnp.transpose(jnp.concatenate(outs, axis=0)).astype(o_ref.dtype)


def _key_rows(k_ref, kb):
    return k_ref[pl.ds(pl.multiple_of(kb * BLK, BLK), BLK), :]


def _slot_values(vt, p, slot_rows):
    if vt.shape[0] > 2 * HEAD_DIM:
        return [_dot(vt[slot_rows[j]:slot_rows[j] + HEAD_DIM, :], p[:, j * BLK:(j + 1) * BLK])
                for j in range(N_SLOTS)]
    res = _dot(vt, p)
    return [res[slot_rows[j]:slot_rows[j] + HEAD_DIM, j * BLK:(j + 1) * BLK] for j in range(N_SLOTS)]


def _window_softmax(qi, wq, k_ref, vt_of, bias_ref, n_off, slot_rows):
    tiles = []
    m = None
    for o in range(n_off):
        kb = qi - o
        kbc = jnp.maximum(kb, 0)
        s = _dot_nt(_key_rows(k_ref, kbc), wq) + bias_ref[o]
        if o > 0:
            s = s + jnp.where(kb >= 0, 0.0, NEG_INF)
        tiles.append((kbc, s))
        mo = jnp.max(s, axis=0, keepdims=True)
        m = mo if m is None else jnp.maximum(m, mo)
    l = None
    res = None
    for kbc, s in tiles:
        p = jnp.exp(s - m)
        lo = jnp.sum(p, axis=0, keepdims=True)
        ro = _slot_values(vt_of(kbc), p.astype(BF16), slot_rows)
        l = lo if l is None else l + lo
        res = ro if res is None else [a + b for a, b in zip(res, ro)]
    return (m, l, *res)


def _run_interleaved(gens):
    results = [None] * len(gens)
    live = list(range(len(gens)))
    while live:
        for i in tuple(live):
            try:
                next(gens[i])
            except StopIteration as done:
                results[i] = done.value
                live.remove(i)
    return results


def _chained_loop(lo, hi, steps, states):
    def body(kb, sts):
        return tuple(_run_interleaved([f(kb, st) for f, st in zip(steps, sts)]))
    return lax.fori_loop(lo, hi, body, tuple(states))


def _compress_kernel(ch_ref, wa_ref, wb_ref, pea_ref, peb_ref, w2k_ref, w2vt_ref, kc_ref, vct_ref):
    ch = ch_ref[...].astype(F32)
    wa = wa_ref[...]
    wb = wb_ref[...]
    first = _dot(ch, wa, HIGHEST)
    second = _dot(ch, wb, HIGHEST)
    pe = (_dot(pea_ref[...], wa, HIGHEST) + _dot(peb_ref[...], wb, HIGHEST))[0:1]
    n = second.shape[0]
    nxt = pltpu.roll(second, n - 1, 0)
    hid = jax.nn.gelu(first + nxt + pe)
    kc_ref[...] = _dot(hid, w2k_ref[...], HIGHEST).astype(kc_ref.dtype)
    vct_ref[...] = _dot_nt(w2vt_ref[...], hid, HIGHEST).astype(vct_ref.dtype)


def _compress(chunks, cmp_pe, cmp_w1, cmp_w2):
    bsz, n_chunk, width = chunks.shape
    per = CMP_LEN // 2
    hd = HEAD_DIM
    w1 = cmp_w1.reshape(2, CMP_LEN, hd, hd)
    wexp = jnp.einsum('cpdn,ce->pcden', w1, jnp.eye(2, dtype=F32)).reshape(CMP_LEN, 2 * hd, 2 * hd)
    wa = wexp[:per].reshape(width, 2 * hd)
    wb = wexp[per:].reshape(width, 2 * hd)
    pe = jnp.transpose(cmp_pe, (1, 0, 2)).reshape(CMP_LEN, 2 * hd)
    pea = jnp.broadcast_to(pe[:per].reshape(1, width), (8, width))
    peb = jnp.broadcast_to(pe[per:].reshape(1, width), (8, width))
    zero = jnp.zeros((hd, hd), F32)
    w2k = jnp.concatenate([jnp.concatenate([cmp_w2[0], cmp_w2[0]], 1), jnp.concatenate([zero, zero], 1)], 0)
    w2vt = jnp.concatenate([zero, cmp_w2[1].T], 1)
    return pl.pallas_call(
        _compress_kernel,
        grid=(bsz,),
        in_specs=[pl.BlockSpec((None, n_chunk, width), lambda b: (b, 0, 0)),
                  _full(wa.shape), _full(wb.shape), _full(pea.shape), _full(peb.shape), _full(w2k.shape),
                  _full(w2vt.shape)],
        out_specs=(pl.BlockSpec((None, n_chunk, 2 * hd), lambda b: (b, 0, 0)),
                   pl.BlockSpec((None, hd, n_chunk), lambda b: (b, 0, 0))),
        out_shape=(jax.ShapeDtypeStruct((bsz, n_chunk, 2 * hd), BF16),
                   jax.ShapeDtypeStruct((bsz, hd, n_chunk), BF16)),
        compiler_params=_params(("parallel",)),
        name="nsa_compress",
    )(chunks, wa, wb, pea, peb, w2k, w2vt)


def _nsa_kernel(q_ref, kc_ref, vct_ref, ovt_ref, ksel_ref, kwin_ref, vt_ref, gate_ref, bsel_ref, bwin_ref, et_ref,
                o_ref, *, n_blk, n_win_off):
    qi = pl.program_id(1)
    t0 = qi * BLK
    nb = q_ref.shape[0]
    n_cmp = kc_ref.shape[1]
    slot_rows = (0,) * N_SLOTS
    shape = (n_cmp, N_SLOTS * BLK)
    tq = t0 + lax.broadcasted_iota(jnp.int32, shape, 1) % BLK
    ci = lax.broadcasted_iota(jnp.int32, shape, 0)
    ok = (ci * CMP_STRIDE + (CMP_LEN - 1)) <= tq
    jb = lax.broadcasted_iota(jnp.int32, (n_blk, BLK), 0)
    cur = (t0 + lax.broadcasted_iota(jnp.int32, (n_blk, BLK), 1)) // SEL_LEN
    forced = (jb == 0) | (jb == cur) | (jb == cur - 1)

    o_cmps, o_wins, sel_steps = [], [], []
    for bb in range(nb):
        wq = _slot_queries(q_ref[bb], 2 * HEAD_DIM)

        s = jnp.where(ok, _dot_nt(kc_ref[bb], wq), NEG_INF)
        m = jnp.max(s, axis=0, keepdims=True)
        p = jnp.where(ok, jnp.exp(s - m), 0.0)
        p = p / jnp.maximum(jnp.sum(p, axis=0, keepdims=True), TINY)
        o_cmps.append(_dot(vct_ref[bb], p.astype(BF16)))
        psum = p[:, 0:BLK]
        for j in range(1, N_SLOTS):
            psum = psum + p[:, j * BLK:(j + 1) * BLK]

        hi = psum.astype(BF16)
        lo = (psum - hi.astype(F32)).astype(BF16)
        score = _dot(ovt_ref[...], hi) + _dot(ovt_ref[...], lo)
        score = jnp.where(forced, FORCE_SCORE, jnp.where(jb > cur, NEG_INF, score))
        rank = jnp.zeros((n_blk, BLK), F32)
        for c in range(n_blk):
            row = score[c:c + 1, :]
            beats = (row > score) | ((row == score) & (jb > c))
            rank = rank + jnp.where(beats, 1.0, 0.0)
        sel = jnp.where(rank < N_SEL, 1.0, 0.0).astype(BF16)

        st = _window_softmax(qi, wq, kwin_ref.at[bb], lambda kb, bb=bb: vt_ref[bb, kb, HEAD_DIM:2 * HEAD_DIM, :],
                             bwin_ref, n_win_off, slot_rows)
        o_wins.append([st[2 + j] / st[1][:, j * BLK:(j + 1) * BLK] for j in range(N_SLOTS)])

        def sel_step(kb, state, bb=bb, wq=wq, sel=sel):
            pen = (_dot(et_ref[kb], sel) - 1.0) * (-NEG_INF)
            pen = jnp.concatenate([pen] * N_SLOTS, axis=1)
            sc = _dot_nt(_key_rows(ksel_ref.at[bb], kb), wq) + bsel_ref[qi - kb] + pen
            yield
            return (yield from _softmax_step(sc, vt_ref[bb, kb, 0:HEAD_DIM, :], state, slot_rows))

        sel_steps.append(sel_step)

    sel_states = _run_interleaved([f(qi, _softmax_init()) for f in sel_steps])
    sel_states = _chained_loop(0, qi, sel_steps, sel_states)
    for bb in range(nb):
        st = sel_states[bb]
        gate = jax.nn.sigmoid(gate_ref[bb].astype(F32))
        outs = []
        for j in range(N_SLOTS):
            o_sel = st[2 + j] / st[1][:, j * BLK:(j + 1) * BLK]
            outs.append(gate[3 * j:3 * j + 1] * o_cmps[bb][:, j * BLK:(j + 1) * BLK]
                        + gate[3 * j + 1:3 * j + 2] * o_sel + gate[3 * j + 2:3 * j + 3] * o_wins[bb][j])
        _store_heads(o_ref.at[bb], outs)


def _nsa(z, zt, kc, vct, bias_sel, bias_win, nb=ATTN_BATCH):
    bsz, seq, _ = z.shape
    nq = seq // BLK
    n_cmp = kc.shape[1]
    n_blk = seq // SEL_LEN
    assert n_blk >= N_SEL
    ci = np.arange(n_cmp)[None, :]
    sj = np.arange(n_blk)[:, None]
    overlap_t = ((ci * CMP_STRIDE + CMP_LEN - 1 >= sj * SEL_LEN) & (ci * CMP_STRIDE < (sj + 1) * SEL_LEN)
                 & (ci < n_cmp - 1)).astype(np.float32)
    key_blk = (np.arange(seq) // SEL_LEN).reshape(nq, BLK, 1)
    expand_t = (key_blk == np.arange(n_blk)[None, None, :]).astype(np.float32)
    qw = NSA_HEADS * HEAD_DIM
    kw = 2 * HEAD_DIM
    return pl.pallas_call(
        functools.partial(_nsa_kernel, n_blk=n_blk, n_win_off=bias_win.shape[0]),
        grid=(bsz // nb, nq),
        in_specs=[pl.BlockSpec((nb, BLK, qw), lambda b, i: (b, i, COL_NSA_Q // qw)),
                  pl.BlockSpec((nb, n_cmp, kw), lambda b, i: (b, 0, 0)),
                  pl.BlockSpec((nb, HEAD_DIM, n_cmp), lambda b, i: (b, 0, 0)),
                  _full(overlap_t.shape),
                  pl.BlockSpec((nb, seq, kw), lambda b, i: (b, 0, COL_NSA_SEL // kw)),
                  pl.BlockSpec((nb, seq, kw), lambda b, i: (b, 0, COL_NSA_WIN // kw)),
                  pl.BlockSpec((nb, nq, kw, BLK), lambda b, i: (b, 0, ROW_NSA_V // kw, 0)),
                  pl.BlockSpec((nb, None, 16, BLK), lambda b, i: (b, i, ROW_GATE // 16, 0)),
                  _full(bias_sel.shape), _full(bias_win.shape), _full(expand_t.shape)],
        out_specs=pl.BlockSpec((nb, BLK, qw), lambda b, i: (b, i, 0)),
        out_shape=jax.ShapeDtypeStruct((bsz, seq, qw), BF16),
        compiler_params=_params(("parallel", "parallel")),
        name="nsa_mixer",
    )(z, kc, vct, jnp.asarray(overlap_t, dtype=BF16), z, z, zt, zt, bias_sel, bias_win,
      jnp.asarray(expand_t, dtype=BF16))


def _swa_kernel(q_ref, k_ref, vt_ref, bias_ref, sink_ref, o_ref, *, n_off):
    qi = pl.program_id(1)
    slot_rows = tuple((j % 2) * HEAD_DIM for j in range(N_SLOTS))
    for bb in range(q_ref.shape[0]):
        wq = _slot_queries(q_ref[bb], 2 * HEAD_DIM)
        st = _window_softmax(qi, wq, k_ref.at[bb], lambda kb, bb=bb: vt_ref[bb, kb], bias_ref, n_off, slot_rows)
        m, l = st[0], st[1]
        outs = []
        for j in range(N_SLOTS):
            sink = sink_ref[SWA_SLOT_HEADS[j]]
            mj = m[:, j * BLK:(j + 1) * BLK]
            lj = l[:, j * BLK:(j + 1) * BLK]
            m2 = jnp.maximum(mj, sink)
            scale = jnp.exp(mj - m2)
            den = lj * scale + jnp.exp(sink - m2)
            outs.append(st[2 + j] * (scale / den))
        _store_heads(o_ref.at[bb], outs)


def _swa(z, zt, bias, sinks, nb=ATTN_BATCH):
    bsz, seq, _ = z.shape
    nq = seq // BLK
    qw = SWA_HEADS * HEAD_DIM
    kw = 2 * HEAD_DIM
    return pl.pallas_call(
        functools.partial(_swa_kernel, n_off=bias.shape[0]),
        grid=(bsz // nb, nq),
        in_specs=[pl.BlockSpec((nb, BLK, qw), lambda b, i: (b, i, COL_SWA_Q // qw)),
                  pl.BlockSpec((nb, seq, kw), lambda b, i: (b, 0, COL_SWA_K // kw)),
                  pl.BlockSpec((nb, nq, kw, BLK), lambda b, i: (b, 0, ROW_SWA_V // kw, 0)),
                  _full(bias.shape),
                  pl.BlockSpec(memory_space=pltpu.SMEM)],
        out_specs=pl.BlockSpec((nb, BLK, qw), lambda b, i: (b, i, 0)),
        out_shape=jax.ShapeDtypeStruct((bsz, seq, qw), BF16),
        compiler_params=_params(("parallel", "parallel")),
        name="swa_mixer",
    )(z, z, zt, bias, sinks)


def _dil_kernel(q0_ref, q1_ref, q2_ref, k_ref, vt_ref, b0_ref, b1_ref, b2_ref, o_ref):
    qi = pl.program_id(1)
    nb = q0_ref.shape[0]
    slot_rows = tuple(j * HEAD_DIM for j in range(N_SLOTS))
    kw = N_SLOTS * HEAD_DIM
    short, far_steps = [], []
    for bb in range(nb):
        vt_of = lambda kb, bb=bb: vt_ref[bb, kb]
        short.append([_window_softmax(qi, _slot_queries(q_ref[bb], kw), k_ref.at[bb], vt_of, bias_ref,
                                      bias_ref.shape[0], slot_rows)
                      for q_ref, bias_ref in ((q0_ref, b0_ref), (q1_ref, b1_ref))])
        wq2 = _slot_queries(q2_ref[bb], kw)

        def far_step(kb, state, bb=bb, wq2=wq2, vt_of=vt_of):
            sc = _dot_nt(_key_rows(k_ref.at[bb], kb), wq2) + b2_ref[qi - kb]
            yield
            return (yield from _softmax_step(sc, vt_of(kb), state, slot_rows))

        far_steps.append(far_step)
    far_states = _run_interleaved([f(qi, _softmax_init()) for f in far_steps])
    far_states = _chained_loop(jnp.maximum(qi - (b2_ref.shape[0] - 1), 0), qi, far_steps, far_states)
    for bb in range(nb):
        groups = short[bb] + [far_states[bb]]
        lses = [st[0] + jnp.log(st[1]) for st in groups]
        top = jnp.maximum(jnp.maximum(lses[0], lses[1]), lses[2])
        es = [jnp.exp(v - top) for v in lses]
        den = es[0] + es[1] + es[2]
        outs = []
        for j in range(N_SLOTS):
            sl = slice(j * BLK, (j + 1) * BLK)
            acc = jnp.zeros((HEAD_DIM, BLK), F32)
            for st, e in zip(groups, es):
                acc = acc + (e[:, sl] / den[:, sl]) * (st[2 + j] / st[1][:, sl])
            outs.append(acc)
        _store_heads(o_ref.at[bb], outs)


def _dilated(z, zt, biases, nb=ATTN_BATCH):
    bsz, seq, _ = z.shape
    nq = seq // BLK
    qw = DIL_HEADS_PER_GROUP * HEAD_DIM
    qspec = lambda g: pl.BlockSpec((nb, BLK, qw), lambda b, i: (b, i, COL_DIL_Q // qw + g))
    return pl.pallas_call(
        _dil_kernel,
        grid=(bsz // nb, nq),
        in_specs=[qspec(0), qspec(1), qspec(2),
                  pl.BlockSpec((nb, seq, qw), lambda b, i: (b, 0, COL_DIL_K // qw)),
                  pl.BlockSpec((nb, nq, qw, BLK), lambda b, i: (b, 0, ROW_DIL_V // qw, 0)),
                  _full(biases[0].shape), _full(biases[1].shape), _full(biases[2].shape)],
        out_specs=pl.BlockSpec((nb, BLK, qw), lambda b, i: (b, i, 0)),
        out_shape=jax.ShapeDtypeStruct((bsz, seq, qw), BF16),
        compiler_params=_params(("parallel", "parallel")),
        name="dilated_mixer",
    )(z, z, z, z, zt, *biases)


def _sb_kernel(q_ref, k_ref, vt_ref, u_ref, o_ref):
    qi = pl.program_id(1)
    nb = q_ref.shape[0]
    u = u_ref[...]
    slot_rows = tuple(j * HEAD_DIM for j in range(N_SLOTS))
    shape = (BLK, N_SLOTS * BLK)
    strict =lax.broadcasted_iota(jnp.int32, shape, 0) < lax.broadcasted_iota(jnp.int32, shape, 1) % BLK

    def step(bb, wq, kb, state, diag):
        run = state[0]
        zz = _dot_nt(_key_rows(k_ref.at[bb], kb), wq)
        yield
        ls = jnp.minimum(zz, 0.0) - jnp.log(1.0 + jnp.exp(-jnp.abs(zz)))
        lf = ls - zz
        if diag:
            lf = jnp.where(strict, lf, 0.0)
        hi = lf.astype(BF16)
        lo = (lf - hi.astype(F32)).astype(BF16)
        after = _dot(u, jnp.concatenate([hi, lo], axis=0)) + run
        yield
        a = jnp.exp(ls + after)
        if diag:
            a = jnp.where(strict, a, 0.0)
        res = _slot_values(vt_ref[bb, kb], a.astype(BF16), slot_rows)
        yield
        accs = [state[1 + j] + res[j] for j in range(N_SLOTS)]
        return (run + jnp.sum(lf, axis=0, keepdims=True), *accs)

    init = (jnp.zeros((1, N_SLOTS * BLK), F32), *[jnp.zeros((HEAD_DIM, BLK), F32) for _ in range(N_SLOTS)])
    wqs = [_slot_queries(q_ref[bb], SB_HEADS * HEAD_DIM) for bb in range(nb)]
    steps = [lambda t, st, bb=bb: step(bb, wqs[bb], qi - 1 - t, st, False) for bb in range(nb)]
    states = _run_interleaved([step(bb, wqs[bb], qi, init, True) for bb in range(nb)])
    states = _chained_loop(0, qi, steps, states)
    for bb in range(nb):
        _store_heads(o_ref.at[bb], list(states[bb][1:]))


def _stick_breaking(z, zt, nb=ATTN_BATCH):
    bsz, seq, _ = z.shape
    nq = seq // BLK
    w = SB_HEADS * HEAD_DIM
    later = (np.arange(BLK)[None, :] > np.arange(BLK)[:, None]).astype(np.float32)
    later = np.concatenate([later, later], axis=1)
    return pl.pallas_call(
        _sb_kernel,
        grid=(bsz // nb, nq),
        in_specs=[pl.BlockSpec((nb, BLK, w), lambda b, i: (b, i, COL_SB_Q // w)),
                  pl.BlockSpec((nb, seq, w), lambda b, i: (b, 0, COL_SB_K // w)),
                  pl.BlockSpec((nb, nq, w, BLK), lambda b, i: (b, 0, ROW_SB_V // w, 0)),
                  _full((BLK, 2 * BLK))],
        out_specs=pl.BlockSpec((nb, BLK, w), lambda b, i: (b, i, 0)),
        out_shape=jax.ShapeDtypeStruct((bsz, seq, w), BF16),
        compiler_params=_params(("parallel", "parallel")),
        name="stick_breaking",
    )(z, z, zt, jnp.asarray(later, dtype=BF16))


def _merge_kernel(oa_ref, ob_ref, oc_ref, od_ref, mg_ref, wb_ref, wo_ref, x_ref, g_ref, lng_ref, lnb_ref, o_ref,
                  *, alpha):
    d = x_ref.shape[1]
    merged = jnp.zeros((x_ref.shape[0], d), F32)
    for b, ref in enumerate((oa_ref, ob_ref, oc_ref, od_ref)):
        proj = _dot(ref[...], wb_ref[b])
        merged = merged + jax.nn.sigmoid(mg_ref[:, b * d:(b + 1) * d].astype(F32)) * proj
    y = _dot(merged.astype(BF16), wo_ref[...])
    v = alpha * x_ref[...] + g_ref[...] * y
    o_ref[...] = _ln(v) * lng_ref[...] + lnb_ref[...]


def _merge(branches, z2, wb, wo, x2, mod4, lng, lnb, seq, *, alpha, g_idx, tm=512):
    m, d = x2.shape
    bw = BRANCH_WIDTH
    per = seq // tm
    row = lambda w: pl.BlockSpec((tm, w), lambda i: (i, 0))
    return pl.pallas_call(
        functools.partial(_merge_kernel, alpha=alpha),
        grid=(m // tm,),
        in_specs=[row(bw), row(bw), row(bw), row(bw),
                  pl.BlockSpec((tm, N_BRANCHES * d), lambda i: (i, COL_MG // (N_BRANCHES * d))),
                  _full(wb.shape), _full(wo.shape),
                  row(d),
                  pl.BlockSpec((None, None, 1, d), lambda i: (i // per, g_idx, 0, 0)),
                  _full((1, d)), _full((1, d))],
        out_specs=row(d),
        out_shape=jax.ShapeDtypeStruct((m, d), F32),
        compiler_params=_params(("parallel",)),
        name="merge_out",
    )(*branches, z2, wb, wo, x2, mod4, lng.reshape(1, d), lnb.reshape(1, d))


def _ffn_kernel(x_ref, sc_ref, sh_ref, g_ref, lng_ref, lnb_ref, wg_ref, wu_ref, wd_ref, o_ref, *, alpha, tf):
    x = x_ref[...]
    h = (_ln(x) * (1.0 + sc_ref[...]) + sh_ref[...]).astype(BF16)
    y = None
    for f0 in range(0, wg_ref.shape[1], tf):
        a = _dot(h, wg_ref[:, f0:f0 + tf])
        u = _dot(h, wu_ref[:, f0:f0 + tf])
        act = (a * jax.nn.sigmoid(a)) * u
        t = _dot(act.astype(BF16), wd_ref[f0:f0 + tf, :])
        y = t if y is None else y + t
    v = alpha * x + g_ref[...] * y
    o_ref[...] = _ln(v) * lng_ref[...] + lnb_ref[...]


def _ffn(x2, mod4, wg, wu, wd, lng, lnb, seq, *, alpha, tm=512, tf=1408):
    m, d = x2.shape
    per = seq // tm
    modspec = lambda k: pl.BlockSpec((None, None, 1, d), lambda i: (i // per, k, 0, 0))
    resident = lambda shape: pl.BlockSpec(shape, lambda i: (0,) * len(shape), pipeline_mode=pl.Buffered(1))
    return pl.pallas_call(
        functools.partial(_ffn_kernel, alpha=alpha, tf=tf),
        grid=(m // tm,),
        in_specs=[pl.BlockSpec((tm, d), lambda i: (i, 0)),
                  modspec(4), modspec(3), modspec(5),
                  _full((1, d)), _full((1, d)),
                  resident(wg.shape), resident(wu.shape), resident(wd.shape)],
        out_specs=pl.BlockSpec((tm, d), lambda i: (i, 0)),
        out_shape=jax.ShapeDtypeStruct((m, d), F32),
        compiler_params=_params(("parallel",)),
        name="ffn_dense",
    )(x2, mod4, mod4, mod4, lng.reshape(1, d), lnb.reshape(1, d), wg, wu, wd)


MOE_WIN = 512
MOE_TILE = 256
MOE_FFN_TILE = 512
INFO_E1, INFO_E2, INFO_W1, INFO_W2, INFO_R1, INFO_R2 = range(6)


def _router_kernel(x_ref, sc_ref, sh_ref, wr_ref, ls_ref, h_ref, col_ref, row_ref, cum_ref, cnt_ref, carry_ref):
    @pl.when(pl.program_id(0) == 0)
    def _():
        carry_ref[...] = jnp.zeros_like(carry_ref)

    tm = x_ref.shape[0]
    lane = lax.broadcasted_iota(jnp.int32, (tm, 128), 1).astype(F32)
    h32 = _ln(x_ref[...]) * (1.0 + sc_ref[...]) + sh_ref[...]
    h_ref[...] = h32.astype(BF16)
    logits = _dot(h32, wr_ref[...], HIGHEST)
    logits = jnp.where(lane < N_EXPERTS, logits, -jnp.inf)
    t1 = jnp.max(logits, axis=1, keepdims=True)
    i1 = jnp.min(jnp.where(logits == t1, lane, 128.0), axis=1, keepdims=True)
    rest = jnp.where(lane == i1, -jnp.inf, logits)
    t2 = jnp.max(rest, axis=1, keepdims=True)
    i2 = jnp.min(jnp.where(rest == t2, lane, 128.0), axis=1, keepdims=True)
    e2 = jnp.exp(t2 - t1)
    den = 1.0 + e2
    w1 = 1.0 / den
    w2 = e2 / den
    onehot = jnp.where(lane == i1, 1.0, jnp.where(lane == i2, 1.0, 0.0))
    before = _dot(ls_ref[...], onehot.astype(BF16)) + carry_ref[0:1, :]
    r1 = jnp.sum(jnp.where(lane == i1, before, 0.0), axis=1, keepdims=True)
    r2 = jnp.sum(jnp.where(lane == i2, before, 0.0), axis=1, keepdims=True)
    info = jnp.zeros((tm, 128), F32)
    for k, v in ((INFO_E1, i1), (INFO_E2, i2), (INFO_W1, w1), (INFO_W2, w2), (INFO_R1, r1), (INFO_R2, r2)):
        info = jnp.where(lane == k, v, info)
    col_ref[...] = info
    row_ref[...] = jnp.transpose(info)[0:8, :]
    cnt = jnp.sum(onehot, axis=0, keepdims=True)
    cum_ref[...] = carry_ref[...]
    cnt_ref[...] = jnp.broadcast_to(cnt, cnt_ref.shape)
    carry_ref[...] = carry_ref[...] + cnt


def _router(x2, mod4, wr, seq):
    m, d = x2.shape
    tm = MOE_WIN
    per = seq // tm
    n_win = m // tm
    wr_pad = jnp.pad(wr, ((0, 0), (0, 128 - wr.shape[1])))
    earlier = (np.arange(tm)[:, None] > np.arange(tm)[None, :]).astype(np.float32)
    modspec = lambda k: pl.BlockSpec((None, None, 1, d), lambda i: (i // per, k, 0, 0))
    return pl.pallas_call(
        _router_kernel,
        grid=(n_win,),
        in_specs=[pl.BlockSpec((tm, d), lambda i: (i, 0)), modspec(4), modspec(3),
                  _full((d, 128)), _full((tm, tm))],
        out_specs=(pl.BlockSpec((tm, d), lambda i: (i, 0)),
                   pl.BlockSpec((tm, 128), lambda i: (i, 0)),
                   pl.BlockSpec((None, 8, tm), lambda i: (i, 0, 0)),
                   pl.BlockSpec((None, 8, 128), lambda i: (i, 0, 0)),
                   pl.BlockSpec((None, 8, 128), lambda i: (i, 0, 0))),
        out_shape=(jax.ShapeDtypeStruct((m, d), BF16),
                   jax.ShapeDtypeStruct((m, 128), F32),
                   jax.ShapeDtypeStruct((n_win, 8, tm), F32),
                   jax.ShapeDtypeStruct((n_win, 8, 128), F32),
                   jax.ShapeDtypeStruct((n_win, 8, 128), F32)),
        scratch_shapes=[pltpu.VMEM((8, 128), F32)],
        compiler_params=_params(("arbitrary",)),
        name="moe_router",
    )(x2, mod4, mod4, wr_pad, jnp.asarray(earlier, dtype=BF16))


def _expert_offset(e, start_ref):
    off = jnp.zeros_like(e)
    for k in range(N_EXPERTS):
        off = jnp.where(e == k, start_ref[k].astype(F32), off)
    return off


def _pair_flags(s, n, key_ref):
    last_idx = key_ref.shape[0] - 1
    key = key_ref[s]
    first = (s == 0) | (key != key_ref[jnp.maximum(s - 1, 0)])
    last = (s == n - 1) | (key != key_ref[jnp.minimum(s + 1, last_idx)])
    return s < n, first, last


def _gather_kernel(pt_ref, pw_ref, n_ref, start_ref, h_ref, row_ref, hs_ref, acc_ref):
    s = pl.program_id(0)
    active, first, last = _pair_flags(s, n_ref[0], pt_ref)

    @pl.when(active & first)
    def _():
        acc_ref[...] = jnp.zeros_like(acc_ref)

    @pl.when(active)
    def _():
        info = row_ref[...]
        half = info.shape[1] // 2
        p1 = info[INFO_R1:INFO_R1 + 1] + _expert_offset(info[INFO_E1:INFO_E1 + 1], start_ref)
        p2 = info[INFO_R2:INFO_R2 + 1] + _expert_offset(info[INFO_E2:INFO_E2 + 1], start_ref)
        rows = (pt_ref[s] * MOE_TILE + lax.broadcasted_iota(jnp.int32, (MOE_TILE, half), 0)).astype(F32)
        part = None
        for c in range(2):
            sl = slice(c * half, (c + 1) * half)
            perm = jnp.where(rows == p1[:, sl], 1.0, jnp.where(rows == p2[:, sl], 1.0, 0.0)).astype(BF16)
            term = _dot(perm, h_ref[sl, :])
            part = term if part is None else part + term
        acc_ref[...] += part

    @pl.when(active & last)
    def _():
        hs_ref[...] = acc_ref[...].astype(hs_ref.dtype)


def _moe_ffn_kernel(te_ref, used_ref, hs_ref, wg_ref, wu_ref, wd_ref, ys_ref, *, tf):
    used = used_ref[pl.program_id(0)] > 0

    @pl.when(used)
    def _():
        h = hs_ref[...]
        y = None
        for f0 in range(0, wg_ref.shape[1], tf):
            a = _dot(h, wg_ref[:, f0:f0 + tf])
            u = _dot(h, wu_ref[:, f0:f0 + tf])
            act = (a * jax.nn.sigmoid(a)) * u
            t = _dot(act.astype(BF16), wd_ref[f0:f0 + tf, :])
            y = t if y is None else y + t
        ys_ref[...] = y.astype(ys_ref.dtype)

    @pl.when(jnp.logical_not(used))
    def _():
        ys_ref[...] = jnp.zeros_like(ys_ref)


def _combine_kernel(pw_ref, pt_ref, n_ref, start_ref, ys_ref, col_ref, x_ref, g_ref, lng_ref, lnb_ref, o_ref,
                    acc_ref, pos_ref, *, alpha):
    s = pl.program_id(0)
    active, first, last = _pair_flags(s, n_ref[0], pw_ref)

    @pl.when(active & first)
    def _():
        acc_ref[...] = jnp.zeros_like(acc_ref)
        info = col_ref[...]
        pos_ref[:, 0:1] = info[:, INFO_R1:INFO_R1 + 1] + _expert_offset(info[:, INFO_E1:INFO_E1 + 1], start_ref)
        pos_ref[:, 1:2] = info[:, INFO_R2:INFO_R2 + 1] + _expert_offset(info[:, INFO_E2:INFO_E2 + 1], start_ref)

    @pl.when(active)
    def _():
        win = col_ref.shape[0]
        p1 = pos_ref[:, 0:1]
        p2 = pos_ref[:, 1:2]
        row0 = (pt_ref[s] * MOE_TILE).astype(F32)
        in1 = (p1 >= row0) & (p1 < row0 + MOE_TILE)
        in2 = (p2 >= row0) & (p2 < row0 + MOE_TILE)
        weight = jnp.where(in1, col_ref[:, INFO_W1:INFO_W1 + 1], jnp.where(in2, col_ref[:, INFO_W2:INFO_W2 + 1], 0.0))
        cols = lax.broadcasted_iota(jnp.int32, (win, MOE_TILE), 1).astype(F32) + row0
        perm = jnp.where(cols == p1, 1.0, jnp.where(cols == p2, 1.0, 0.0)).astype(BF16)
        acc_ref[...] += weight * _dot(perm, ys_ref[...])

    @pl.when(active & last)
    def _():
        v = alpha * x_ref[...] + g_ref[...] * acc_ref[...]
        o_ref[...] = _ln(v) * lng_ref[...] + lnb_ref[...]


def _pair_list(mask, n_pairs):
    flat = mask.reshape(-1)
    n = jnp.sum(flat.astype(jnp.int32))
    idx = jnp.nonzero(flat, size=n_pairs, fill_value=0)[0].astype(jnp.int32)
    idx = jnp.where(jnp.arange(n_pairs) < n, idx, idx[jnp.maximum(n - 1, 0)])
    return idx // mask.shape[1], idx % mask.shape[1], n.reshape(1)


def _moe(x2, mod4, wr, wg, wu, wd, lng, lnb, seq, *, alpha, tf=1792):
    m, d = x2.shape
    n_exp, _, ff = wg.shape
    n_win = m // MOE_WIN
    per = seq // MOE_WIN
    n_rows = 2 * m + n_exp * MOE_FFN_TILE
    n_tiles = n_rows // MOE_TILE
    n_ffn = n_rows // MOE_FFN_TILE
    n_pairs = n_exp * n_win + n_tiles + 3 * n_exp

    h, col, row, cum, cnt = _router(x2, mod4, wr, seq)

    cum = cum[:, 0, :n_exp].astype(jnp.int32)
    cnt = cnt[:, 0, :n_exp].astype(jnp.int32)
    total = cum[-1] + cnt[-1]
    padded = (total + MOE_FFN_TILE - 1) // MOE_FFN_TILE * MOE_FFN_TILE
    end = jnp.cumsum(padded)
    start = (end - padded).astype(jnp.int32)

    def tile_expert(tile_rows):
        row0 = jnp.arange(n_rows // tile_rows, dtype=jnp.int32) * tile_rows
        te = jnp.minimum(jnp.sum((row0[:, None] >= end[None, :]).astype(jnp.int32), axis=1), n_exp - 1)
        return row0, te, row0 < end[-1]

    row0, te, used = tile_expert(MOE_TILE)
    local0 = row0 - start[te]
    lo = jnp.take(cum.T, te, axis=0)
    hi = lo + jnp.take(cnt.T, te, axis=0)
    meet = (lo < local0[:, None] + MOE_TILE) & (hi > local0[:, None]) & (hi > lo) & used[:, None]
    visit = meet.at[:, 0].set(meet[:, 0] | jnp.logical_not(jnp.any(meet, axis=1)))
    g_tile, g_win, g_n = _pair_list(visit, n_pairs)
    c_win, c_tile, c_n = _pair_list(meet.T, n_pairs)
    _, ffn_te, ffn_used = tile_expert(MOE_FFN_TILE)

    hs = pl.pallas_call(
        _gather_kernel,
        grid_spec=pltpu.PrefetchScalarGridSpec(
            num_scalar_prefetch=4,
            grid=(n_pairs,),
            in_specs=[pl.BlockSpec((MOE_WIN, d), lambda s, pt, pw, n, st: (pw[s], 0)),
                      pl.BlockSpec((None, 8, MOE_WIN), lambda s, pt, pw, n, st: (pw[s], 0, 0))],
            out_specs=pl.BlockSpec((MOE_TILE, d), lambda s, pt, pw, n, st: (pt[s], 0)),
            scratch_shapes=[pltpu.VMEM((MOE_TILE, d), F32)]),
        out_shape=jax.ShapeDtypeStruct((n_rows, d), BF16),
        compiler_params=_params(("arbitrary",)),
        name="moe_gather",
    )(g_tile, g_win, g_n, start, h, row)

    expert = lambda shape: pl.BlockSpec((None,) + shape, lambda j, te, us: (te[j], 0, 0),
                                        pipeline_mode=pl.Buffered(1))
    ys = pl.pallas_call(
        functools.partial(_moe_ffn_kernel, tf=tf),
        grid_spec=pltpu.PrefetchScalarGridSpec(
            num_scalar_prefetch=2,
            grid=(n_ffn,),
            in_specs=[pl.BlockSpec((MOE_FFN_TILE, d), lambda j, te, us: (j, 0)),
                      expert((d, ff)), expert((d, ff)), expert((ff, d))],
            out_specs=pl.BlockSpec((MOE_FFN_TILE, d), lambda j, te, us: (j, 0))),
        out_shape=jax.ShapeDtypeStruct((n_rows, d), BF16),
        compiler_params=_params(("arbitrary",)),
        name="moe_experts",
    )(ffn_te, ffn_used.astype(jnp.int32), hs, wg, wu, wd)

    return pl.pallas_call(
        functools.partial(_combine_kernel, alpha=alpha),
        grid_spec=pltpu.PrefetchScalarGridSpec(
            num_scalar_prefetch=4,
            grid=(n_pairs,),
            in_specs=[pl.BlockSpec((MOE_TILE, d), lambda s, pw, pt, n, st: (pt[s], 0)),
                      pl.BlockSpec((MOE_WIN, 128), lambda s, pw, pt, n, st: (pw[s], 0)),
                      pl.BlockSpec((MOE_WIN, d), lambda s, pw, pt, n, st: (pw[s], 0)),
                      pl.BlockSpec((None, None, 1, d), lambda s, pw, pt, n, st: (pw[s] // per, 5, 0, 0)),
                      pl.BlockSpec((1, d), lambda s, pw, pt, n, st: (0, 0)),
                      pl.BlockSpec((1, d), lambda s, pw, pt, n, st: (0, 0))],
            out_specs=pl.BlockSpec((MOE_WIN, d), lambda s, pw, pt, n, st: (pw[s], 0)),
            scratch_shapes=[pltpu.VMEM((MOE_WIN, d), F32), pltpu.VMEM((MOE_WIN, 128), F32)]),
        out_shape=jax.ShapeDtypeStruct((m, d), F32),
        compiler_params=_params(("arbitrary",)),
        name="moe_combine",
    )(c_win, c_tile, c_n, start, ys, col, x2, mod4, lng.reshape(1, d), lnb.reshape(1, d))


def _in_columns():
    cols = np.full((N_COLS,), -1, np.int64)
    scale = np.ones((N_COLS,), np.float32)
    qs = 1.0 / math.sqrt(HEAD_DIM)

    def put(dst, src, width, s=1.0):
        cols[dst:dst + width] = np.arange(src, src + width)
        scale[dst:dst + width] = s

    put(COL_MG, IN_MG, COL_NSA_Q - COL_MG)
    put(COL_NSA_Q, IN_NSA_Q, 256, qs)
    put(COL_NSA_CMP, IN_K_CMP, 64)
    put(COL_NSA_CMP + 64, IN_V_CMP, 64)
    put(COL_NSA_SEL, IN_K_SEL, 64)
    put(COL_NSA_SEL + 64, IN_K_SEL, 64)
    put(COL_NSA_WIN, IN_K_WIN, 64)
    put(COL_NSA_WIN + 64, IN_K_WIN, 64)
    for j, h in enumerate(SWA_SLOT_HEADS):
        put(COL_SWA_Q + j * HEAD_DIM, IN_SWA_Q + h * HEAD_DIM, HEAD_DIM, qs)
    put(COL_SWA_K, IN_SWA_K, 128)
    put(COL_DIL_Q, IN_DIL_Q, 768, qs)
    put(COL_DIL_K, IN_DIL_K, 256)
    put(COL_SB_Q, IN_SB_Q, 256, qs)
    put(COL_SB_K, IN_SB_K, 256)
    return cols, scale


def _in_rows_t():
    rows = np.full((N_ROWS_T,), -1, np.int64)

    def put(dst, src, width):
        rows[dst:dst + width] = np.arange(src, src + width)

    put(ROW_NSA_V, IN_V_SEL, 64)
    put(ROW_NSA_V + 64, IN_V_WIN, 64)
    put(ROW_SWA_V, IN_SWA_V, 128)
    put(ROW_DIL_V, IN_DIL_V, 256)
    put(ROW_SB_V, IN_SB_V, 256)
    put(ROW_GATE, IN_GATE, GATE_COLS)
    return rows


def _select_columns(w, src):
    d = w.shape[0]
    parts = []
    i = 0
    while i < len(src):
        j = i + 1
        while j < len(src) and (src[j] == src[j - 1] + 1 if src[i] >= 0 else src[j] < 0):
            j += 1
        parts.append(w[:, src[i]:src[i] + (j - i)] if src[i] >= 0 else jnp.zeros((d, j - i), w.dtype))
        i = j
    return jnp.concatenate(parts, axis=1)


def _relayout_w_in(w_in):
    cols, scale = _in_columns()
    w = _select_columns(w_in, cols) * jnp.asarray(scale)
    wt = _select_columns(w_in, _in_rows_t())
    return w.astype(BF16), wt.T.astype(BF16)


def kernel(x, c, rel_bias, w_ada, b_ada, w_in, w_branch, w_out, cmp_pe, cmp_w1, cmp_w2, swa_sinks, ln_g, ln_b,
           ffn_w_gate, ffn_w_up, ffn_w_down, moe_router, moe_w_gate, moe_w_up, moe_w_down):
    bsz, seq, d = x.shape
    depth = w_ada.shape[0]
    alpha = (2 * depth) ** 0.25
    m = bsz * seq
    nq = seq // BLK

    nsa_heads = tuple(range(NSA_HEADS))
    bias_sel = _bias_tiles(rel_bias, _bucket_tiles(nq, seq, 1), nsa_heads)
    bias_win = _bias_tiles(rel_bias, _bucket_tiles(-(-(NSA_WINDOW - 1) // BLK) + 1, NSA_WINDOW - 1, 1), nsa_heads)
    bias_swa = _bias_tiles(rel_bias, _bucket_tiles(-(-(SWA_WINDOW - 1) // BLK) + 1, SWA_WINDOW - 1, 1),
                           tuple(NSA_HEADS + h for h in SWA_SLOT_HEADS))
    bias_dil = []
    for gi, (win, dil) in enumerate(DIL_PATTERNS):
        h0 = NSA_HEADS + SWA_HEADS + gi * DIL_HEADS_PER_GROUP
        bias_dil.append(_bias_tiles(rel_bias, _bucket_tiles(min(win // BLK + 1, nq), win, dil),
                                    tuple(range(h0, h0 + DIL_HEADS_PER_GROUP))))
    swa_rows = np.concatenate([np.arange(h * HEAD_DIM, (h + 1) * HEAD_DIM) for h in SWA_SLOT_HEADS])

    x2 = x.reshape(m, d)
    for layer in range(depth):
        mod4 = _ada(c, w_ada[layer], b_ada[layer]).reshape(bsz, 6, 1, d)
        w_nat, w_t = _relayout_w_in(w_in[layer])
        z2, zt = _inproj(x2, mod4, w_nat, w_t, seq, sc_idx=1, sh_idx=0)
        zt = zt.reshape(bsz, nq, N_ROWS_T, BLK)
        z = z2.reshape(bsz, seq, N_COLS)
        chunks = z[:, :, COL_NSA_CMP:COL_NSA_CMP + 2 * HEAD_DIM].reshape(bsz, seq // CMP_STRIDE,
                                                                        CMP_STRIDE * 2 * HEAD_DIM)
        kc, vct = _compress(chunks, cmp_pe[layer], cmp_w1[layer], cmp_w2[layer])
        o_a = _nsa(z, zt, kc, vct, bias_sel, bias_win)
        o_b = _swa(z, zt, bias_swa, swa_sinks[layer])
        o_c = _dilated(z, zt, bias_dil)
        o_d = _stick_breaking(z, zt)
        wb = w_branch[layer]
        wb = jnp.stack([wb[0], wb[1][swa_rows], wb[2], wb[3]]).astype(BF16)
        branches = [a.reshape(m, BRANCH_WIDTH) for a in (o_a, o_b, o_c, o_d)]
        x2 = _merge(branches, z2, wb, w_out[layer].astype(BF16), x2, mod4, ln_g[layer, 0], ln_b[layer, 0], seq,
                    alpha=alpha, g_idx=2)
        i = layer // 2
        if layer % 2 == 0:
            x2 = _ffn(x2, mod4, ffn_w_gate[i].astype(BF16), ffn_w_up[i].astype(BF16), ffn_w_down[i].astype(BF16),
                      ln_g[layer, 1], ln_b[layer, 1], seq, alpha=alpha)
        else:
            x2 = _moe(x2, mod4, moe_router[i], moe_w_gate[i].astype(BF16), moe_w_up[i].astype(BF16),
                      moe_w_down[i].astype(BF16), ln_g[layer, 1], ln_b[layer, 1], seq, alpha=alpha)
    return x2.reshape(bsz, seq, d)
```

```python
import functools
import math

import numpy as np
import jax
import jax.numpy as jnp
from jax import lax
from jax.experimental import pallas as pl
from jax.experimental.pallas import tpu as pltpu

F32 = jnp.float32
BF16 = jnp.bfloat16
HIGHEST = lax.Precision.HIGHEST

HEAD_DIM = 64
BLK = 128
N_SLOTS = 4
NSA_HEADS = 4
CMP_LEN = 32
CMP_STRIDE = 16
SEL_LEN = 64
N_SEL = 16
NSA_WINDOW = 512
SWA_HEADS = 4
SWA_WINDOW = 128
SWA_SLOT_HEADS = (0, 2, 1, 3)
DIL_PATTERNS = ((128, 1), (512, 4), (2048, 16))
DIL_HEADS_PER_GROUP = 4
SB_HEADS = 4
N_BRANCHES = 4
BRANCH_WIDTH = 4 * HEAD_DIM
NUM_BUCKETS = 32
MAX_DISTANCE = 2048
N_EXPERTS = 8
LN_EPS = 1e-5
NEG_INF = -1e30
TINY = 1e-30
FORCE_SCORE = 1e4
LOG2E = math.log2(math.e)

IN_NSA_Q, IN_K_CMP, IN_V_CMP, IN_K_SEL, IN_V_SEL, IN_K_WIN, IN_V_WIN, IN_GATE = 0, 256, 320, 384, 448, 512, 576, 640
IN_SWA_Q, IN_SWA_K, IN_SWA_V = 652, 908, 1036
IN_DIL_Q, IN_DIL_K, IN_DIL_V = 1164, 1932, 2188
IN_SB_Q, IN_SB_K, IN_SB_V = 2444, 2700, 2956
IN_MG = 3212
GATE_COLS = NSA_HEADS * 3

COL_MG = 0
COL_NSA_Q = 4096
COL_NSA_CMP = 4352
COL_NSA_SEL = 4480
COL_NSA_WIN = 4608
COL_NSA_GATE = 4736
COL_SWA_Q = 4864
COL_SWA_K = 5120
COL_DIL_Q = 5376
COL_DIL_K = 6144
COL_SB_Q = 6400
COL_SB_K = 6656
N_COLS = 6912
ROW_NSA_V = 0
ROW_SWA_V = 128
ROW_DIL_V = 256
ROW_SB_V = 512
ROW_GATE = 768
N_ROWS_T = 784

VMEM_LIMIT = 56 * 1024 * 1024
ATTN_BATCH = 4


def _params(sem):
    return pltpu.CompilerParams(dimension_semantics=sem, vmem_limit_bytes=VMEM_LIMIT)


def _ln(x):
    mu = jnp.mean(x, axis=-1, keepdims=True)
    xc = x - mu
    var = jnp.mean(xc * xc, axis=-1, keepdims=True)
    return xc * lax.rsqrt(var + LN_EPS)


def _dot_nt(a, b, precision=None):
    return lax.dot_general(a, b, (((1,), (1,)), ((), ())), preferred_element_type=F32, precision=precision)


def _dot(a, b, precision=None):
    return jnp.dot(a, b, preferred_element_type=F32, precision=precision)


def _full(shape):
    return pl.BlockSpec(shape, lambda *_: (0,) * len(shape))


def _ada_kernel(c_ref, w_ref, b_ref, o_ref):
    c = c_ref[...]
    s = c * jax.nn.sigmoid(c)
    o_ref[...] = _dot(s, w_ref[...], HIGHEST) + b_ref[...]


def _ada(c, w, b):
    bsz, d = c.shape
    n = w.shape[1]
    tn = 1024
    return pl.pallas_call(
        _ada_kernel,
        grid=(n // tn,),
        in_specs=[pl.BlockSpec((bsz, d), lambda j: (0, 0)),
                  pl.BlockSpec((d, tn), lambda j: (0, j)),
                  pl.BlockSpec((1, tn), lambda j: (0, j))],
        out_specs=pl.BlockSpec((bsz, tn), lambda j: (0, j)),
        out_shape=jax.ShapeDtypeStruct((bsz, n), F32),
        compiler_params=_params(("parallel",)),
        name="ada",
    )(c, w, b.reshape(1, n))


def _inproj_kernel(x_ref, sc_ref, sh_ref, w_ref, wt_ref, o_ref, ot_ref, *, tn):
    h = (_ln(x_ref[...]) * (1.0 + sc_ref[...]) + sh_ref[...]).astype(BF16)
    for n0 in range(0, w_ref.shape[1], tn):
        o_ref[:, n0:n0 + tn] = _dot(h, w_ref[:, n0:n0 + tn]).astype(o_ref.dtype)
    res = _dot_nt(wt_ref[...], h)
    for j in range(ot_ref.shape[0]):
        ot_ref[j] = res[:, j * BLK:(j + 1) * BLK].astype(ot_ref.dtype)


def _inproj(x2, mod4, w, wt, seq, *, sc_idx, sh_idx, tm=512, n_chunks=3):
    m, d = x2.shape
    n = w.shape[1]
    rows = wt.shape[0]
    per = seq // tm
    resident = lambda shape: pl.BlockSpec(shape, lambda i: (0,) * len(shape), pipeline_mode=pl.Buffered(1))
    return pl.pallas_call(
        functools.partial(_inproj_kernel, tn=n // n_chunks),
        grid=(m // tm,),
        in_specs=[pl.BlockSpec((tm, d), lambda i: (i, 0)),
                  pl.BlockSpec((None, None, 1, d), lambda i: (i // per, sc_idx, 0, 0)),
                  pl.BlockSpec((None, None, 1, d), lambda i: (i // per, sh_idx, 0, 0)),
                  resident(w.shape), resident(wt.shape)],
        out_specs=(pl.BlockSpec((tm, n), lambda i: (i, 0)),
                   pl.BlockSpec((tm // BLK, rows, BLK), lambda i: (i, 0, 0))),
        out_shape=(jax.ShapeDtypeStruct((m, n), BF16),
                   jax.ShapeDtypeStruct((m // BLK, rows, BLK), BF16)),
        compiler_params=_params(("parallel",)),
        name="inproj",
    )(x2, mod4, mod4, w, wt)


def _np_bucket(dist):
    dist = np.maximum(dist, 0)
    max_exact = NUM_BUCKETS // 2
    d_f = np.maximum(dist, 1).astype(np.float32)
    large = max_exact + (np.log(d_f / np.float32(max_exact)) / np.float32(math.log(MAX_DISTANCE / max_exact))
                         * np.float32(NUM_BUCKETS - max_exact)).astype(np.int32)
    large = np.minimum(large, NUM_BUCKETS - 1)
    return np.where(dist < max_exact, dist, large).astype(np.int32)


def _bucket_tiles(n_off, max_dist, dil):
    b = np.arange(BLK)[:, None]
    a = np.arange(BLK)[None, :]
    tiles = []
    for o in range(n_off):
        dist = o * BLK + a - b
        ok = (dist >= 0) & (dist <= max_dist) & (dist % dil == 0)
        tiles.append(np.where(ok, _np_bucket(dist), -1))
    return np.stack(tiles).astype(np.int32)


def _bias_kernel(tbl_ref, bk_ref, o_ref, *, heads):
    slot = pl.program_id(0)
    h = jnp.int32(heads[0])
    for j in range(1, len(heads)):
        h = jnp.where(slot == j, heads[j], h)
    b = bk_ref[...]
    out = jnp.where(b < 0, NEG_INF, 0.0).astype(F32)
    for k in range(NUM_BUCKETS):
        out = jnp.where(b == k, tbl_ref[k, h] * LOG2E, out)
    o_ref[...] = out


def _bias_tiles(rel_bias, buckets, heads):
    n_off = buckets.shape[0]
    return pl.pallas_call(
        functools.partial(_bias_kernel, heads=tuple(heads)),
        grid=(len(heads),),
        in_specs=[pl.BlockSpec(memory_space=pltpu.SMEM), _full((n_off, BLK, BLK))],
        out_specs=pl.BlockSpec((n_off, BLK, BLK), lambda s: (0, 0, s)),
        out_shape=jax.ShapeDtypeStruct((n_off, BLK, len(heads) * BLK), F32),
        compiler_params=_params(("parallel",)),
        name="bias_tiles",
    )(rel_bias, jnp.asarray(buckets))


def _slot_queries(q, kw):
    lane = lax.broadcasted_iota(jnp.int32, (BLK, kw), 1)
    zero = jnp.zeros((BLK, kw), q.dtype)
    parts = []
    for j in range(N_SLOTS):
        lo = j * HEAD_DIM
        grp = q[:, (lo // kw) * kw:(lo // kw + 1) * kw]
        inside = (lane >= lo % kw) & (lane < lo % kw + HEAD_DIM)
        parts.append(jnp.where(inside, grp, zero))
    return jnp.concatenate(parts, axis=0)


def _softmax_step(s, vt, state, slot_rows):
    m, l = state[0], state[1]
    m_new = jnp.maximum(m, jnp.max(s, axis=0, keepdims=True))
    alpha = jnp.exp2(m - m_new)
    p = jnp.exp2(s - m_new)
    l = alpha * l + jnp.sum(p, axis=0, keepdims=True)
    yield
    res = _slot_values(vt, p.astype(BF16), slot_rows)
    yield
    accs = [alpha[:, j * BLK:(j + 1) * BLK] * state[2 + j] + res[j] for j in range(N_SLOTS)]
    return (m_new, l, *accs)


def _softmax_init():
    return (jnp.full((1, N_SLOTS * BLK), NEG_INF, F32), jnp.zeros((1, N_SLOTS * BLK), F32),
            *[jnp.zeros((HEAD_DIM, BLK), F32) for _ in range(N_SLOTS)])


def _store_heads(o_ref, outs):
    o_ref[...] = jnp.transpose(jnp.concatenate(outs, axis=0)).astype(o_ref.dtype)


def _key_rows(k_ref, kb):
    return k_ref[pl.ds(pl.multiple_of(kb * BLK, BLK), BLK), :]


def _slot_values(vt, p, slot_rows):
    if vt.shape[0] > 2 * HEAD_DIM:
        return [_dot(vt[slot_rows[j]:slot_rows[j] + HEAD_DIM, :], p[:, j * BLK:(j + 1) * BLK])
                for j in range(N_SLOTS)]
    res = _dot(vt, p)
    return [res[slot_rows[j]:slot_rows[j] + HEAD_DIM, j * BLK:(j + 1) * BLK] for j in range(N_SLOTS)]


def _window_softmax(qi, wq, k_ref, vt_of, bias_ref, n_off, slot_rows):
    tiles = []
    m = None
    for o in range(n_off):
        kb = qi - o
        kbc = jnp.maximum(kb, 0)
        s = _dot_nt(_key_rows(k_ref, kbc), wq) + bias_ref[o]
        if o > 0:
            s = s + jnp.where(kb >= 0, 0.0, NEG_INF)
        tiles.append((kbc, s))
        mo = jnp.max(s, axis=0, keepdims=True)
        m = mo if m is None else jnp.maximum(m, mo)
    l = None
    res = None
    for kbc, s in tiles:
        p = jnp.exp2(s - m)
        lo = jnp.sum(p, axis=0, keepdims=True)
        ro = _slot_values(vt_of(kbc), p.astype(BF16), slot_rows)
        l = lo if l is None else l + lo
        res = ro if res is None else [a + b for a, b in zip(res, ro)]
    return (m, l, *res)


def _run_interleaved(gens):
    results = [None] * len(gens)
    live = list(range(len(gens)))
    while live:
        for i in tuple(live):
            try:
                next(gens[i])
            except StopIteration as done:
                results[i] = done.value
                live.remove(i)
    return results


def _chained_loop(lo, hi, steps, states):
    def body(kb, sts):
        return tuple(_run_interleaved([f(kb, st) for f, st in zip(steps, sts)]))
    return lax.fori_loop(lo, hi, body, tuple(states))


def _compress_kernel(ch_ref, wa_ref, wb_ref, pea_ref, peb_ref, w2k_ref, w2vt_ref, kc_ref, vct_ref):
    ch = ch_ref[...].astype(F32)
    wa = wa_ref[...]
    wb = wb_ref[...]
    first = _dot(ch, wa, HIGHEST)
    second = _dot(ch, wb, HIGHEST)
    pe = (_dot(pea_ref[...], wa, HIGHEST) + _dot(peb_ref[...], wb, HIGHEST))[0:1]
    n = second.shape[0]
    nxt = pltpu.roll(second, n - 1, 0)
    hid = jax.nn.gelu(first + nxt + pe)
    kc_ref[...] = _dot(hid, w2k_ref[...], HIGHEST).astype(kc_ref.dtype)
    vct_ref[...] = _dot_nt(w2vt_ref[...], hid, HIGHEST).astype(vct_ref.dtype)


def _compress(chunks, cmp_pe, cmp_w1, cmp_w2):
    bsz, n_chunk, width = chunks.shape
    per = CMP_LEN // 2
    hd = HEAD_DIM
    w1 = cmp_w1.reshape(2, CMP_LEN, hd, hd)
    wexp = jnp.einsum('cpdn,ce->pcden', w1, jnp.eye(2, dtype=F32)).reshape(CMP_LEN, 2 * hd, 2 * hd)
    wa = wexp[:per].reshape(width, 2 * hd)
    wb = wexp[per:].reshape(width, 2 * hd)
    pe = jnp.transpose(cmp_pe, (1, 0, 2)).reshape(CMP_LEN, 2 * hd)
    pea = jnp.broadcast_to(pe[:per].reshape(1, width), (8, width))
    peb = jnp.broadcast_to(pe[per:].reshape(1, width), (8, width))
    zero = jnp.zeros((hd, hd), F32)
    w2k = jnp.concatenate([jnp.concatenate([cmp_w2[0], cmp_w2[0]], 1), jnp.concatenate([zero, zero], 1)], 0)
    w2vt = jnp.concatenate([zero, cmp_w2[1].T], 1)
    return pl.pallas_call(
        _compress_kernel,
        grid=(bsz,),
        in_specs=[pl.BlockSpec((None, n_chunk, width), lambda b: (b, 0, 0)),
                  _full(wa.shape), _full(wb.shape), _full(pea.shape), _full(peb.shape), _full(w2k.shape),
                  _full(w2vt.shape)],
        out_specs=(pl.BlockSpec((None, n_chunk, 2 * hd), lambda b: (b, 0, 0)),
                   pl.BlockSpec((None, hd, n_chunk), lambda b: (b, 0, 0))),
        out_shape=(jax.ShapeDtypeStruct((bsz, n_chunk, 2 * hd), BF16),
                   jax.ShapeDtypeStruct((bsz, hd, n_chunk), BF16)),
        compiler_params=_params(("parallel",)),
        name="nsa_compress",
    )(chunks, wa, wb, pea, peb, w2k, w2vt)


def _nsa_kernel(q_ref, kc_ref, vct_ref, ovt_ref, ksel_ref, kwin_ref, vt_ref, gate_ref, bsel_ref, bwin_ref, et_ref,
                o_ref, *, n_blk, n_win_off):
    qi = pl.program_id(1)
    t0 = qi * BLK
    nb = q_ref.shape[0]
    n_cmp = kc_ref.shape[1]
    slot_rows = (0,) * N_SLOTS
    shape = (n_cmp, N_SLOTS * BLK)
    tq = t0 + lax.broadcasted_iota(jnp.int32, shape, 1) % BLK
    ci = lax.broadcasted_iota(jnp.int32, shape, 0)
    ok = (ci * CMP_STRIDE + (CMP_LEN - 1)) <= tq
    jb = lax.broadcasted_iota(jnp.int32, (n_blk, BLK), 0)
    cur = (t0 + lax.broadcasted_iota(jnp.int32, (n_blk, BLK), 1)) // SEL_LEN
    forced = (jb == 0) | (jb == cur) | (jb == cur - 1)

    o_cmps, o_wins, sel_steps = [], [], []
    for bb in range(nb):
        wq = _slot_queries(q_ref[bb], 2 * HEAD_DIM)

        s = jnp.where(ok, _dot_nt(kc_ref[bb], wq), NEG_INF)
        m = jnp.max(s, axis=0, keepdims=True)
        p = jnp.where(ok, jnp.exp2(s - m), 0.0)
        p = p / jnp.maximum(jnp.sum(p, axis=0, keepdims=True), TINY)
        o_cmps.append(_dot(vct_ref[bb], p.astype(BF16)))
        psum = p[:, 0:BLK]
        for j in range(1, N_SLOTS):
            psum = psum + p[:, j * BLK:(j + 1) * BLK]

        hi = psum.astype(BF16)
        lo = (psum - hi.astype(F32)).astype(BF16)
        score = _dot(ovt_ref[...], hi) + _dot(ovt_ref[...], lo)
        score = jnp.where(forced, FORCE_SCORE, jnp.where(jb > cur, NEG_INF, score))
        rank = jnp.zeros((n_blk, BLK), F32)
        for c in range(n_blk):
            row = score[c:c + 1, :]
            beats = (row > score) | ((row == score) & (jb > c))
            rank = rank + jnp.where(beats, 1.0, 0.0)
        sel = jnp.where(rank < N_SEL, 1.0, 0.0).astype(BF16)

        st = _window_softmax(qi, wq, kwin_ref.at[bb], lambda kb, bb=bb: vt_ref[bb, kb, HEAD_DIM:2 * HEAD_DIM, :],
                             bwin_ref, n_win_off, slot_rows)
        o_wins.append([st[2 + j] / st[1][:, j * BLK:(j + 1) * BLK] for j in range(N_SLOTS)])

        def sel_step(kb, state, bb=bb, wq=wq, sel=sel):
            pen = (_dot(et_ref[kb], sel) - 1.0) * (-NEG_INF)
            pen = jnp.concatenate([pen] * N_SLOTS, axis=1)
            sc = _dot_nt(_key_rows(ksel_ref.at[bb], kb), wq) + bsel_ref[qi - kb] + pen
            yield
            return (yield from _softmax_step(sc, vt_ref[bb, kb, 0:HEAD_DIM, :], state, slot_rows))

        sel_steps.append(sel_step)

    sel_states = _run_interleaved([f(qi, _softmax_init()) for f in sel_steps])
    sel_states = _chained_loop(0, qi, sel_steps, sel_states)
    for bb in range(nb):
        st = sel_states[bb]
        gate = jax.nn.sigmoid(gate_ref[bb].astype(F32))
        outs = []
        for j in range(N_SLOTS):
            o_sel = st[2 + j] / st[1][:, j * BLK:(j + 1) * BLK]
            outs.append(gate[3 * j:3 * j + 1] * o_cmps[bb][:, j * BLK:(j + 1) * BLK]
                        + gate[3 * j + 1:3 * j + 2] * o_sel + gate[3 * j + 2:3 * j + 3] * o_wins[bb][j])
        _store_heads(o_ref.at[bb], outs)


def _nsa(z, zt, kc, vct, bias_sel, bias_win, nb=ATTN_BATCH):
    bsz, seq, _ = z.shape
    nq = seq // BLK
    n_cmp = kc.shape[1]
    n_blk = seq // SEL_LEN
    assert n_blk >= N_SEL
    ci = np.arange(n_cmp)[None, :]
    sj = np.arange(n_blk)[:, None]
    overlap_t = ((ci * CMP_STRIDE + CMP_LEN - 1 >= sj * SEL_LEN) & (ci * CMP_STRIDE < (sj + 1) * SEL_LEN)
                 & (ci < n_cmp - 1)).astype(np.float32)
    key_blk = (np.arange(seq) // SEL_LEN).reshape(nq, BLK, 1)
    expand_t = (key_blk == np.arange(n_blk)[None, None, :]).astype(np.float32)
    qw = NSA_HEADS * HEAD_DIM
    kw = 2 * HEAD_DIM
    return pl.pallas_call(
        functools.partial(_nsa_kernel, n_blk=n_blk, n_win_off=bias_win.shape[0]),
        grid=(bsz // nb, nq),
        in_specs=[pl.BlockSpec((nb, BLK, qw), lambda b, i: (b, i, COL_NSA_Q // qw)),
                  pl.BlockSpec((nb, n_cmp, kw), lambda b, i: (b, 0, 0)),
                  pl.BlockSpec((nb, HEAD_DIM, n_cmp), lambda b, i: (b, 0, 0)),
                  _full(overlap_t.shape),
                  pl.BlockSpec((nb, seq, kw), lambda b, i: (b, 0, COL_NSA_SEL // kw)),
                  pl.BlockSpec((nb, seq, kw), lambda b, i: (b, 0, COL_NSA_WIN // kw)),
                  pl.BlockSpec((nb, nq, kw, BLK), lambda b, i: (b, 0, ROW_NSA_V // kw, 0)),
                  pl.BlockSpec((nb, None, 16, BLK), lambda b, i: (b, i, ROW_GATE // 16, 0)),
                  _full(bias_sel.shape), _full(bias_win.shape), _full(expand_t.shape)],
        out_specs=pl.BlockSpec((nb, BLK, qw), lambda b, i: (b, i, 0)),
        out_shape=jax.ShapeDtypeStruct((bsz, seq, qw), BF16),
        compiler_params=_params(("parallel", "parallel")),
        name="nsa_mixer",
    )(z, kc, vct, jnp.asarray(overlap_t, dtype=BF16), z, z, zt, zt, bias_sel, bias_win,
      jnp.asarray(expand_t, dtype=BF16))


def _swa_kernel(q_ref, k_ref, vt_ref, bias_ref, sink_ref, o_ref, *, n_off):
    qi = pl.program_id(1)
    slot_rows = tuple((j % 2) * HEAD_DIM for j in range(N_SLOTS))
    for bb in range(q_ref.shape[0]):
        wq = _slot_queries(q_ref[bb], 2 * HEAD_DIM)
        st = _window_softmax(qi, wq, k_ref.at[bb], lambda kb, bb=bb: vt_ref[bb, kb], bias_ref, n_off, slot_rows)
        m, l = st[0], st[1]
        outs = []
        for j in range(N_SLOTS):
            sink = sink_ref[SWA_SLOT_HEADS[j]] * LOG2E
            mj = m[:, j * BLK:(j + 1) * BLK]
            lj = l[:, j * BLK:(j + 1) * BLK]
            m2 = jnp.maximum(mj, sink)
            scale = jnp.exp2(mj - m2)
            den = lj * scale + jnp.exp2(sink - m2)
            outs.append(st[2 + j] * (scale / den))
        _store_heads(o_ref.at[bb], outs)


def _swa(z, zt, bias, sinks, nb=ATTN_BATCH):
    bsz, seq, _ = z.shape
    nq = seq // BLK
    qw = SWA_HEADS * HEAD_DIM
    kw = 2 * HEAD_DIM
    return pl.pallas_call(
        functools.partial(_swa_kernel, n_off=bias.shape[0]),
        grid=(bsz // nb, nq),
        in_specs=[pl.BlockSpec((nb, BLK, qw), lambda b, i: (b, i, COL_SWA_Q // qw)),
                  pl.BlockSpec((nb, seq, kw), lambda b, i: (b, 0, COL_SWA_K // kw)),
                  pl.BlockSpec((nb, nq, kw, BLK), lambda b, i: (b, 0, ROW_SWA_V // kw, 0)),
                  _full(bias.shape),
                  pl.BlockSpec(memory_space=pltpu.SMEM)],
        out_specs=pl.BlockSpec((nb, BLK, qw), lambda b, i: (b, i, 0)),
        out_shape=jax.ShapeDtypeStruct((bsz, seq, qw), BF16),
        compiler_params=_params(("parallel", "parallel")),
        name="swa_mixer",
    )(z, z, zt, bias, sinks)


def _dil_kernel(q0_ref, q1_ref, q2_ref, k_ref, vt_ref, b0_ref, b1_ref, b2_ref, o_ref):
    qi = pl.program_id(1)
    nb = q0_ref.shape[0]
    slot_rows = tuple(j * HEAD_DIM for j in range(N_SLOTS))
    kw = N_SLOTS * HEAD_DIM
    short, far_steps = [], []
    for bb in range(nb):
        vt_of = lambda kb, bb=bb: vt_ref[bb, kb]
        short.append([_window_softmax(qi, _slot_queries(q_ref[bb], kw), k_ref.at[bb], vt_of, bias_ref,
                                      bias_ref.shape[0], slot_rows)
                      for q_ref, bias_ref in ((q0_ref, b0_ref), (q1_ref, b1_ref))])
        wq2 = _slot_queries(q2_ref[bb], kw)

        def far_step(kb, state, bb=bb, wq2=wq2, vt_of=vt_of):
            sc = _dot_nt(_key_rows(k_ref.at[bb], kb), wq2) + b2_ref[qi - kb]
            yield
            return (yield from _softmax_step(sc, vt_of(kb), state, slot_rows))

        far_steps.append(far_step)
    far_states = _run_interleaved([f(qi, _softmax_init()) for f in far_steps])
    far_states = _chained_loop(jnp.maximum(qi - (b2_ref.shape[0] - 1), 0), qi, far_steps, far_states)
    for bb in range(nb):
        groups = short[bb] + [far_states[bb]]
        lses = [st[0] + jnp.log(st[1]) * LOG2E for st in groups]
        top = jnp.maximum(jnp.maximum(lses[0], lses[1]), lses[2])
        es = [jnp.exp2(v - top) for v in lses]
        den = es[0] + es[1] + es[2]
        outs = []
        for j in range(N_SLOTS):
            sl = slice(j * BLK, (j + 1) * BLK)
            acc = jnp.zeros((HEAD_DIM, BLK), F32)
            for st, e in zip(groups, es):
                acc = acc + (e[:, sl] / den[:, sl]) * (st[2 + j] / st[1][:, sl])
            outs.append(acc)
        _store_heads(o_ref.at[bb], outs)


def _dilated(z, zt, biases, nb=ATTN_BATCH):
    bsz, seq, _ = z.shape
    nq = seq // BLK
    qw = DIL_HEADS_PER_GROUP * HEAD_DIM
    qspec = lambda g: pl.BlockSpec((nb, BLK, qw), lambda b, i: (b, i, COL_DIL_Q // qw + g))
    return pl.pallas_call(
        _dil_kernel,
        grid=(bsz // nb, nq),
        in_specs=[qspec(0), qspec(1), qspec(2),
                  pl.BlockSpec((nb, seq, qw), lambda b, i: (b, 0, COL_DIL_K // qw)),
                  pl.BlockSpec((nb, nq, qw, BLK), lambda b, i: (b, 0, ROW_DIL_V // qw, 0)),
                  _full(biases[0].shape), _full(biases[1].shape), _full(biases[2].shape)],
        out_specs=pl.BlockSpec((nb, BLK, qw), lambda b, i: (b, i, 0)),
        out_shape=jax.ShapeDtypeStruct((bsz, seq, qw), BF16),
        compiler_params=_params(("parallel", "parallel")),
        name="dilated_mixer",
    )(z, z, z, z, zt, *biases)


def _sb_kernel(q_ref, k_ref, vt_ref, u_ref, o_ref):
    qi = pl.program_id(1)
    nb = q_ref.shape[0]
    u = u_ref[...]
    slot_rows = tuple(j * HEAD_DIM for j in range(N_SLOTS))
    shape = (BLK, N_SLOTS * BLK)
    strict =lax.broadcasted_iota(jnp.int32, shape, 0) < lax.broadcasted_iota(jnp.int32, shape, 1) % BLK

    def step(bb, wq, kb, state, diag):
        run = state[0]
        zz = _dot_nt(_key_rows(k_ref.at[bb], kb), wq)
        yield
        ls = jnp.minimum(zz, 0.0) - jnp.log(1.0 + jnp.exp2(-jnp.abs(zz))) * LOG2E
        lf = ls - zz
        if diag:
            lf = jnp.where(strict, lf, 0.0)
        hi = lf.astype(BF16)
        lo = (lf - hi.astype(F32)).astype(BF16)
        after = _dot(u, jnp.concatenate([hi, lo], axis=0)) + run
        yield
        a = jnp.exp2(ls + after)
        if diag:
            a = jnp.where(strict, a, 0.0)
        res = _slot_values(vt_ref[bb, kb], a.astype(BF16), slot_rows)
        yield
        accs = [state[1 + j] + res[j] for j in range(N_SLOTS)]
        return (run + jnp.sum(lf, axis=0, keepdims=True), *accs)

    init = (jnp.zeros((1, N_SLOTS * BLK), F32), *[jnp.zeros((HEAD_DIM, BLK), F32) for _ in range(N_SLOTS)])
    wqs = [_slot_queries(q_ref[bb], SB_HEADS * HEAD_DIM) for bb in range(nb)]
    steps = [lambda t, st, bb=bb: step(bb, wqs[bb], qi - 1 - t, st, False) for bb in range(nb)]
    states = _run_interleaved([step(bb, wqs[bb], qi, init, True) for bb in range(nb)])
    states = _chained_loop(0, qi, steps, states)
    for bb in range(nb):
        _store_heads(o_ref.at[bb], list(states[bb][1:]))


def _stick_breaking(z, zt, nb=ATTN_BATCH):
    bsz, seq, _ = z.shape
    nq = seq // BLK
    w = SB_HEADS * HEAD_DIM
    later = (np.arange(BLK)[None, :] > np.arange(BLK)[:, None]).astype(np.float32)
    later = np.concatenate([later, later], axis=1)
    return pl.pallas_call(
        _sb_kernel,
        grid=(bsz // nb, nq),
        in_specs=[pl.BlockSpec((nb, BLK, w), lambda b, i: (b, i, COL_SB_Q // w)),
                  pl.BlockSpec((nb, seq, w), lambda b, i: (b, 0, COL_SB_K // w)),
                  pl.BlockSpec((nb, nq, w, BLK), lambda b, i: (b, 0, ROW_SB_V // w, 0)),
                  _full((BLK, 2 * BLK))],
        out_specs=pl.BlockSpec((nb, BLK, w), lambda b, i: (b, i, 0)),
        out_shape=jax.ShapeDtypeStruct((bsz, seq, w), BF16),
        compiler_params=_params(("parallel", "parallel")),
        name="stick_breaking",
    )(z, z, zt, jnp.asarray(later, dtype=BF16))


def _merge_kernel(oa_ref, ob_ref, oc_ref, od_ref, mg_ref, wb_ref, wo_ref, x_ref, g_ref, lng_ref, lnb_ref, o_ref,
                  *, alpha):
    d = x_ref.shape[1]
    merged = jnp.zeros((x_ref.shape[0], d), F32)
    for b, ref in enumerate((oa_ref, ob_ref, oc_ref, od_ref)):
        proj = _dot(ref[...], wb_ref[b])
        merged = merged + jax.nn.sigmoid(mg_ref[:, b * d:(b + 1) * d].astype(F32)) * proj
    y = _dot(merged.astype(BF16), wo_ref[...])
    v = alpha * x_ref[...] + g_ref[...] * y
    o_ref[...] = _ln(v) * lng_ref[...] + lnb_ref[...]


def _merge(branches, z2, wb, wo, x2, mod4, lng, lnb, seq, *, alpha, g_idx, tm=512):
    m, d = x2.shape
    bw = BRANCH_WIDTH
    per = seq // tm
    row = lambda w: pl.BlockSpec((tm, w), lambda i: (i, 0))
    return pl.pallas_call(
        functools.partial(_merge_kernel, alpha=alpha),
        grid=(m // tm,),
        in_specs=[row(bw), row(bw), row(bw), row(bw),
                  pl.BlockSpec((tm, N_BRANCHES * d), lambda i: (i, COL_MG // (N_BRANCHES * d))),
                  _full(wb.shape), _full(wo.shape),
                  row(d),
                  pl.BlockSpec((None, None, 1, d), lambda i: (i // per, g_idx, 0, 0)),
                  _full((1, d)), _full((1, d))],
        out_specs=row(d),
        out_shape=jax.ShapeDtypeStruct((m, d), F32),
        compiler_params=_params(("parallel",)),
        name="merge_out",
    )(*branches, z2, wb, wo, x2, mod4, lng.reshape(1, d), lnb.reshape(1, d))


def _ffn_kernel(x_ref, sc_ref, sh_ref, g_ref, lng_ref, lnb_ref, wg_ref, wu_ref, wd_ref, o_ref, *, alpha, tf):
    x = x_ref[...]
    h = (_ln(x) * (1.0 + sc_ref[...]) + sh_ref[...]).astype(BF16)
    y = None
    for f0 in range(0, wg_ref.shape[1], tf):
        a = _dot(h, wg_ref[:, f0:f0 + tf])
        u = _dot(h, wu_ref[:, f0:f0 + tf])
        act = (a * jax.nn.sigmoid(a)) * u
        t = _dot(act.astype(BF16), wd_ref[f0:f0 + tf, :])
        y = t if y is None else y + t
    v = alpha * x + g_ref[...] * y
    o_ref[...] = _ln(v) * lng_ref[...] + lnb_ref[...]


def _ffn(x2, mod4, wg, wu, wd, lng, lnb, seq, *, alpha, tm=512, tf=1408):
    m, d = x2.shape
    per = seq // tm
    modspec = lambda k: pl.BlockSpec((None, None, 1, d), lambda i: (i // per, k, 0, 0))
    resident = lambda shape: pl.BlockSpec(shape, lambda i: (0,) * len(shape), pipeline_mode=pl.Buffered(1))
    return pl.pallas_call(
        functools.partial(_ffn_kernel, alpha=alpha, tf=tf),
        grid=(m // tm,),
        in_specs=[pl.BlockSpec((tm, d), lambda i: (i, 0)),
                  modspec(4), modspec(3), modspec(5),
                  _full((1, d)), _full((1, d)),
                  resident(wg.shape), resident(wu.shape), resident(wd.shape)],
        out_specs=pl.BlockSpec((tm, d), lambda i: (i, 0)),
        out_shape=jax.ShapeDtypeStruct((m, d), F32),
        compiler_params=_params(("parallel",)),
        name="ffn_dense",
    )(x2, mod4, mod4, mod4, lng.reshape(1, d), lnb.reshape(1, d), wg, wu, wd)


MOE_WIN = 512
MOE_TILE = 256
MOE_FFN_TILE = 512
INFO_E1, INFO_E2, INFO_W1, INFO_W2, INFO_R1, INFO_R2 = range(6)


def _router_kernel(x_ref, sc_ref, sh_ref, wrh_ref, wrl_ref, ls_ref, h_ref, col_ref, row_ref, cum_ref, cnt_ref,
                   carry_ref):
    @pl.when(pl.program_id(0) == 0)
    def _():
        carry_ref[...] = jnp.zeros_like(carry_ref)

    tm = x_ref.shape[0]
    lane = lax.broadcasted_iota(jnp.int32, (tm, 128), 1).astype(F32)
    h32 = _ln(x_ref[...]) * (1.0 + sc_ref[...]) + sh_ref[...]
    h_hi = h32.astype(BF16)
    h_ref[...] = h_hi
    h_lo = (h32 - h_hi.astype(F32)).astype(BF16)
    logits = _dot(h_hi, wrh_ref[...]) + (_dot(h_lo, wrh_ref[...]) + _dot(h_hi, wrl_ref[...]))
    logits = jnp.where(lane < N_EXPERTS, logits, -jnp.inf)
    t1 = jnp.max(logits, axis=1, keepdims=True)
    i1 = jnp.min(jnp.where(logits == t1, lane, 128.0), axis=1, keepdims=True)
    rest = jnp.where(lane == i1, -jnp.inf, logits)
    t2 = jnp.max(rest, axis=1, keepdims=True)
    i2 = jnp.min(jnp.where(rest == t2, lane, 128.0), axis=1, keepdims=True)
    e2 = jnp.exp(t2 - t1)
    den = 1.0 + e2
    w1 = 1.0 / den
    w2 = e2 / den
    onehot = jnp.where(lane == i1, 1.0, jnp.where(lane == i2, 1.0, 0.0))
    before = _dot(ls_ref[...], onehot.astype(BF16)) + carry_ref[0:1, :]
    r1 = jnp.sum(jnp.where(lane == i1, before, 0.0), axis=1, keepdims=True)
    r2 = jnp.sum(jnp.where(lane == i2, before, 0.0), axis=1, keepdims=True)
    info = jnp.zeros((tm, 128), F32)
    for k, v in ((INFO_E1, i1), (INFO_E2, i2), (INFO_W1, w1), (INFO_W2, w2), (INFO_R1, r1), (INFO_R2, r2)):
        info = jnp.where(lane == k, v, info)
    col_ref[...] = info
    row_ref[...] = jnp.transpose(info)[0:8, :]
    cnt = jnp.sum(onehot, axis=0, keepdims=True)
    cum_ref[...] = carry_ref[...]
    cnt_ref[...] = jnp.broadcast_to(cnt, cnt_ref.shape)
    carry_ref[...] = carry_ref[...] + cnt


def _router(x2, mod4, wr, seq):
    m, d = x2.shape
    tm = MOE_WIN
    per = seq // tm
    n_win = m // tm
    wr_pad = jnp.pad(wr, ((0, 0), (0, 128 - wr.shape[1])))
    wr_hi = wr_pad.astype(BF16)
    wr_lo = (wr_pad - wr_hi.astype(F32)).astype(BF16)
    earlier =(np.arange(tm)[:, None] > np.arange(tm)[None, :]).astype(np.float32)
    modspec = lambda k: pl.BlockSpec((None, None, 1, d), lambda i: (i // per, k, 0, 0))
    return pl.pallas_call(
        _router_kernel,
        grid=(n_win,),
        in_specs=[pl.BlockSpec((tm, d), lambda i: (i, 0)), modspec(4), modspec(3),
                  _full((d, 128)), _full((d, 128)), _full((tm, tm))],
        out_specs=(pl.BlockSpec((tm, d), lambda i: (i, 0)),
                   pl.BlockSpec((tm, 128), lambda i: (i, 0)),
                   pl.BlockSpec((None, 8, tm), lambda i: (i, 0, 0)),
                   pl.BlockSpec((None, 8, 128), lambda i: (i, 0, 0)),
                   pl.BlockSpec((None, 8, 128), lambda i: (i, 0, 0))),
        out_shape=(jax.ShapeDtypeStruct((m, d), BF16),
                   jax.ShapeDtypeStruct((m, 128), F32),
                   jax.ShapeDtypeStruct((n_win, 8, tm), F32),
                   jax.ShapeDtypeStruct((n_win, 8, 128), F32),
                   jax.ShapeDtypeStruct((n_win, 8, 128), F32)),
        scratch_shapes=[pltpu.VMEM((8, 128), F32)],
        compiler_params=_params(("arbitrary",)),
        name="moe_router",
    )(x2, mod4, mod4, wr_hi, wr_lo, jnp.asarray(earlier, dtype=BF16))


def _expert_offset(e, start_ref):
    off = jnp.zeros_like(e)
    for k in range(N_EXPERTS):
        off = jnp.where(e == k, start_ref[k].astype(F32), off)
    return off


def _pair_flags(s, n, key_ref):
    last_idx = key_ref.shape[0] - 1
    key = key_ref[s]
    first = (s == 0) | (key != key_ref[jnp.maximum(s - 1, 0)])
    last = (s == n - 1) | (key != key_ref[jnp.minimum(s + 1, last_idx)])
    return s < n, first, last


def _gather_kernel(pt_ref, pw_ref, n_ref, start_ref, h_ref, row_ref, hs_ref, acc_ref):
    s = pl.program_id(0)
    active, first, last = _pair_flags(s, n_ref[0], pt_ref)

    @pl.when(active & first)
    def _():
        acc_ref[...] = jnp.zeros_like(acc_ref)

    @pl.when(active)
    def _():
        info = row_ref[...]
        half = info.shape[1] // 2
        p1 = info[INFO_R1:INFO_R1 + 1] + _expert_offset(info[INFO_E1:INFO_E1 + 1], start_ref)
        p2 = info[INFO_R2:INFO_R2 + 1] + _expert_offset(info[INFO_E2:INFO_E2 + 1], start_ref)
        rows = (pt_ref[s] * MOE_TILE + lax.broadcasted_iota(jnp.int32, (MOE_TILE, half), 0)).astype(F32)
        part = None
        for c in range(2):
            sl = slice(c * half, (c + 1) * half)
            perm = jnp.where(rows == p1[:, sl], 1.0, jnp.where(rows == p2[:, sl], 1.0, 0.0)).astype(BF16)
            term = _dot(perm, h_ref[sl, :])
            part = term if part is None else part + term
        acc_ref[...] += part

    @pl.when(active & last)
    def _():
        hs_ref[...] = acc_ref[...].astype(hs_ref.dtype)


def _moe_ffn_kernel(te_ref, used_ref, hs_ref, wg_ref, wu_ref, wd_ref, ys_ref, *, tf):
    used = used_ref[pl.program_id(0)] > 0

    @pl.when(used)
    def _():
        h = hs_ref[...]
        y = None
        for f0 in range(0, wg_ref.shape[1], tf):
            a = _dot(h, wg_ref[:, f0:f0 + tf])
            u = _dot(h, wu_ref[:, f0:f0 + tf])
            act = (a * jax.nn.sigmoid(a)) * u
            t = _dot(act.astype(BF16), wd_ref[f0:f0 + tf, :])
            y = t if y is None else y + t
        ys_ref[...] = y.astype(ys_ref.dtype)

    @pl.when(jnp.logical_not(used))
    def _():
        ys_ref[...] = jnp.zeros_like(ys_ref)


def _combine_kernel(pw_ref, pt_ref, n_ref, start_ref, ys_ref, col_ref, x_ref, g_ref, lng_ref, lnb_ref, o_ref,
                    acc_ref, pos_ref, *, alpha):
    s = pl.program_id(0)
    active, first, last = _pair_flags(s, n_ref[0], pw_ref)

    @pl.when(active & first)
    def _():
        acc_ref[...] = jnp.zeros_like(acc_ref)
        info = col_ref[...]
        pos_ref[:, 0:1] = info[:, INFO_R1:INFO_R1 + 1] + _expert_offset(info[:, INFO_E1:INFO_E1 + 1], start_ref)
        pos_ref[:, 1:2] = info[:, INFO_R2:INFO_R2 + 1] + _expert_offset(info[:, INFO_E2:INFO_E2 + 1], start_ref)

    @pl.when(active)
    def _():
        win = col_ref.shape[0]
        p1 = pos_ref[:, 0:1]
        p2 = pos_ref[:, 1:2]
        row0 = (pt_ref[s] * MOE_TILE).astype(F32)
        in1 = (p1 >= row0) & (p1 < row0 + MOE_TILE)
        in2 = (p2 >= row0) & (p2 < row0 + MOE_TILE)
        weight = jnp.where(in1, col_ref[:, INFO_W1:INFO_W1 + 1], jnp.where(in2, col_ref[:, INFO_W2:INFO_W2 + 1], 0.0))
        cols = lax.broadcasted_iota(jnp.int32, (win, MOE_TILE), 1).astype(F32) + row0
        perm = jnp.where(cols == p1, 1.0, jnp.where(cols == p2, 1.0, 0.0)).astype(BF16)
        acc_ref[...] += weight * _dot(perm, ys_ref[...])

    @pl.when(active & last)
    def _():
        v = alpha * x_ref[...] + g_ref[...] * acc_ref[...]
        o_ref[...] = _ln(v) * lng_ref[...] + lnb_ref[...]


def _pair_list(mask, n_pairs):
    flat = mask.reshape(-1)
    n = jnp.sum(flat.astype(jnp.int32))
    idx = jnp.nonzero(flat, size=n_pairs, fill_value=0)[0].astype(jnp.int32)
    idx = jnp.where(jnp.arange(n_pairs) < n, idx, idx[jnp.maximum(n - 1, 0)])
    return idx // mask.shape[1], idx % mask.shape[1], n.reshape(1)


def _moe(x2, mod4, wr, wg, wu, wd, lng, lnb, seq, *, alpha, tf=1792):
    m, d = x2.shape
    n_exp, _, ff = wg.shape
    n_win = m // MOE_WIN
    per = seq // MOE_WIN
    n_rows = 2 * m + n_exp * MOE_FFN_TILE
    n_tiles = n_rows // MOE_TILE
    n_ffn = n_rows // MOE_FFN_TILE
    n_pairs = n_exp * n_win + n_tiles + 3 * n_exp

    h, col, row, cum, cnt = _router(x2, mod4, wr, seq)

    cum = cum[:, 0, :n_exp].astype(jnp.int32)
    cnt = cnt[:, 0, :n_exp].astype(jnp.int32)
    total = cum[-1] + cnt[-1]
    padded = (total + MOE_FFN_TILE - 1) // MOE_FFN_TILE * MOE_FFN_TILE
    end = jnp.cumsum(padded)
    start = (end - padded).astype(jnp.int32)

    def tile_expert(tile_rows):
        row0 = jnp.arange(n_rows // tile_rows, dtype=jnp.int32) * tile_rows
        te = jnp.minimum(jnp.sum((row0[:, None] >= end[None, :]).astype(jnp.int32), axis=1), n_exp - 1)
        return row0, te, row0 < end[-1]

    row0, te, used = tile_expert(MOE_TILE)
    local0 = row0 - start[te]
    lo = jnp.take(cum.T, te, axis=0)
    hi = lo + jnp.take(cnt.T, te, axis=0)
    meet = (lo < local0[:, None] + MOE_TILE) & (hi > local0[:, None]) & (hi > lo) & used[:, None]
    visit = meet.at[:, 0].set(meet[:, 0] | jnp.logical_not(jnp.any(meet, axis=1)))
    g_tile, g_win, g_n = _pair_list(visit, n_pairs)
    c_win, c_tile, c_n = _pair_list(meet.T, n_pairs)
    _, ffn_te, ffn_used = tile_expert(MOE_FFN_TILE)

    hs = pl.pallas_call(
        _gather_kernel,
        grid_spec=pltpu.PrefetchScalarGridSpec(
            num_scalar_prefetch=4,
            grid=(n_pairs,),
            in_specs=[pl.BlockSpec((MOE_WIN, d), lambda s, pt, pw, n, st: (pw[s], 0)),
                      pl.BlockSpec((None, 8, MOE_WIN), lambda s, pt, pw, n, st: (pw[s], 0, 0))],
            out_specs=pl.BlockSpec((MOE_TILE, d), lambda s, pt, pw, n, st: (pt[s], 0)),
            scratch_shapes=[pltpu.VMEM((MOE_TILE, d), F32)]),
        out_shape=jax.ShapeDtypeStruct((n_rows, d), BF16),
        compiler_params=_params(("arbitrary",)),
        name="moe_gather",
    )(g_tile, g_win, g_n, start, h, row)

    expert = lambda shape: pl.BlockSpec((None,) + shape, lambda j, te, us: (te[j], 0, 0),
                                        pipeline_mode=pl.Buffered(1))
    ys = pl.pallas_call(
        functools.partial(_moe_ffn_kernel, tf=tf),
        grid_spec=pltpu.PrefetchScalarGridSpec(
            num_scalar_prefetch=2,
            grid=(n_ffn,),
            in_specs=[pl.BlockSpec((MOE_FFN_TILE, d), lambda j, te, us: (j, 0)),
                      expert((d, ff)), expert((d, ff)), expert((ff, d))],
            out_specs=pl.BlockSpec((MOE_FFN_TILE, d), lambda j, te, us: (j, 0))),
        out_shape=jax.ShapeDtypeStruct((n_rows, d), BF16),
        compiler_params=_params(("arbitrary",)),
        name="moe_experts",
    )(ffn_te, ffn_used.astype(jnp.int32), hs, wg, wu, wd)

    return pl.pallas_call(
        functools.partial(_combine_kernel, alpha=alpha),
        grid_spec=pltpu.PrefetchScalarGridSpec(
            num_scalar_prefetch=4,
            grid=(n_pairs,),
            in_specs=[pl.BlockSpec((MOE_TILE, d), lambda s, pw, pt, n, st: (pt[s], 0)),
                      pl.BlockSpec((MOE_WIN, 128), lambda s, pw, pt, n, st: (pw[s], 0)),
                      pl.BlockSpec((MOE_WIN, d), lambda s, pw, pt, n, st: (pw[s], 0)),
                      pl.BlockSpec((None, None, 1, d), lambda s, pw, pt, n, st: (pw[s] // per, 5, 0, 0)),
                      pl.BlockSpec((1, d), lambda s, pw, pt, n, st: (0, 0)),
                      pl.BlockSpec((1, d), lambda s, pw, pt, n, st: (0, 0))],
            out_specs=pl.BlockSpec((MOE_WIN, d), lambda s, pw, pt, n, st: (pw[s], 0)),
            scratch_shapes=[pltpu.VMEM((MOE_WIN, d), F32), pltpu.VMEM((MOE_WIN, 128), F32)]),
        out_shape=jax.ShapeDtypeStruct((m, d), F32),
        compiler_params=_params(("arbitrary",)),
        name="moe_combine",
    )(c_win, c_tile, c_n, start, ys, col, x2, mod4, lng.reshape(1, d), lnb.reshape(1, d))


def _in_columns():
    cols = np.full((N_COLS,), -1, np.int64)
    scale = np.ones((N_COLS,), np.float32)
    qs = LOG2E / math.sqrt(HEAD_DIM)

    def put(dst, src, width, s=1.0):
        cols[dst:dst + width] = np.arange(src, src + width)
        scale[dst:dst + width] = s

    put(COL_MG, IN_MG, COL_NSA_Q - COL_MG)
    put(COL_NSA_Q, IN_NSA_Q, 256, qs)
    put(COL_NSA_CMP, IN_K_CMP, 64)
    put(COL_NSA_CMP + 64, IN_V_CMP, 64)
    put(COL_NSA_SEL, IN_K_SEL, 64)
    put(COL_NSA_SEL + 64, IN_K_SEL, 64)
    put(COL_NSA_WIN, IN_K_WIN, 64)
    put(COL_NSA_WIN + 64, IN_K_WIN, 64)
    for j, h in enumerate(SWA_SLOT_HEADS):
        put(COL_SWA_Q + j * HEAD_DIM, IN_SWA_Q + h * HEAD_DIM, HEAD_DIM, qs)
    put(COL_SWA_K, IN_SWA_K, 128)
    put(COL_DIL_Q, IN_DIL_Q, 768, qs)
    put(COL_DIL_K, IN_DIL_K, 256)
    put(COL_SB_Q, IN_SB_Q, 256, qs)
    put(COL_SB_K, IN_SB_K, 256)
    return cols, scale


def _in_rows_t():
    rows = np.full((N_ROWS_T,), -1, np.int64)

    def put(dst, src, width):
        rows[dst:dst + width] = np.arange(src, src + width)

    put(ROW_NSA_V, IN_V_SEL, 64)
    put(ROW_NSA_V + 64, IN_V_WIN, 64)
    put(ROW_SWA_V, IN_SWA_V, 128)
    put(ROW_DIL_V, IN_DIL_V, 256)
    put(ROW_SB_V, IN_SB_V, 256)
    put(ROW_GATE, IN_GATE, GATE_COLS)
    return rows


def _select_columns(w, src):
    d = w.shape[0]
    parts = []
    i = 0
    while i < len(src):
        j = i + 1
        while j < len(src) and (src[j] == src[j - 1] + 1 if src[i] >= 0 else src[j] < 0):
            j += 1
        parts.append(w[:, src[i]:src[i] + (j - i)] if src[i] >= 0 else jnp.zeros((d, j - i), w.dtype))
        i = j
    return jnp.concatenate(parts, axis=1)


def _relayout_w_in(w_in):
    cols, scale = _in_columns()
    w = _select_columns(w_in, cols) * jnp.asarray(scale)
    wt = _select_columns(w_in, _in_rows_t())
    return w.astype(BF16), wt.T.astype(BF16)


def kernel(x, c, rel_bias, w_ada, b_ada, w_in, w_branch, w_out, cmp_pe, cmp_w1, cmp_w2, swa_sinks, ln_g, ln_b,
           ffn_w_gate, ffn_w_up, ffn_w_down, moe_router, moe_w_gate, moe_w_up, moe_w_down):
    bsz, seq, d = x.shape
    depth = w_ada.shape[0]
    alpha = (2 * depth) ** 0.25
    m = bsz * seq
    nq = seq // BLK

    nsa_heads = tuple(range(NSA_HEADS))
    bias_sel = _bias_tiles(rel_bias, _bucket_tiles(nq, seq, 1), nsa_heads)
    bias_win = _bias_tiles(rel_bias, _bucket_tiles(-(-(NSA_WINDOW - 1) // BLK) + 1, NSA_WINDOW - 1, 1), nsa_heads)
    bias_swa = _bias_tiles(rel_bias, _bucket_tiles(-(-(SWA_WINDOW - 1) // BLK) + 1, SWA_WINDOW - 1, 1),
                           tuple(NSA_HEADS + h for h in SWA_SLOT_HEADS))
    bias_dil = []
    for gi, (win, dil) in enumerate(DIL_PATTERNS):
        h0 = NSA_HEADS + SWA_HEADS + gi * DIL_HEADS_PER_GROUP
        bias_dil.append(_bias_tiles(rel_bias, _bucket_tiles(min(win // BLK + 1, nq), win, dil),
                                    tuple(range(h0, h0 + DIL_HEADS_PER_GROUP))))
    swa_rows = np.concatenate([np.arange(h * HEAD_DIM, (h + 1) * HEAD_DIM) for h in SWA_SLOT_HEADS])

    x2 = x.reshape(m, d)
    for layer in range(depth):
        mod4 = _ada(c, w_ada[layer], b_ada[layer]).reshape(bsz, 6, 1, d)
        w_nat, w_t = _relayout_w_in(w_in[layer])
        z2, zt = _inproj(x2, mod4, w_nat, w_t, seq, sc_idx=1, sh_idx=0)
        zt = zt.reshape(bsz, nq, N_ROWS_T, BLK)
        z = z2.reshape(bsz, seq, N_COLS)
        chunks = z[:, :, COL_NSA_CMP:COL_NSA_CMP + 2 * HEAD_DIM].reshape(bsz, seq // CMP_STRIDE,
                                                                        CMP_STRIDE * 2 * HEAD_DIM)
        kc, vct = _compress(chunks, cmp_pe[layer], cmp_w1[layer], cmp_w2[layer])
        o_a = _nsa(z, zt, kc, vct, bias_sel, bias_win)
        o_b = _swa(z, zt, bias_swa, swa_sinks[layer])
        o_c = _dilated(z, zt, bias_dil)
        o_d = _stick_breaking(z, zt)
        wb = w_branch[layer]
        wb = jnp.stack([wb[0], wb[1][swa_rows], wb[2], wb[3]]).astype(BF16)
        branches = [a.reshape(m, BRANCH_WIDTH) for a in (o_a, o_b, o_c, o_d)]
        x2 = _merge(branches, z2, wb, w_out[layer].astype(BF16), x2, mod4, ln_g[layer, 0], ln_b[layer, 0], seq,
                    alpha=alpha, g_idx=2)
        i = layer // 2
        if layer % 2 == 0:
            x2 = _ffn(x2, mod4, ffn_w_gate[i].astype(BF16), ffn_w_up[i].astype(BF16), ffn_w_down[i].astype(BF16),
                      ln_g[layer, 1], ln_b[layer, 1], seq, alpha=alpha)
        else:
            x2 = _moe(x2, mod4, moe_router[i], moe_w_gate[i].astype(BF16), moe_w_up[i].astype(BF16),
                      moe_w_down[i].astype(BF16), ln_g[layer, 1], ln_b[layer, 1], seq, alpha=alpha)
    return x2.reshape(bsz, seq, d)
```

```python
import functools
import math

import numpy as np
import jax
import jax.numpy as jnp
from jax import lax
from jax.experimental import pallas as pl
from jax.experimental.pallas import tpu as pltpu

F32 = jnp.float32
BF16 = jnp.bfloat16
HIGHEST = lax.Precision.HIGHEST

HEAD_DIM = 64
BLK = 128
N_SLOTS = 4
NSA_HEADS = 4
CMP_LEN = 32
CMP_STRIDE = 16
SEL_LEN = 64
N_SEL = 16
NSA_WINDOW = 512
SWA_HEADS = 4
SWA_WINDOW = 128
SWA_SLOT_HEADS = (0, 2, 1, 3)
DIL_PATTERNS = ((128, 1), (512, 4), (2048, 16))
DIL_HEADS_PER_GROUP = 4
SB_HEADS = 4
N_BRANCHES = 4
BRANCH_WIDTH = 4 * HEAD_DIM
NUM_BUCKETS = 32
MAX_DISTANCE = 2048
N_EXPERTS = 8
LN_EPS = 1e-5
NEG_INF = -1e30
TINY = 1e-30
FORCE_SCORE = 1e4
LOG2E = math.log2(math.e)

IN_NSA_Q, IN_K_CMP, IN_V_CMP, IN_K_SEL, IN_V_SEL, IN_K_WIN, IN_V_WIN, IN_GATE = 0, 256, 320, 384, 448, 512, 576, 640
IN_SWA_Q, IN_SWA_K, IN_SWA_V = 652, 908, 1036
IN_DIL_Q, IN_DIL_K, IN_DIL_V = 1164, 1932, 2188
IN_SB_Q, IN_SB_K, IN_SB_V = 2444, 2700, 2956
IN_MG = 3212
GATE_COLS = NSA_HEADS * 3

COL_MG = 0
COL_NSA_Q = 4096
COL_NSA_CMP = 4352
COL_NSA_SEL = 4480
COL_NSA_WIN = 4608
COL_NSA_GATE = 4736
COL_SWA_Q = 4864
COL_SWA_K = 5120
COL_DIL_Q = 5376
COL_DIL_K = 6144
COL_SB_Q = 6400
COL_SB_K = 6656
N_COLS = 6912
ROW_NSA_V = 0
ROW_SWA_V = 128
ROW_DIL_V = 256
ROW_SB_V = 512
ROW_GATE = 768
N_ROWS_T = 784

VMEM_LIMIT = 56 * 1024 * 1024
ATTN_BATCH = 4


def _params(sem):
    return pltpu.CompilerParams(dimension_semantics=sem, vmem_limit_bytes=VMEM_LIMIT)


def _ln(x):
    mu = jnp.mean(x, axis=-1, keepdims=True)
    xc = x - mu
    var = jnp.mean(xc * xc, axis=-1, keepdims=True)
    return xc * lax.rsqrt(var + LN_EPS)


def _dot_nt(a, b, precision=None):
    return lax.dot_general(a, b, (((1,), (1,)), ((), ())), preferred_element_type=F32, precision=precision)


def _dot(a, b, precision=None):
    return jnp.dot(a, b, preferred_element_type=F32, precision=precision)


def _full(shape):
    return pl.BlockSpec(shape, lambda *_: (0,) * len(shape))


def _ada_kernel(c_ref, w_ref, b_ref, o_ref):
    c = c_ref[...]
    s = c * jax.nn.sigmoid(c)
    o_ref[...] = _dot(s, w_ref[...], HIGHEST) + b_ref[...]


def _ada(c, w, b):
    bsz, d = c.shape
    n = w.shape[1]
    tn = 1024
    return pl.pallas_call(
        _ada_kernel,
        grid=(n // tn,),
        in_specs=[pl.BlockSpec((bsz, d), lambda j: (0, 0)),
                  pl.BlockSpec((d, tn), lambda j: (0, j)),
                  pl.BlockSpec((1, tn), lambda j: (0, j))],
        out_specs=pl.BlockSpec((bsz, tn), lambda j: (0, j)),
        out_shape=jax.ShapeDtypeStruct((bsz, n), F32),
        compiler_params=_params(("parallel",)),
        name="ada",
    )(c, w, b.reshape(1, n))


def _inproj_kernel(x_ref, sc_ref, sh_ref, w_ref, wt_ref, o_ref, ot_ref, *, tn):
    h = (_ln(x_ref[...]) * (1.0 + sc_ref[...]) + sh_ref[...]).astype(BF16)
    for n0 in range(0, w_ref.shape[1], tn):
        o_ref[:, n0:n0 + tn] = _dot(h, w_ref[:, n0:n0 + tn]).astype(o_ref.dtype)
    res = _dot_nt(wt_ref[...], h)
    for j in range(ot_ref.shape[0]):
        ot_ref[j] = res[:, j * BLK:(j + 1) * BLK].astype(ot_ref.dtype)


def _inproj(x2, mod4, w, wt, seq, *, sc_idx, sh_idx, tm=512, n_chunks=3):
    m, d = x2.shape
    n = w.shape[1]
    rows = wt.shape[0]
    per = seq // tm
    resident = lambda shape: pl.BlockSpec(shape, lambda i: (0,) * len(shape), pipeline_mode=pl.Buffered(1))
    return pl.pallas_call(
        functools.partial(_inproj_kernel, tn=n // n_chunks),
        grid=(m // tm,),
        in_specs=[pl.BlockSpec((tm, d), lambda i: (i, 0)),
                  pl.BlockSpec((None, None, 1, d), lambda i: (i // per, sc_idx, 0, 0)),
                  pl.BlockSpec((None, None, 1, d), lambda i: (i // per, sh_idx, 0, 0)),
                  resident(w.shape), resident(wt.shape)],
        out_specs=(pl.BlockSpec((tm, n), lambda i: (i, 0)),
                   pl.BlockSpec((tm // BLK, rows, BLK), lambda i: (i, 0, 0))),
        out_shape=(jax.ShapeDtypeStruct((m, n), BF16),
                   jax.ShapeDtypeStruct((m // BLK, rows, BLK), BF16)),
        compiler_params=_params(("parallel",)),
        name="inproj",
    )(x2, mod4, mod4, w, wt)


def _np_bucket(dist):
    dist = np.maximum(dist, 0)
    max_exact = NUM_BUCKETS // 2
    d_f = np.maximum(dist, 1).astype(np.float32)
    large = max_exact + (np.log(d_f / np.float32(max_exact)) / np.float32(math.log(MAX_DISTANCE / max_exact))
                         * np.float32(NUM_BUCKETS - max_exact)).astype(np.int32)
    large = np.minimum(large, NUM_BUCKETS - 1)
    return np.where(dist < max_exact, dist, large).astype(np.int32)


def _bucket_tiles(n_off, max_dist, dil):
    b = np.arange(BLK)[:, None]
    a = np.arange(BLK)[None, :]
    tiles = []
    for o in range(n_off):
        dist = o * BLK + a - b
        ok = (dist >= 0) & (dist <= max_dist) & (dist % dil == 0)
        tiles.append(np.where(ok, _np_bucket(dist), -1))
    return np.stack(tiles).astype(np.int32)


def _bias_kernel(tbl_ref, bk_ref, o_ref, *, heads):
    slot = pl.program_id(0)
    h = jnp.int32(heads[0])
    for j in range(1, len(heads)):
        h = jnp.where(slot == j, heads[j], h)
    b = bk_ref[...]
    out = jnp.where(b < 0, NEG_INF, 0.0).astype(F32)
    for k in range(NUM_BUCKETS):
        out = jnp.where(b == k, tbl_ref[k, h] * LOG2E, out)
    o_ref[...] = out


def _bias_tiles(rel_bias, buckets, heads):
    n_off = buckets.shape[0]
    return pl.pallas_call(
        functools.partial(_bias_kernel, heads=tuple(heads)),
        grid=(len(heads),),
        in_specs=[pl.BlockSpec(memory_space=pltpu.SMEM), _full((n_off, BLK, BLK))],
        out_specs=pl.BlockSpec((n_off, BLK, BLK), lambda s: (0, 0, s)),
        out_shape=jax.ShapeDtypeStruct((n_off, BLK, len(heads) * BLK), F32),
        compiler_params=_params(("parallel",)),
        name="bias_tiles",
    )(rel_bias, jnp.asarray(buckets))


def _slot_queries(q, kw):
    lane = lax.broadcasted_iota(jnp.int32, (BLK, kw), 1)
    zero = jnp.zeros((BLK, kw), q.dtype)
    parts = []
    for j in range(N_SLOTS):
        lo = j * HEAD_DIM
        grp = q[:, (lo // kw) * kw:(lo // kw + 1) * kw]
        inside = (lane >= lo % kw) & (lane < lo % kw + HEAD_DIM)
        parts.append(jnp.where(inside, grp, zero))
    return jnp.concatenate(parts, axis=0)


def _softmax_step(s, vt, state, slot_rows):
    m, l = state[0], state[1]
    m_new = jnp.maximum(m, jnp.max(s, axis=0, keepdims=True))
    alpha = jnp.exp2(m - m_new)
    p = jnp.exp2(s - m_new)
    l = alpha * l + jnp.sum(p, axis=0, keepdims=True)
    yield
    res = _slot_values(vt, p.astype(BF16), slot_rows)
    yield
    accs = [alpha[:, j * BLK:(j + 1) * BLK] * state[2 + j] + res[j] for j in range(N_SLOTS)]
    return (m_new, l, *accs)


def _softmax_init():
    return (jnp.full((1, N_SLOTS * BLK), NEG_INF, F32), jnp.zeros((1, N_SLOTS * BLK), F32),
            *[jnp.zeros((HEAD_DIM, BLK), F32) for _ in range(N_SLOTS)])


def _store_heads(o_ref, outs):
    o_ref[...] = jnp.transpose(jnp.concatenate(outs, axis=0)).astype(o_ref.dtype)


def _key_rows(k_ref, kb):
    return k_ref[pl.ds(pl.multiple_of(kb * BLK, BLK), BLK), :]


def _slot_values(vt, p, slot_rows):
    if vt.shape[0] > 2 * HEAD_DIM:
        return [_dot(vt[slot_rows[j]:slot_rows[j] + HEAD_DIM, :], p[:, j * BLK:(j + 1) * BLK])
                for j in range(N_SLOTS)]
    res = _dot(vt, p)
    return [res[slot_rows[j]:slot_rows[j] + HEAD_DIM, j * BLK:(j + 1) * BLK] for j in range(N_SLOTS)]


def _window_softmax(qi, wq, k_ref, vt_of, bias_ref, n_off, slot_rows):
    tiles = []
    m = None
    for o in range(n_off):
        kb = qi - o
        kbc = jnp.maximum(kb, 0)
        s = _dot_nt(_key_rows(k_ref, kbc), wq) + bias_ref[o]
        if o > 0:
            s = s + jnp.where(kb >= 0, 0.0, NEG_INF)
        tiles.append((kbc, s))
        mo = jnp.max(s, axis=0, keepdims=True)
        m = mo if m is None else jnp.maximum(m, mo)
    l = None
    res = None
    for kbc, s in tiles:
        p = jnp.exp2(s - m)
        lo = jnp.sum(p, axis=0, keepdims=True)
        ro = _slot_values(vt_of(kbc), p.astype(BF16), slot_rows)
        l = lo if l is None else l + lo
        res = ro if res is None else [a + b for a, b in zip(res, ro)]
    return (m, l, *res)


def _run_interleaved(gens):
    results = [None] * len(gens)
    live = list(range(len(gens)))
    while live:
        for i in tuple(live):
            try:
                next(gens[i])
            except StopIteration as done:
                results[i] = done.value
                live.remove(i)
    return results


def _chained_loop(lo, hi, steps, states):
    def body(kb, sts):
        return tuple(_run_interleaved([f(kb, st) for f, st in zip(steps, sts)]))
    return lax.fori_loop(lo, hi, body, tuple(states))


def _compress_kernel(ch_ref, wa_ref, wb_ref, pea_ref, peb_ref, w2k_ref, w2vt_ref, kc_ref, vct_ref):
    ch = ch_ref[...].astype(F32)
    wa = wa_ref[...]
    wb = wb_ref[...]
    first = _dot(ch, wa, HIGHEST)
    second = _dot(ch, wb, HIGHEST)
    pe = (_dot(pea_ref[...], wa, HIGHEST) + _dot(peb_ref[...], wb, HIGHEST))[0:1]
    n = second.shape[0]
    nxt = pltpu.roll(second, n - 1, 0)
    hid = jax.nn.gelu(first + nxt + pe)
    kc_ref[...] = _dot(hid, w2k_ref[...], HIGHEST).astype(kc_ref.dtype)
    vct_ref[...] = _dot_nt(w2vt_ref[...], hid, HIGHEST).astype(vct_ref.dtype)


def _compress(chunks, cmp_pe, cmp_w1, cmp_w2):
    bsz, n_chunk, width = chunks.shape
    per = CMP_LEN // 2
    hd = HEAD_DIM
    w1 = cmp_w1.reshape(2, CMP_LEN, hd, hd)
    wexp = jnp.einsum('cpdn,ce->pcden', w1, jnp.eye(2, dtype=F32)).reshape(CMP_LEN, 2 * hd, 2 * hd)
    wa = wexp[:per].reshape(width, 2 * hd)
    wb = wexp[per:].reshape(width, 2 * hd)
    pe = jnp.transpose(cmp_pe, (1, 0, 2)).reshape(CMP_LEN, 2 * hd)
    pea = jnp.broadcast_to(pe[:per].reshape(1, width), (8, width))
    peb = jnp.broadcast_to(pe[per:].reshape(1, width), (8, width))
    zero = jnp.zeros((hd, hd), F32)
    w2k = jnp.concatenate([jnp.concatenate([cmp_w2[0], cmp_w2[0]], 1), jnp.concatenate([zero, zero], 1)], 0)
    w2vt = jnp.concatenate([zero, cmp_w2[1].T], 1)
    return pl.pallas_call(
        _compress_kernel,
        grid=(bsz,),
        in_specs=[pl.BlockSpec((None, n_chunk, width), lambda b: (b, 0, 0)),
                  _full(wa.shape), _full(wb.shape), _full(pea.shape), _full(peb.shape), _full(w2k.shape),
                  _full(w2vt.shape)],
        out_specs=(pl.BlockSpec((None, n_chunk, 2 * hd), lambda b: (b, 0, 0)),
                   pl.BlockSpec((None, hd, n_chunk), lambda b: (b, 0, 0))),
        out_shape=(jax.ShapeDtypeStruct((bsz, n_chunk, 2 * hd), BF16),
                   jax.ShapeDtypeStruct((bsz, hd, n_chunk), BF16)),
        compiler_params=_params(("parallel",)),
        name="nsa_compress",
    )(chunks, wa, wb, pea, peb, w2k, w2vt)


def _nsa_kernel(q_ref, kc_ref, vct_ref, ovt_ref, ksel_ref, kwin_ref, vt_ref, gate_ref, bsel_ref, bwin_ref, et_ref,
                o_ref, *, n_blk, n_win_off):
    qi = pl.program_id(1)
    t0 = qi * BLK
    nb = q_ref.shape[0]
    n_cmp = kc_ref.shape[1]
    slot_rows = (0,) * N_SLOTS
    shape = (n_cmp, N_SLOTS * BLK)
    tq = t0 + lax.broadcasted_iota(jnp.int32, shape, 1) % BLK
    ci = lax.broadcasted_iota(jnp.int32, shape, 0)
    ok = (ci * CMP_STRIDE + (CMP_LEN - 1)) <= tq
    jb = lax.broadcasted_iota(jnp.int32, (n_blk, BLK), 0)
    cur = (t0 + lax.broadcasted_iota(jnp.int32, (n_blk, BLK), 1)) // SEL_LEN
    forced = (jb == 0) | (jb == cur) | (jb == cur - 1)

    o_cmps, o_wins, sel_steps = [], [], []
    for bb in range(nb):
        wq = _slot_queries(q_ref[bb], 2 * HEAD_DIM)

        s = jnp.where(ok, _dot_nt(kc_ref[bb], wq), NEG_INF)
        m = jnp.max(s, axis=0, keepdims=True)
        p = jnp.where(ok, jnp.exp2(s - m), 0.0)
        p = p / jnp.maximum(jnp.sum(p, axis=0, keepdims=True), TINY)
        o_cmps.append(_dot(vct_ref[bb], p.astype(BF16)))
        psum = p[:, 0:BLK]
        for j in range(1, N_SLOTS):
            psum = psum + p[:, j * BLK:(j + 1) * BLK]

        hi = psum.astype(BF16)
        lo = (psum - hi.astype(F32)).astype(BF16)
        score = _dot(ovt_ref[...], hi) + _dot(ovt_ref[...], lo)
        score = jnp.where(forced, FORCE_SCORE, jnp.where(jb > cur, NEG_INF, score))
        rank = jnp.zeros((n_blk, BLK), F32)
        for c in range(n_blk):
            row = score[c:c + 1, :]
            beats = (row > score) | ((row == score) & (jb > c))
            rank = rank + jnp.where(beats, 1.0, 0.0)
        sel = jnp.where(rank < N_SEL, 1.0, 0.0).astype(BF16)

        st = _window_softmax(qi, wq, kwin_ref.at[bb], lambda kb, bb=bb: vt_ref[bb, kb, HEAD_DIM:2 * HEAD_DIM, :],
                             bwin_ref, n_win_off, slot_rows)
        o_wins.append([st[2 + j] / st[1][:, j * BLK:(j + 1) * BLK] for j in range(N_SLOTS)])

        def sel_step(kb, state, bb=bb, wq=wq, sel=sel):
            pen = (_dot(et_ref[kb], sel) - 1.0) * (-NEG_INF)
            pen = jnp.concatenate([pen] * N_SLOTS, axis=1)
            sc = _dot_nt(_key_rows(ksel_ref.at[bb], kb), wq) + bsel_ref[qi - kb] + pen
            yield
            return (yield from _softmax_step(sc, vt_ref[bb, kb, 0:HEAD_DIM, :], state, slot_rows))

        sel_steps.append(sel_step)

    sel_states = _run_interleaved([f(qi, _softmax_init()) for f in sel_steps])
    sel_states = _chained_loop(0, qi, sel_steps, sel_states)
    for bb in range(nb):
        st = sel_states[bb]
        gate = jax.nn.sigmoid(gate_ref[bb].astype(F32))
        outs = []
        for j in range(N_SLOTS):
            o_sel = st[2 + j] / st[1][:, j * BLK:(j + 1) * BLK]
            outs.append(gate[3 * j:3 * j + 1] * o_cmps[bb][:, j * BLK:(j + 1) * BLK]
                        + gate[3 * j + 1:3 * j + 2] * o_sel + gate[3 * j + 2:3 * j + 3] * o_wins[bb][j])
        _store_heads(o_ref.at[bb], outs)


def _nsa(z, zt, kc, vct, bias_sel, bias_win, nb=ATTN_BATCH):
    bsz, seq, _ = z.shape
    nq = seq // BLK
    n_cmp = kc.shape[1]
    n_blk = seq // SEL_LEN
    assert n_blk >= N_SEL
    ci = np.arange(n_cmp)[None, :]
    sj = np.arange(n_blk)[:, None]
    overlap_t = ((ci * CMP_STRIDE + CMP_LEN - 1 >= sj * SEL_LEN) & (ci * CMP_STRIDE < (sj + 1) * SEL_LEN)
                 & (ci < n_cmp - 1)).astype(np.float32)
    key_blk = (np.arange(seq) // SEL_LEN).reshape(nq, BLK, 1)
    expand_t = (key_blk == np.arange(n_blk)[None, None, :]).astype(np.float32)
    qw = NSA_HEADS * HEAD_DIM
    kw = 2 * HEAD_DIM
    return pl.pallas_call(
        functools.partial(_nsa_kernel, n_blk=n_blk, n_win_off=bias_win.shape[0]),
        grid=(bsz // nb, nq),
        in_specs=[pl.BlockSpec((nb, BLK, qw), lambda b, i: (b, i, COL_NSA_Q // qw)),
                  pl.BlockSpec((nb, n_cmp, kw), lambda b, i: (b, 0, 0)),
                  pl.BlockSpec((nb, HEAD_DIM, n_cmp), lambda b, i: (b, 0, 0)),
                  _full(overlap_t.shape),
                  pl.BlockSpec((nb, seq, kw), lambda b, i: (b, 0, COL_NSA_SEL // kw)),
                  pl.BlockSpec((nb, seq, kw), lambda b, i: (b, 0, COL_NSA_WIN // kw)),
                  pl.BlockSpec((nb, nq, kw, BLK), lambda b, i: (b, 0, ROW_NSA_V // kw, 0)),
                  pl.BlockSpec((nb, None, 16, BLK), lambda b, i: (b, i, ROW_GATE // 16, 0)),
                  _full(bias_sel.shape), _full(bias_win.shape), _full(expand_t.shape)],
        out_specs=pl.BlockSpec((nb, BLK, qw), lambda b, i: (b, i, 0)),
        out_shape=jax.ShapeDtypeStruct((bsz, seq, qw), BF16),
        compiler_params=_params(("parallel", "parallel")),
        name="nsa_mixer",
    )(z, kc, vct, jnp.asarray(overlap_t, dtype=BF16), z, z, zt, zt, bias_sel, bias_win,
      jnp.asarray(expand_t, dtype=BF16))


def _swa_kernel(q_ref, k_ref, vt_ref, bias_ref, sink_ref, o_ref, *, n_off):
    qi = pl.program_id(1)
    slot_rows = tuple((j % 2) * HEAD_DIM for j in range(N_SLOTS))
    for bb in range(q_ref.shape[0]):
        wq = _slot_queries(q_ref[bb], 2 * HEAD_DIM)
        st = _window_softmax(qi, wq, k_ref.at[bb], lambda kb, bb=bb: vt_ref[bb, kb], bias_ref, n_off, slot_rows)
        m, l = st[0], st[1]
        outs = []
        for j in range(N_SLOTS):
            sink = sink_ref[SWA_SLOT_HEADS[j]] * LOG2E
            mj = m[:, j * BLK:(j + 1) * BLK]
            lj = l[:, j * BLK:(j + 1) * BLK]
            m2 = jnp.maximum(mj, sink)
            scale = jnp.exp2(mj - m2)
            den = lj * scale + jnp.exp2(sink - m2)
            outs.append(st[2 + j] * (scale / den))
        _store_heads(o_ref.at[bb], outs)


def _swa(z, zt, bias, sinks, nb=ATTN_BATCH):
    bsz, seq, _ = z.shape
    nq = seq // BLK
    qw = SWA_HEADS * HEAD_DIM
    kw = 2 * HEAD_DIM
    return pl.pallas_call(
        functools.partial(_swa_kernel, n_off=bias.shape[0]),
        grid=(bsz // nb, nq),
        in_specs=[pl.BlockSpec((nb, BLK, qw), lambda b, i: (b, i, COL_SWA_Q // qw)),
                  pl.BlockSpec((nb, seq, kw), lambda b, i: (b, 0, COL_SWA_K // kw)),
                  pl.BlockSpec((nb, nq, kw, BLK), lambda b, i: (b, 0, ROW_SWA_V // kw, 0)),
                  _full(bias.shape),
                  pl.BlockSpec(memory_space=pltpu.SMEM)],
        out_specs=pl.BlockSpec((nb, BLK, qw), lambda b, i: (b, i, 0)),
        out_shape=jax.ShapeDtypeStruct((bsz, seq, qw), BF16),
        compiler_params=_params(("parallel", "parallel")),
        name="swa_mixer",
    )(z, z, zt, bias, sinks)


def _dil_kernel(q0_ref, q1_ref, q2_ref, k_ref, vt_ref, b0_ref, b1_ref, b2_ref, o_ref):
    qi = pl.program_id(1)
    nb = q0_ref.shape[0]
    slot_rows = tuple(j * HEAD_DIM for j in range(N_SLOTS))
    kw = N_SLOTS * HEAD_DIM
    short, far_steps = [], []
    for bb in range(nb):
        vt_of = lambda kb, bb=bb: vt_ref[bb, kb]
        short.append([_window_softmax(qi, _slot_queries(q_ref[bb], kw), k_ref.at[bb], vt_of, bias_ref,
                                      bias_ref.shape[0], slot_rows)
                      for q_ref, bias_ref in ((q0_ref, b0_ref), (q1_ref, b1_ref))])
        wq2 = _slot_queries(q2_ref[bb], kw)

        def far_step(kb, state, bb=bb, wq2=wq2, vt_of=vt_of):
            sc = _dot_nt(_key_rows(k_ref.at[bb], kb), wq2) + b2_ref[qi - kb]
            yield
            return (yield from _softmax_step(sc, vt_of(kb), state, slot_rows))

        far_steps.append(far_step)
    far_states = _run_interleaved([f(qi, _softmax_init()) for f in far_steps])
    far_states = _chained_loop(jnp.maximum(qi - (b2_ref.shape[0] - 1), 0), qi, far_steps, far_states)
    for bb in range(nb):
        groups = short[bb] + [far_states[bb]]
        lses = [st[0] + jnp.log(st[1]) * LOG2E for st in groups]
        top = jnp.maximum(jnp.maximum(lses[0], lses[1]), lses[2])
        es = [jnp.exp2(v - top) for v in lses]
        den = es[0] + es[1] + es[2]
        outs = []
        for j in range(N_SLOTS):
            sl = slice(j * BLK, (j + 1) * BLK)
            acc = jnp.zeros((HEAD_DIM, BLK), F32)
            for st, e in zip(groups, es):
                acc = acc + (e[:, sl] / den[:, sl]) * (st[2 + j] / st[1][:, sl])
            outs.append(acc)
        _store_heads(o_ref.at[bb], outs)


def _dilated(z, zt, biases, nb=ATTN_BATCH):
    bsz, seq, _ = z.shape
    nq = seq // BLK
    qw = DIL_HEADS_PER_GROUP * HEAD_DIM
    qspec = lambda g: pl.BlockSpec((nb, BLK, qw), lambda b, i: (b, i, COL_DIL_Q // qw + g))
    return pl.pallas_call(
        _dil_kernel,
        grid=(bsz // nb, nq),
        in_specs=[qspec(0), qspec(1), qspec(2),
                  pl.BlockSpec((nb, seq, qw), lambda b, i: (b, 0, COL_DIL_K // qw)),
                  pl.BlockSpec((nb, nq, qw, BLK), lambda b, i: (b, 0, ROW_DIL_V // qw, 0)),
                  _full(biases[0].shape), _full(biases[1].shape), _full(biases[2].shape)],
        out_specs=pl.BlockSpec((nb, BLK, qw), lambda b, i: (b, i, 0)),
        out_shape=jax.ShapeDtypeStruct((bsz, seq, qw), BF16),
        compiler_params=_params(("parallel", "parallel")),
        name="dilated_mixer",
    )(z, z, z, z, zt, *biases)


def _sb_kernel(q_ref, k_ref, vt_ref, u_ref, o_ref):
    qi = pl.program_id(1)
    nb = q_ref.shape[0]
    u = u_ref[...]
    slot_rows = tuple(j * HEAD_DIM for j in range(N_SLOTS))
    shape = (BLK, N_SLOTS * BLK)
    strict =lax.broadcasted_iota(jnp.int32, shape, 0) < lax.broadcasted_iota(jnp.int32, shape, 1) % BLK

    def step(bb, wq, kb, state, diag):
        run = state[0]
        zz = _dot_nt(_key_rows(k_ref.at[bb], kb), wq)
        yield
        ls = jnp.minimum(zz, 0.0) - jnp.log(1.0 + jnp.exp2(-jnp.abs(zz))) * LOG2E
        lf = ls - zz
        if diag:
            lf = jnp.where(strict, lf, 0.0)
        hi = lf.astype(BF16)
        lo = (lf - hi.astype(F32)).astype(BF16)
        after = _dot(u, jnp.concatenate([hi, lo], axis=0)) + run
        yield
        a = jnp.exp2(ls + after)
        if diag:
            a = jnp.where(strict, a, 0.0)
        res = _slot_values(vt_ref[bb, kb], a.astype(BF16), slot_rows)
        yield
        accs = [state[1 + j] + res[j] for j in range(N_SLOTS)]
        return (run + jnp.sum(lf, axis=0, keepdims=True), *accs)

    init = (jnp.zeros((1, N_SLOTS * BLK), F32), *[jnp.zeros((HEAD_DIM, BLK), F32) for _ in range(N_SLOTS)])
    wqs = [_slot_queries(q_ref[bb], SB_HEADS * HEAD_DIM) for bb in range(nb)]
    steps = [lambda t, st, bb=bb: step(bb, wqs[bb], qi - 1 - t, st, False) for bb in range(nb)]
    states = _run_interleaved([step(bb, wqs[bb], qi, init, True) for bb in range(nb)])
    states = _chained_loop(0, qi, steps, states)
    for bb in range(nb):
        _store_heads(o_ref.at[bb], list(states[bb][1:]))


def _stick_breaking(z, zt, nb=ATTN_BATCH):
    bsz, seq, _ = z.shape
    nq = seq // BLK
    w = SB_HEADS * HEAD_DIM
    later = (np.arange(BLK)[None, :] > np.arange(BLK)[:, None]).astype(np.float32)
    later = np.concatenate([later, later], axis=1)
    return pl.pallas_call(
        _sb_kernel,
        grid=(bsz // nb, nq),
        in_specs=[pl.BlockSpec((nb, BLK, w), lambda b, i: (b, i, COL_SB_Q // w)),
                  pl.BlockSpec((nb, seq, w), lambda b, i: (b, 0, COL_SB_K // w)),
                  pl.BlockSpec((nb, nq, w, BLK), lambda b, i: (b, 0, ROW_SB_V // w, 0)),
                  _full((BLK, 2 * BLK))],
        out_specs=pl.BlockSpec((nb, BLK, w), lambda b, i: (b, i, 0)),
        out_shape=jax.ShapeDtypeStruct((bsz, seq, w), BF16),
        compiler_params=_params(("parallel", "parallel")),
        name="stick_breaking",
    )(z, z, zt, jnp.asarray(later, dtype=BF16))


def _merge_kernel(oa_ref, ob_ref, oc_ref, od_ref, mg_ref, wb_ref, wo_ref, x_ref, g_ref, lng_ref, lnb_ref, o_ref,
                  *, alpha):
    d = x_ref.shape[1]
    merged = jnp.zeros((x_ref.shape[0], d), F32)
    for b, ref in enumerate((oa_ref, ob_ref, oc_ref, od_ref)):
        proj = _dot(ref[...], wb_ref[b])
        merged = merged + jax.nn.sigmoid(mg_ref[:, b * d:(b + 1) * d].astype(F32)) * proj
    y = _dot(merged.astype(BF16), wo_ref[...])
    v = alpha * x_ref[...] + g_ref[...] * y
    o_ref[...] = _ln(v) * lng_ref[...] + lnb_ref[...]


def _merge(branches, z2, wb, wo, x2, mod4, lng, lnb, seq, *, alpha, g_idx, tm=512):
    m, d = x2.shape
    bw = BRANCH_WIDTH
    per = seq // tm
    row = lambda w: pl.BlockSpec((tm, w), lambda i: (i, 0))
    return pl.pallas_call(
        functools.partial(_merge_kernel, alpha=alpha),
        grid=(m // tm,),
        in_specs=[row(bw), row(bw), row(bw), row(bw),
                  pl.BlockSpec((tm, N_BRANCHES * d), lambda i: (i, COL_MG // (N_BRANCHES * d))),
                  _full(wb.shape), _full(wo.shape),
                  row(d),
                  pl.BlockSpec((None, None, 1, d), lambda i: (i // per, g_idx, 0, 0)),
                  _full((1, d)), _full((1, d))],
        out_specs=row(d),
        out_shape=jax.ShapeDtypeStruct((m, d), F32),
        compiler_params=_params(("parallel",)),
        name="merge_out",
    )(*branches, z2, wb, wo, x2, mod4, lng.reshape(1, d), lnb.reshape(1, d))


def _ffn_kernel(x_ref, sc_ref, sh_ref, g_ref, lng_ref, lnb_ref, wg_ref, wu_ref, wd_ref, o_ref, *, alpha, tf):
    x = x_ref[...]
    h = (_ln(x) * (1.0 + sc_ref[...]) + sh_ref[...]).astype(BF16)
    y = None
    for f0 in range(0, wg_ref.shape[1], tf):
        a = _dot(h, wg_ref[:, f0:f0 + tf])
        u = _dot(h, wu_ref[:, f0:f0 + tf])
        act = (a * jax.nn.sigmoid(a)) * u
        t = _dot(act.astype(BF16), wd_ref[f0:f0 + tf, :])
        y = t if y is None else y + t
    v = alpha * x + g_ref[...] * y
    o_ref[...] = _ln(v) * lng_ref[...] + lnb_ref[...]


def _ffn(x2, mod4, wg, wu, wd, lng, lnb, seq, *, alpha, tm=512, tf=1408):
    m, d = x2.shape
    per = seq // tm
    modspec = lambda k: pl.BlockSpec((None, None, 1, d), lambda i: (i // per, k, 0, 0))
    resident = lambda shape: pl.BlockSpec(shape, lambda i: (0,) * len(shape), pipeline_mode=pl.Buffered(1))
    return pl.pallas_call(
        functools.partial(_ffn_kernel, alpha=alpha, tf=tf),
        grid=(m // tm,),
        in_specs=[pl.BlockSpec((tm, d), lambda i: (i, 0)),
                  modspec(4), modspec(3), modspec(5),
                  _full((1, d)), _full((1, d)),
                  resident(wg.shape), resident(wu.shape), resident(wd.shape)],
        out_specs=pl.BlockSpec((tm, d), lambda i: (i, 0)),
        out_shape=jax.ShapeDtypeStruct((m, d), F32),
        compiler_params=_params(("parallel",)),
        name="ffn_dense",
    )(x2, mod4, mod4, mod4, lng.reshape(1, d), lnb.reshape(1, d), wg, wu, wd)


MOE_WIN = 512
MOE_TILE = 256
MOE_FFN_TILE = 512
MOE_RING = 4
INFO_E1, INFO_E2, INFO_W1, INFO_W2, INFO_R1, INFO_R2 = range(6)


def _router_kernel(x_ref, sc_ref, sh_ref, wrh_ref, wrl_ref, ls_ref, h_ref, col_ref, row_ref, cum_ref, cnt_ref,
                   carry_ref):
    @pl.when(pl.program_id(0) == 0)
    def _():
        carry_ref[...] = jnp.zeros_like(carry_ref)

    tm = x_ref.shape[0]
    lane = lax.broadcasted_iota(jnp.int32, (tm, 128), 1).astype(F32)
    h32 = _ln(x_ref[...]) * (1.0 + sc_ref[...]) + sh_ref[...]
    h_hi = h32.astype(BF16)
    h_ref[...] = h_hi
    h_lo = (h32 - h_hi.astype(F32)).astype(BF16)
    logits = _dot(h_hi, wrh_ref[...]) + (_dot(h_lo, wrh_ref[...]) + _dot(h_hi, wrl_ref[...]))
    logits = jnp.where(lane < N_EXPERTS, logits, -jnp.inf)
    t1 = jnp.max(logits, axis=1, keepdims=True)
    i1 = jnp.min(jnp.where(logits == t1, lane, 128.0), axis=1, keepdims=True)
    rest = jnp.where(lane == i1, -jnp.inf, logits)
    t2 = jnp.max(rest, axis=1, keepdims=True)
    i2 = jnp.min(jnp.where(rest == t2, lane, 128.0), axis=1, keepdims=True)
    e2 = jnp.exp(t2 - t1)
    den = 1.0 + e2
    w1 = 1.0 / den
    w2 = e2 / den
    onehot = jnp.where(lane == i1, 1.0, jnp.where(lane == i2, 1.0, 0.0))
    before = _dot(ls_ref[...], onehot.astype(BF16)) + carry_ref[0:1, :]
    r1 = jnp.sum(jnp.where(lane == i1, before, 0.0), axis=1, keepdims=True)
    r2 = jnp.sum(jnp.where(lane == i2, before, 0.0), axis=1, keepdims=True)
    info = jnp.zeros((tm, 128), F32)
    for k, v in ((INFO_E1, i1), (INFO_E2, i2), (INFO_W1, w1), (INFO_W2, w2), (INFO_R1, r1), (INFO_R2, r2)):
        info = jnp.where(lane == k, v, info)
    col_ref[...] = info
    row_ref[...] = jnp.transpose(info)[0:8, :]
    cnt = jnp.sum(onehot, axis=0, keepdims=True)
    cum_ref[...] = carry_ref[...]
    cnt_ref[...] = jnp.broadcast_to(cnt, cnt_ref.shape)
    carry_ref[...] = carry_ref[...] + cnt


def _router(x2, mod4, wr, seq):
    m, d = x2.shape
    tm = MOE_WIN
    per = seq // tm
    n_win = m // tm
    wr_pad = jnp.pad(wr, ((0, 0), (0, 128 - wr.shape[1])))
    wr_hi = wr_pad.astype(BF16)
    wr_lo = (wr_pad - wr_hi.astype(F32)).astype(BF16)
    earlier =(np.arange(tm)[:, None] > np.arange(tm)[None, :]).astype(np.float32)
    modspec = lambda k: pl.BlockSpec((None, None, 1, d), lambda i: (i // per, k, 0, 0))
    return pl.pallas_call(
        _router_kernel,
        grid=(n_win,),
        in_specs=[pl.BlockSpec((tm, d), lambda i: (i, 0)), modspec(4), modspec(3),
                  _full((d, 128)), _full((d, 128)), _full((tm, tm))],
        out_specs=(pl.BlockSpec((tm, d), lambda i: (i, 0)),
                   pl.BlockSpec((tm, 128), lambda i: (i, 0)),
                   pl.BlockSpec((None, 8, tm), lambda i: (i, 0, 0)),
                   pl.BlockSpec((None, 8, 128), lambda i: (i, 0, 0)),
                   pl.BlockSpec((None, 8, 128), lambda i: (i, 0, 0))),
        out_shape=(jax.ShapeDtypeStruct((m, d), BF16),
                   jax.ShapeDtypeStruct((m, 128), F32),
                   jax.ShapeDtypeStruct((n_win, 8, tm), F32),
                   jax.ShapeDtypeStruct((n_win, 8, 128), F32),
                   jax.ShapeDtypeStruct((n_win, 8, 128), F32)),
        scratch_shapes=[pltpu.VMEM((8, 128), F32)],
        compiler_params=_params(("arbitrary",)),
        name="moe_router",
    )(x2, mod4, mod4, wr_hi, wr_lo, jnp.asarray(earlier, dtype=BF16))


def _expert_offset(e, start_ref):
    off = jnp.zeros_like(e)
    for k in range(N_EXPERTS):
        off = jnp.where(e == k, start_ref[k].astype(F32), off)
    return off


def _pair_flags(s, n, key_ref):
    last_idx = key_ref.shape[0] - 1
    key = key_ref[s]
    first = (s == 0) | (key != key_ref[jnp.maximum(s - 1, 0)])
    last = (s == n - 1) | (key != key_ref[jnp.minimum(s + 1, last_idx)])
    return s < n, first, last


def _ring_copy(src_ref, block_rows, idx_ref, buf_ref, sem_ref, step):
    slot = step % MOE_RING
    row0 = pl.multiple_of(idx_ref[step] * block_rows, block_rows)
    return pltpu.make_async_copy(src_ref.at[pl.ds(row0, block_rows)], buf_ref.at[slot], sem_ref.at[slot])


def _ring_start_ahead(src_ref, block_rows, idx_ref, buf_ref, sem_ref, s, n):
    ahead = MOE_RING - 1

    @pl.when(s == 0)
    def _():
        for k in range(ahead):
            @pl.when(k < n)
            def _():
                _ring_copy(src_ref, block_rows, idx_ref, buf_ref, sem_ref, k).start()

    @pl.when(s + ahead < n)
    def _():
        _ring_copy(src_ref, block_rows, idx_ref, buf_ref, sem_ref, s + ahead).start()


def _gather_kernel(pt_ref, pw_ref, n_ref, start_ref, h_any, row_ref, hs_ref, acc_ref, hbuf_ref, sem_ref):
    s = pl.program_id(0)
    n = n_ref[0]
    active, first, last = _pair_flags(s, n, pt_ref)
    _ring_start_ahead(h_any, MOE_WIN, pw_ref, hbuf_ref, sem_ref, s, n)

    @pl.when(active & first)
    def _():
        acc_ref[...] = jnp.zeros_like(acc_ref)

    @pl.when(active)
    def _():
        _ring_copy(h_any, MOE_WIN, pw_ref, hbuf_ref, sem_ref, s).wait()
        h_ref = hbuf_ref.at[s % MOE_RING]
        info = row_ref[pw_ref[s]]
        half = info.shape[1] // 2
        p1 = info[INFO_R1:INFO_R1 + 1] + _expert_offset(info[INFO_E1:INFO_E1 + 1], start_ref)
        p2 = info[INFO_R2:INFO_R2 + 1] + _expert_offset(info[INFO_E2:INFO_E2 + 1], start_ref)
        rows = (pt_ref[s] * MOE_TILE + lax.broadcasted_iota(jnp.int32, (MOE_TILE, half), 0)).astype(F32)
        part = None
        for c in range(2):
            sl = slice(c * half, (c + 1) * half)
            perm = jnp.where(rows == p1[:, sl], 1.0, jnp.where(rows == p2[:, sl], 1.0, 0.0)).astype(BF16)
            term = _dot(perm, h_ref[sl, :])
            part = term if part is None else part + term
        acc_ref[...] += part

    @pl.when(active & last)
    def _():
        hs_ref[...] = acc_ref[...].astype(hs_ref.dtype)


def _moe_ffn_kernel(te_ref, used_ref, hs_ref, wg_ref, wu_ref, wd_ref, ys_ref, *, tf):
    used = used_ref[pl.program_id(0)] > 0

    @pl.when(used)
    def _():
        h = hs_ref[...]
        y = None
        for f0 in range(0, wg_ref.shape[1], tf):
            a = _dot(h, wg_ref[:, f0:f0 + tf])
            u = _dot(h, wu_ref[:, f0:f0 + tf])
            act = (a * jax.nn.sigmoid(a)) * u
            t = _dot(act.astype(BF16), wd_ref[f0:f0 + tf, :])
            y = t if y is None else y + t
        ys_ref[...] = y.astype(ys_ref.dtype)

    @pl.when(jnp.logical_not(used))
    def _():
        ys_ref[...] = jnp.zeros_like(ys_ref)


def _combine_kernel(pw_ref, pt_ref, n_ref, start_ref, ys_any, col_ref, x_ref, g_ref, lng_ref, lnb_ref, o_ref,
                    acc_ref, pos_ref, ysbuf_ref, sem_ref, *, alpha):
    s = pl.program_id(0)
    n = n_ref[0]
    active, first, last = _pair_flags(s, n, pw_ref)
    _ring_start_ahead(ys_any, MOE_TILE, pt_ref, ysbuf_ref, sem_ref, s, n)

    @pl.when(active & first)
    def _():
        acc_ref[...] = jnp.zeros_like(acc_ref)
        info = col_ref[...]
        pos_ref[:, 0:1] = info[:, INFO_R1:INFO_R1 + 1] + _expert_offset(info[:, INFO_E1:INFO_E1 + 1], start_ref)
        pos_ref[:, 1:2] = info[:, INFO_R2:INFO_R2 + 1] + _expert_offset(info[:, INFO_E2:INFO_E2 + 1], start_ref)

    @pl.when(active)
    def _():
        win = col_ref.shape[0]
        p1 = pos_ref[:, 0:1]
        p2 = pos_ref[:, 1:2]
        row0 = (pt_ref[s] * MOE_TILE).astype(F32)
        in1 = (p1 >= row0) & (p1 < row0 + MOE_TILE)
        in2 = (p2 >= row0) & (p2 < row0 + MOE_TILE)
        weight = jnp.where(in1, col_ref[:, INFO_W1:INFO_W1 + 1], jnp.where(in2, col_ref[:, INFO_W2:INFO_W2 + 1], 0.0))
        cols = lax.broadcasted_iota(jnp.int32, (win, MOE_TILE), 1).astype(F32) + row0
        perm = jnp.where(cols == p1, 1.0, jnp.where(cols == p2, 1.0, 0.0)).astype(BF16)
        _ring_copy(ys_any, MOE_TILE, pt_ref, ysbuf_ref, sem_ref, s).wait()
        acc_ref[...] += weight * _dot(perm, ysbuf_ref[s % MOE_RING])

    @pl.when(active & last)
    def _():
        v = alpha * x_ref[...] + g_ref[...] * acc_ref[...]
        o_ref[...] = _ln(v) * lng_ref[...] + lnb_ref[...]


def _pair_list(mask, n_pairs):
    flat = mask.reshape(-1)
    n = jnp.sum(flat.astype(jnp.int32))
    idx = jnp.nonzero(flat, size=n_pairs, fill_value=0)[0].astype(jnp.int32)
    idx = jnp.where(jnp.arange(n_pairs) < n, idx, idx[jnp.maximum(n - 1, 0)])
    return idx // mask.shape[1], idx % mask.shape[1], n.reshape(1)


def _moe(x2, mod4, wr, wg, wu, wd, lng, lnb, seq, *, alpha, tf=1792):
    m, d = x2.shape
    n_exp, _, ff = wg.shape
    n_win = m // MOE_WIN
    per = seq // MOE_WIN
    n_rows = 2 * m + n_exp * MOE_FFN_TILE
    n_tiles = n_rows // MOE_TILE
    n_ffn = n_rows // MOE_FFN_TILE
    n_pairs = n_exp * n_win + n_tiles + 3 * n_exp

    h, col, row, cum, cnt = _router(x2, mod4, wr, seq)

    cum = cum[:, 0, :n_exp].astype(jnp.int32)
    cnt = cnt[:, 0, :n_exp].astype(jnp.int32)
    total = cum[-1] + cnt[-1]
    padded = (total + MOE_FFN_TILE - 1) // MOE_FFN_TILE * MOE_FFN_TILE
    end = jnp.cumsum(padded)
    start = (end - padded).astype(jnp.int32)

    def tile_expert(tile_rows):
        row0 = jnp.arange(n_rows // tile_rows, dtype=jnp.int32) * tile_rows
        te = jnp.minimum(jnp.sum((row0[:, None] >= end[None, :]).astype(jnp.int32), axis=1), n_exp - 1)
        return row0, te, row0 < end[-1]

    row0, te, used = tile_expert(MOE_TILE)
    local0 = row0 - start[te]
    lo = jnp.take(cum.T, te, axis=0)
    hi = lo + jnp.take(cnt.T, te, axis=0)
    meet = (lo < local0[:, None] + MOE_TILE) & (hi > local0[:, None]) & (hi > lo) & used[:, None]
    visit = meet.at[:, 0].set(meet[:, 0] | jnp.logical_not(jnp.any(meet, axis=1)))
    g_tile, g_win, g_n = _pair_list(visit, n_pairs)
    c_win, c_tile, c_n = _pair_list(meet.T, n_pairs)
    _, ffn_te, ffn_used = tile_expert(MOE_FFN_TILE)

    hs = pl.pallas_call(
        _gather_kernel,
        grid_spec=pltpu.PrefetchScalarGridSpec(
            num_scalar_prefetch=4,
            grid=(n_pairs,),
            in_specs=[pl.BlockSpec(memory_space=pl.ANY),
                      pl.BlockSpec((n_win, 8, MOE_WIN), lambda s, pt, pw, n, st: (0, 0, 0))],
            out_specs=pl.BlockSpec((MOE_TILE, d), lambda s, pt, pw, n, st: (pt[s], 0)),
            scratch_shapes=[pltpu.VMEM((MOE_TILE, d), F32), pltpu.VMEM((MOE_RING, MOE_WIN, d), BF16),
                            pltpu.SemaphoreType.DMA((MOE_RING,))]),
        out_shape=jax.ShapeDtypeStruct((n_rows, d), BF16),
        compiler_params=_params(("arbitrary",)),
        name="moe_gather",
    )(g_tile, g_win, g_n, start, h, row)

    expert = lambda shape: pl.BlockSpec((None,) + shape, lambda j, te, us: (te[j], 0, 0),
                                        pipeline_mode=pl.Buffered(1))
    ys = pl.pallas_call(
        functools.partial(_moe_ffn_kernel, tf=tf),
        grid_spec=pltpu.PrefetchScalarGridSpec(
            num_scalar_prefetch=2,
            grid=(n_ffn,),
            in_specs=[pl.BlockSpec((MOE_FFN_TILE, d), lambda j, te, us: (j, 0)),
                      expert((d, ff)), expert((d, ff)), expert((ff, d))],
            out_specs=pl.BlockSpec((MOE_FFN_TILE, d), lambda j, te, us: (j, 0))),
        out_shape=jax.ShapeDtypeStruct((n_rows, d), BF16),
        compiler_params=_params(("arbitrary",)),
        name="moe_experts",
    )(ffn_te, ffn_used.astype(jnp.int32), hs, wg, wu, wd)

    return pl.pallas_call(
        functools.partial(_combine_kernel, alpha=alpha),
        grid_spec=pltpu.PrefetchScalarGridSpec(
            num_scalar_prefetch=4,
            grid=(n_pairs,),
            in_specs=[pl.BlockSpec(memory_space=pl.ANY),
                      pl.BlockSpec((MOE_WIN, 128), lambda s, pw, pt, n, st: (pw[s], 0)),
                      pl.BlockSpec((MOE_WIN, d), lambda s, pw, pt, n, st: (pw[s], 0)),
                      pl.BlockSpec((None, None, 1, d), lambda s, pw, pt, n, st: (pw[s] // per, 5, 0, 0)),
                      pl.BlockSpec((1, d), lambda s, pw, pt, n, st: (0, 0)),
                      pl.BlockSpec((1, d), lambda s, pw, pt, n, st: (0, 0))],
            out_specs=pl.BlockSpec((MOE_WIN, d), lambda s, pw, pt, n, st: (pw[s], 0)),
            scratch_shapes=[pltpu.VMEM((MOE_WIN, d), F32), pltpu.VMEM((MOE_WIN, 128), F32),
                            pltpu.VMEM((MOE_RING, MOE_TILE, d), BF16), pltpu.SemaphoreType.DMA((MOE_RING,))]),
        out_shape=jax.ShapeDtypeStruct((m, d), F32),
        compiler_params=_params(("arbitrary",)),
        name="moe_combine",
    )(c_win, c_tile, c_n, start, ys, col, x2, mod4, lng.reshape(1, d), lnb.reshape(1, d))


def _in_columns():
    cols = np.full((N_COLS,), -1, np.int64)
    scale = np.ones((N_COLS,), np.float32)
    qs = LOG2E / math.sqrt(HEAD_DIM)

    def put(dst, src, width, s=1.0):
        cols[dst:dst + width] = np.arange(src, src + width)
        scale[dst:dst + width] = s

    put(COL_MG, IN_MG, COL_NSA_Q - COL_MG)
    put(COL_NSA_Q, IN_NSA_Q, 256, qs)
    put(COL_NSA_CMP, IN_K_CMP, 64)
    put(COL_NSA_CMP + 64, IN_V_CMP, 64)
    put(COL_NSA_SEL, IN_K_SEL, 64)
    put(COL_NSA_SEL + 64, IN_K_SEL, 64)
    put(COL_NSA_WIN, IN_K_WIN, 64)
    put(COL_NSA_WIN + 64, IN_K_WIN, 64)
    for j, h in enumerate(SWA_SLOT_HEADS):
        put(COL_SWA_Q + j * HEAD_DIM, IN_SWA_Q + h * HEAD_DIM, HEAD_DIM, qs)
    put(COL_SWA_K, IN_SWA_K, 128)
    put(COL_DIL_Q, IN_DIL_Q, 768, qs)
    put(COL_DIL_K, IN_DIL_K, 256)
    put(COL_SB_Q, IN_SB_Q, 256, qs)
    put(COL_SB_K, IN_SB_K, 256)
    return cols, scale


def _in_rows_t():
    rows = np.full((N_ROWS_T,), -1, np.int64)

    def put(dst, src, width):
        rows[dst:dst + width] = np.arange(src, src + width)

    put(ROW_NSA_V, IN_V_SEL, 64)
    put(ROW_NSA_V + 64, IN_V_WIN, 64)
    put(ROW_SWA_V, IN_SWA_V, 128)
    put(ROW_DIL_V, IN_DIL_V, 256)
    put(ROW_SB_V, IN_SB_V, 256)
    put(ROW_GATE, IN_GATE, GATE_COLS)
    return rows


def _select_columns(w, src):
    d = w.shape[0]
    parts = []
    i = 0
    while i < len(src):
        j = i + 1
        while j < len(src) and (src[j] == src[j - 1] + 1 if src[i] >= 0 else src[j] < 0):
            j += 1
        parts.append(w[:, src[i]:src[i] + (j - i)] if src[i] >= 0 else jnp.zeros((d, j - i), w.dtype))
        i = j
    return jnp.concatenate(parts, axis=1)


def _relayout_w_in(w_in):
    cols, scale = _in_columns()
    w = _select_columns(w_in, cols) * jnp.asarray(scale)
    wt = _select_columns(w_in, _in_rows_t())
    return w.astype(BF16), wt.T.astype(BF16)


def kernel(x, c, rel_bias, w_ada, b_ada, w_in, w_branch, w_out, cmp_pe, cmp_w1, cmp_w2, swa_sinks, ln_g, ln_b,
           ffn_w_gate, ffn_w_up, ffn_w_down, moe_router, moe_w_gate, moe_w_up, moe_w_down):
    bsz, seq, d = x.shape
    depth = w_ada.shape[0]
    alpha = (2 * depth) ** 0.25
    m = bsz * seq
    nq = seq // BLK

    nsa_heads = tuple(range(NSA_HEADS))
    bias_sel = _bias_tiles(rel_bias, _bucket_tiles(nq, seq, 1), nsa_heads)
    bias_win = _bias_tiles(rel_bias, _bucket_tiles(-(-(NSA_WINDOW - 1) // BLK) + 1, NSA_WINDOW - 1, 1), nsa_heads)
    bias_swa = _bias_tiles(rel_bias, _bucket_tiles(-(-(SWA_WINDOW - 1) // BLK) + 1, SWA_WINDOW - 1, 1),
                           tuple(NSA_HEADS + h for h in SWA_SLOT_HEADS))
    bias_dil = []
    for gi, (win, dil) in enumerate(DIL_PATTERNS):
        h0 = NSA_HEADS + SWA_HEADS + gi * DIL_HEADS_PER_GROUP
        bias_dil.append(_bias_tiles(rel_bias, _bucket_tiles(min(win // BLK + 1, nq), win, dil),
                                    tuple(range(h0, h0 + DIL_HEADS_PER_GROUP))))
    swa_rows = np.concatenate([np.arange(h * HEAD_DIM, (h + 1) * HEAD_DIM) for h in SWA_SLOT_HEADS])

    x2 = x.reshape(m, d)
    for layer in range(depth):
        mod4 = _ada(c, w_ada[layer], b_ada[layer]).reshape(bsz, 6, 1, d)
        w_nat, w_t = _relayout_w_in(w_in[layer])
        z2, zt = _inproj(x2, mod4, w_nat, w_t, seq, sc_idx=1, sh_idx=0)
        zt = zt.reshape(bsz, nq, N_ROWS_T, BLK)
        z = z2.reshape(bsz, seq, N_COLS)
        chunks = z[:, :, COL_NSA_CMP:COL_NSA_CMP + 2 * HEAD_DIM].reshape(bsz, seq // CMP_STRIDE,
                                                                        CMP_STRIDE * 2 * HEAD_DIM)
        kc, vct = _compress(chunks, cmp_pe[layer], cmp_w1[layer], cmp_w2[layer])
        o_a = _nsa(z, zt, kc, vct, bias_sel, bias_win)
        o_b = _swa(z, zt, bias_swa, swa_sinks[layer])
        o_c = _dilated(z, zt, bias_dil)
        o_d = _stick_breaking(z, zt)
        wb = w_branch[layer]
        wb = jnp.stack([wb[0], wb[1][swa_rows], wb[2], wb[3]]).astype(BF16)
        branches = [a.reshape(m, BRANCH_WIDTH) for a in (o_a, o_b, o_c, o_d)]
        x2 = _merge(branches, z2, wb, w_out[layer].astype(BF16), x2, mod4, ln_g[layer, 0], ln_b[layer, 0], seq,
                    alpha=alpha, g_idx=2)
        i = layer // 2
        if layer % 2 == 0:
            x2 = _ffn(x2, mod4, ffn_w_gate[i].astype(BF16), ffn_w_up[i].astype(BF16), ffn_w_down[i].astype(BF16),
                      ln_g[layer, 1], ln_b[layer, 1], seq, alpha=alpha)
        else:
            x2 = _moe(x2, mod4, moe_router[i], moe_w_gate[i].astype(BF16), moe_w_up[i].astype(BF16),
                      moe_w_down[i].astype(BF16), ln_g[layer, 1], ln_b[layer, 1], seq, alpha=alpha)
    return x2.reshape(bsz, seq, d)
```

```python
import functools
import math

import numpy as np
import jax
import jax.numpy as jnp
from jax import lax
from jax.experimental import pallas as pl
from jax.experimental.pallas import tpu as pltpu

F32 = jnp.float32
BF16 = jnp.bfloat16
HIGHEST = lax.Precision.HIGHEST

HEAD_DIM = 64
BLK = 128
N_SLOTS = 4
NSA_HEADS = 4
CMP_LEN = 32
CMP_STRIDE = 16
SEL_LEN = 64
N_SEL = 16
NSA_WINDOW = 512
SWA_HEADS = 4
SWA_WINDOW = 128
SWA_SLOT_HEADS = (0, 2, 1, 3)
DIL_PATTERNS = ((128, 1), (512, 4), (2048, 16))
DIL_HEADS_PER_GROUP = 4
SB_HEADS = 4
N_BRANCHES = 4
BRANCH_WIDTH = 4 * HEAD_DIM
NUM_BUCKETS = 32
MAX_DISTANCE = 2048
N_EXPERTS = 8
LN_EPS = 1e-5
NEG_INF = -1e30
TINY = 1e-30
FORCE_SCORE = 1e4
LOG2E = math.log2(math.e)

IN_NSA_Q, IN_K_CMP, IN_V_CMP, IN_K_SEL, IN_V_SEL, IN_K_WIN, IN_V_WIN, IN_GATE = 0, 256, 320, 384, 448, 512, 576, 640
IN_SWA_Q, IN_SWA_K, IN_SWA_V = 652, 908, 1036
IN_DIL_Q, IN_DIL_K, IN_DIL_V = 1164, 1932, 2188
IN_SB_Q, IN_SB_K, IN_SB_V = 2444, 2700, 2956
IN_MG = 3212
GATE_COLS = NSA_HEADS * 3

COL_MG = 0
COL_NSA_Q = 4096
COL_NSA_CMP = 4352
COL_NSA_SEL = 4480
COL_NSA_WIN = 4608
COL_NSA_GATE = 4736
COL_SWA_Q = 4864
COL_SWA_K = 5120
COL_DIL_Q = 5376
COL_DIL_K = 6144
COL_SB_Q = 6400
COL_SB_K = 6656
N_COLS = 6912
ROW_NSA_V = 0
ROW_SWA_V = 128
ROW_DIL_V = 256
ROW_SB_V = 512
ROW_GATE = 768
N_ROWS_T = 784

VMEM_LIMIT = 56 * 1024 * 1024
ATTN_BATCH = 4


def _params(sem):
    return pltpu.CompilerParams(dimension_semantics=sem, vmem_limit_bytes=VMEM_LIMIT)


def _ln(x):
    mu = jnp.mean(x, axis=-1, keepdims=True)
    xc = x - mu
    var = jnp.mean(xc * xc, axis=-1, keepdims=True)
    return xc * lax.rsqrt(var + LN_EPS)


def _dot_nt(a, b, precision=None):
    return lax.dot_general(a, b, (((1,), (1,)), ((), ())), preferred_element_type=F32, precision=precision)


def _dot(a, b, precision=None):
    return jnp.dot(a, b, preferred_element_type=F32, precision=precision)


def _full(shape):
    return pl.BlockSpec(shape, lambda *_: (0,) * len(shape))


def _ada_kernel(c_ref, w_ref, b_ref, o_ref):
    c = c_ref[...]
    s = c * jax.nn.sigmoid(c)
    o_ref[...] = _dot(s, w_ref[...], HIGHEST) + b_ref[...]


def _ada(c, w, b):
    bsz, d = c.shape
    n = w.shape[1]
    tn = 1024
    return pl.pallas_call(
        _ada_kernel,
        grid=(n // tn,),
        in_specs=[pl.BlockSpec((bsz, d), lambda j: (0, 0)),
                  pl.BlockSpec((d, tn), lambda j: (0, j)),
                  pl.BlockSpec((1, tn), lambda j: (0, j))],
        out_specs=pl.BlockSpec((bsz, tn), lambda j: (0, j)),
        out_shape=jax.ShapeDtypeStruct((bsz, n), F32),
        compiler_params=_params(("parallel",)),
        name="ada",
    )(c, w, b.reshape(1, n))


def _inproj_kernel(x_ref, sc_ref, sh_ref, w_ref, wt_ref, o_ref, ot_ref, *, tn):
    h = (_ln(x_ref[...]) * (1.0 + sc_ref[...]) + sh_ref[...]).astype(BF16)
    for n0 in range(0, w_ref.shape[1], tn):
        o_ref[:, n0:n0 + tn] = _dot(h, w_ref[:, n0:n0 + tn]).astype(o_ref.dtype)
    res = _dot_nt(wt_ref[...], h)
    for j in range(ot_ref.shape[0]):
        ot_ref[j] = res[:, j * BLK:(j + 1) * BLK].astype(ot_ref.dtype)


def _inproj(x2, mod4, w, wt, seq, *, sc_idx, sh_idx, tm=512, n_chunks=3):
    m, d = x2.shape
    n = w.shape[1]
    rows = wt.shape[0]
    per = seq // tm
    resident = lambda shape: pl.BlockSpec(shape, lambda i: (0,) * len(shape), pipeline_mode=pl.Buffered(1))
    return pl.pallas_call(
        functools.partial(_inproj_kernel, tn=n // n_chunks),
        grid=(m // tm,),
        in_specs=[pl.BlockSpec((tm, d), lambda i: (i, 0)),
                  pl.BlockSpec((None, None, 1, d), lambda i: (i // per, sc_idx, 0, 0)),
                  pl.BlockSpec((None, None, 1, d), lambda i: (i // per, sh_idx, 0, 0)),
                  resident(w.shape), resident(wt.shape)],
        out_specs=(pl.BlockSpec((tm, n), lambda i: (i, 0)),
                   pl.BlockSpec((tm // BLK, rows, BLK), lambda i: (i, 0, 0))),
        out_shape=(jax.ShapeDtypeStruct((m, n), BF16),
                   jax.ShapeDtypeStruct((m // BLK, rows, BLK), BF16)),
        compiler_params=_params(("parallel",)),
        name="inproj",
    )(x2, mod4, mod4, w, wt)


def _np_bucket(dist):
    dist = np.maximum(dist, 0)
    max_exact = NUM_BUCKETS // 2
    d_f = np.maximum(dist, 1).astype(np.float32)
    large = max_exact + (np.log(d_f / np.float32(max_exact)) / np.float32(math.log(MAX_DISTANCE / max_exact))
                         * np.float32(NUM_BUCKETS - max_exact)).astype(np.int32)
    large = np.minimum(large, NUM_BUCKETS - 1)
    return np.where(dist < max_exact, dist, large).astype(np.int32)


def _bucket_tiles(n_off, max_dist, dil):
    b = np.arange(BLK)[:, None]
    a = np.arange(BLK)[None, :]
    tiles = []
    for o in range(n_off):
        dist = o * BLK + a - b
        ok = (dist >= 0) & (dist <= max_dist) & (dist % dil == 0)
        tiles.append(np.where(ok, _np_bucket(dist), -1))
    return np.stack(tiles).astype(np.int32)


def _bias_kernel(tbl_ref, bk_ref, o_ref, *, heads):
    slot = pl.program_id(0)
    h = jnp.int32(heads[0])
    for j in range(1, len(heads)):
        h = jnp.where(slot == j, heads[j], h)
    b = bk_ref[...]
    out = jnp.where(b < 0, NEG_INF, 0.0).astype(F32)
    for k in range(NUM_BUCKETS):
        out = jnp.where(b == k, tbl_ref[k, h] * LOG2E, out)
    o_ref[...] = out


def _bias_tiles(rel_bias, buckets, heads):
    n_off = buckets.shape[0]
    return pl.pallas_call(
        functools.partial(_bias_kernel, heads=tuple(heads)),
        grid=(len(heads),),
        in_specs=[pl.BlockSpec(memory_space=pltpu.SMEM), _full((n_off, BLK, BLK))],
        out_specs=pl.BlockSpec((n_off, BLK, BLK), lambda s: (0, 0, s)),
        out_shape=jax.ShapeDtypeStruct((n_off, BLK, len(heads) * BLK), F32),
        compiler_params=_params(("parallel",)),
        name="bias_tiles",
    )(rel_bias, jnp.asarray(buckets))


def _slot_queries(q, kw):
    lane = lax.broadcasted_iota(jnp.int32, (BLK, kw), 1)
    zero = jnp.zeros((BLK, kw), q.dtype)
    parts = []
    for j in range(N_SLOTS):
        lo = j * HEAD_DIM
        grp = q[:, (lo // kw) * kw:(lo // kw + 1) * kw]
        inside = (lane >= lo % kw) & (lane < lo % kw + HEAD_DIM)
        parts.append(jnp.where(inside, grp, zero))
    return jnp.concatenate(parts, axis=0)


def _softmax_step(s, vt, state, slot_rows):
    m, l = state[0], state[1]
    m_new = jnp.maximum(m, jnp.max(s, axis=0, keepdims=True))
    alpha = jnp.exp2(m - m_new)
    p = jnp.exp2(s - m_new)
    l = alpha * l + jnp.sum(p, axis=0, keepdims=True)
    yield
    res = _slot_values(vt, p.astype(BF16), slot_rows)
    yield
    accs = [alpha[:, j * BLK:(j + 1) * BLK] * state[2 + j] + res[j] for j in range(N_SLOTS)]
    return (m_new, l, *accs)


def _softmax_init():
    return (jnp.full((1, N_SLOTS * BLK), NEG_INF, F32), jnp.zeros((1, N_SLOTS * BLK), F32),
            *[jnp.zeros((HEAD_DIM, BLK), F32) for _ in range(N_SLOTS)])


def _store_heads(o_ref, outs):
    o_ref[...] = jnp.transpose(jnp.concatenate(outs, axis=0)).astype(o_ref.dtype)


def _key_rows(k_ref, kb):
    return k_ref[pl.ds(pl.multiple_of(kb * BLK, BLK), BLK), :]


def _slot_values(vt, p, slot_rows):
    if vt.shape[0] > 2 * HEAD_DIM:
        return [_dot(vt[slot_rows[j]:slot_rows[j] + HEAD_DIM, :], p[:, j * BLK:(j + 1) * BLK])
                for j in range(N_SLOTS)]
    res = _dot(vt, p)
    return [res[slot_rows[j]:slot_rows[j] + HEAD_DIM, j * BLK:(j + 1) * BLK] for j in range(N_SLOTS)]


def _window_softmax(qi, wq, k_ref, vt_of, bias_ref, n_off, slot_rows):
    tiles = []
    m = None
    for o in range(n_off):
        kb = qi - o
        kbc = jnp.maximum(kb, 0)
        s = _dot_nt(_key_rows(k_ref, kbc), wq) + bias_ref[o]
        if o > 0:
            s = s + jnp.where(kb >= 0, 0.0, NEG_INF)
        tiles.append((kbc, s))
        mo = jnp.max(s, axis=0, keepdims=True)
        m = mo if m is None else jnp.maximum(m, mo)
    l = None
    res = None
    for kbc, s in tiles:
        p = jnp.exp2(s - m)
        lo = jnp.sum(p, axis=0, keepdims=True)
        ro = _slot_values(vt_of(kbc), p.astype(BF16), slot_rows)
        l = lo if l is None else l + lo
        res = ro if res is None else [a + b for a, b in zip(res, ro)]
    return (m, l, *res)


def _run_interleaved(gens):
    results = [None] * len(gens)
    live = list(range(len(gens)))
    while live:
        for i in tuple(live):
            try:
                next(gens[i])
            except StopIteration as done:
                results[i] = done.value
                live.remove(i)
    return results


def _chained_loop(lo, hi, steps, states):
    def advance(k, sts):
        return tuple(_run_interleaved([f(k, st) for f, st in zip(steps, sts)]))

    def body(i, sts):
        k = lo + 2 * i
        return advance(k + 1, advance(k, sts))

    pairs = (hi - lo) // 2
    states = lax.fori_loop(0, pairs, body, tuple(states))
    k_last = lo + 2 * pairs
    return lax.cond(k_last < hi, lambda sts: advance(k_last, sts), lambda sts: sts, states)


def _compress_kernel(ch_ref, wa_ref, wb_ref, pea_ref, peb_ref, w2k_ref, w2vt_ref, kc_ref, vct_ref):
    ch = ch_ref[...].astype(F32)
    wa = wa_ref[...]
    wb = wb_ref[...]
    first = _dot(ch, wa, HIGHEST)
    second = _dot(ch, wb, HIGHEST)
    pe = (_dot(pea_ref[...], wa, HIGHEST) + _dot(peb_ref[...], wb, HIGHEST))[0:1]
    n = second.shape[0]
    nxt = pltpu.roll(second, n - 1, 0)
    hid = jax.nn.gelu(first + nxt + pe)
    kc_ref[...] = _dot(hid, w2k_ref[...], HIGHEST).astype(kc_ref.dtype)
    vct_ref[...] = _dot_nt(w2vt_ref[...], hid, HIGHEST).astype(vct_ref.dtype)


def _compress(chunks, cmp_pe, cmp_w1, cmp_w2):
    bsz, n_chunk, width = chunks.shape
    per = CMP_LEN // 2
    hd = HEAD_DIM
    w1 = cmp_w1.reshape(2, CMP_LEN, hd, hd)
    wexp = jnp.einsum('cpdn,ce->pcden', w1, jnp.eye(2, dtype=F32)).reshape(CMP_LEN, 2 * hd, 2 * hd)
    wa = wexp[:per].reshape(width, 2 * hd)
    wb = wexp[per:].reshape(width, 2 * hd)
    pe = jnp.transpose(cmp_pe, (1, 0, 2)).reshape(CMP_LEN, 2 * hd)
    pea = jnp.broadcast_to(pe[:per].reshape(1, width), (8, width))
    peb = jnp.broadcast_to(pe[per:].reshape(1, width), (8, width))
    zero = jnp.zeros((hd, hd), F32)
    w2k = jnp.concatenate([jnp.concatenate([cmp_w2[0], cmp_w2[0]], 1), jnp.concatenate([zero, zero], 1)], 0)
    w2vt = jnp.concatenate([zero, cmp_w2[1].T], 1)
    return pl.pallas_call(
        _compress_kernel,
        grid=(bsz,),
        in_specs=[pl.BlockSpec((None, n_chunk, width), lambda b: (b, 0, 0)),
                  _full(wa.shape), _full(wb.shape), _full(pea.shape), _full(peb.shape), _full(w2k.shape),
                  _full(w2vt.shape)],
        out_specs=(pl.BlockSpec((None, n_chunk, 2 * hd), lambda b: (b, 0, 0)),
                   pl.BlockSpec((None, hd, n_chunk), lambda b: (b, 0, 0))),
        out_shape=(jax.ShapeDtypeStruct((bsz, n_chunk, 2 * hd), BF16),
                   jax.ShapeDtypeStruct((bsz, hd, n_chunk), BF16)),
        compiler_params=_params(("parallel",)),
        name="nsa_compress",
    )(chunks, wa, wb, pea, peb, w2k, w2vt)


def _nsa_kernel(q_ref, kc_ref, vct_ref, ovt_ref, ksel_ref, kwin_ref, vt_ref, gate_ref, bsel_ref, bwin_ref, et_ref,
                o_ref, *, n_blk, n_win_off):
    qi = pl.program_id(1)
    t0 = qi * BLK
    nb = q_ref.shape[0]
    n_cmp = kc_ref.shape[1]
    slot_rows = (0,) * N_SLOTS
    shape = (n_cmp, N_SLOTS * BLK)
    tq = t0 + lax.broadcasted_iota(jnp.int32, shape, 1) % BLK
    ci = lax.broadcasted_iota(jnp.int32, shape, 0)
    ok = (ci * CMP_STRIDE + (CMP_LEN - 1)) <= tq
    jb = lax.broadcasted_iota(jnp.int32, (n_blk, BLK), 0)
    cur = (t0 + lax.broadcasted_iota(jnp.int32, (n_blk, BLK), 1)) // SEL_LEN
    forced = (jb == 0) | (jb == cur) | (jb == cur - 1)

    o_cmps, o_wins, sel_steps = [], [], []
    for bb in range(nb):
        wq = _slot_queries(q_ref[bb], 2 * HEAD_DIM)

        s = jnp.where(ok, _dot_nt(kc_ref[bb], wq), NEG_INF)
        m = jnp.max(s, axis=0, keepdims=True)
        p = jnp.where(ok, jnp.exp2(s - m), 0.0)
        p = p / jnp.maximum(jnp.sum(p, axis=0, keepdims=True), TINY)
        o_cmps.append(_dot(vct_ref[bb], p.astype(BF16)))
        psum = p[:, 0:BLK]
        for j in range(1, N_SLOTS):
            psum = psum + p[:, j * BLK:(j + 1) * BLK]

        hi = psum.astype(BF16)
        lo = (psum - hi.astype(F32)).astype(BF16)
        score = _dot(ovt_ref[...], hi) + _dot(ovt_ref[...], lo)
        score = jnp.where(forced, FORCE_SCORE, jnp.where(jb > cur, NEG_INF, score))
        rank = jnp.zeros((n_blk, BLK), F32)
        for c in range(n_blk):
            row = score[c:c + 1, :]
            beats = (row > score) | ((row == score) & (jb > c))
            rank = rank + jnp.where(beats, 1.0, 0.0)
        sel = jnp.where(rank < N_SEL, 1.0, 0.0).astype(BF16)

        st = _window_softmax(qi, wq, kwin_ref.at[bb], lambda kb, bb=bb: vt_ref[bb, kb, HEAD_DIM:2 * HEAD_DIM, :],
                             bwin_ref, n_win_off, slot_rows)
        o_wins.append([st[2 + j] / st[1][:, j * BLK:(j + 1) * BLK] for j in range(N_SLOTS)])

        def sel_step(kb, state, bb=bb, wq=wq, sel=sel):
            pen = (_dot(et_ref[kb], sel) - 1.0) * (-NEG_INF)
            pen = jnp.concatenate([pen] * N_SLOTS, axis=1)
            sc = _dot_nt(_key_rows(ksel_ref.at[bb], kb), wq) + bsel_ref[qi - kb] + pen
            yield
            return (yield from _softmax_step(sc, vt_ref[bb, kb, 0:HEAD_DIM, :], state, slot_rows))

        sel_steps.append(sel_step)

    sel_states = _run_interleaved([f(qi, _softmax_init()) for f in sel_steps])
    sel_states = _chained_loop(0, qi, sel_steps, sel_states)
    for bb in range(nb):
        st = sel_states[bb]
        gate = jax.nn.sigmoid(gate_ref[bb].astype(F32))
        outs = []
        for j in range(N_SLOTS):
            o_sel = st[2 + j] / st[1][:, j * BLK:(j + 1) * BLK]
            outs.append(gate[3 * j:3 * j + 1] * o_cmps[bb][:, j * BLK:(j + 1) * BLK]
                        + gate[3 * j + 1:3 * j + 2] * o_sel + gate[3 * j + 2:3 * j + 3] * o_wins[bb][j])
        _store_heads(o_ref.at[bb], outs)


def _nsa(z, zt, kc, vct, bias_sel, bias_win, nb=ATTN_BATCH):
    bsz, seq, _ = z.shape
    nq = seq // BLK
    n_cmp = kc.shape[1]
    n_blk = seq // SEL_LEN
    assert n_blk >= N_SEL
    ci = np.arange(n_cmp)[None, :]
    sj = np.arange(n_blk)[:, None]
    overlap_t = ((ci * CMP_STRIDE + CMP_LEN - 1 >= sj * SEL_LEN) & (ci * CMP_STRIDE < (sj + 1) * SEL_LEN)
                 & (ci < n_cmp - 1)).astype(np.float32)
    key_blk = (np.arange(seq) // SEL_LEN).reshape(nq, BLK, 1)
    expand_t = (key_blk == np.arange(n_blk)[None, None, :]).astype(np.float32)
    qw = NSA_HEADS * HEAD_DIM
    kw = 2 * HEAD_DIM
    return pl.pallas_call(
        functools.partial(_nsa_kernel, n_blk=n_blk, n_win_off=bias_win.shape[0]),
        grid=(bsz // nb, nq),
        in_specs=[pl.BlockSpec((nb, BLK, qw), lambda b, i: (b, i, COL_NSA_Q // qw)),
                  pl.BlockSpec((nb, n_cmp, kw), lambda b, i: (b, 0, 0)),
                  pl.BlockSpec((nb, HEAD_DIM, n_cmp), lambda b, i: (b, 0, 0)),
                  _full(overlap_t.shape),
                  pl.BlockSpec((nb, seq, kw), lambda b, i: (b, 0, COL_NSA_SEL // kw)),
                  pl.BlockSpec((nb, seq, kw), lambda b, i: (b, 0, COL_NSA_WIN // kw)),
                  pl.BlockSpec((nb, nq, kw, BLK), lambda b, i: (b, 0, ROW_NSA_V // kw, 0)),
                  pl.BlockSpec((nb, None, 16, BLK), lambda b, i: (b, i, ROW_GATE // 16, 0)),
                  _full(bias_sel.shape), _full(bias_win.shape), _full(expand_t.shape)],
        out_specs=pl.BlockSpec((nb, BLK, qw), lambda b, i: (b, i, 0)),
        out_shape=jax.ShapeDtypeStruct((bsz, seq, qw), BF16),
        compiler_params=_params(("parallel", "parallel")),
        name="nsa_mixer",
    )(z, kc, vct, jnp.asarray(overlap_t, dtype=BF16), z, z, zt, zt, bias_sel, bias_win,
      jnp.asarray(expand_t, dtype=BF16))


def _swa_kernel(q_ref, k_ref, vt_ref, bias_ref, sink_ref, o_ref, *, n_off):
    qi = pl.program_id(1)
    slot_rows = tuple((j % 2) * HEAD_DIM for j in range(N_SLOTS))
    for bb in range(q_ref.shape[0]):
        wq = _slot_queries(q_ref[bb], 2 * HEAD_DIM)
        st = _window_softmax(qi, wq, k_ref.at[bb], lambda kb, bb=bb: vt_ref[bb, kb], bias_ref, n_off, slot_rows)
        m, l = st[0], st[1]
        outs = []
        for j in range(N_SLOTS):
            sink = sink_ref[SWA_SLOT_HEADS[j]] * LOG2E
            mj = m[:, j * BLK:(j + 1) * BLK]
            lj = l[:, j * BLK:(j + 1) * BLK]
            m2 = jnp.maximum(mj, sink)
            scale = jnp.exp2(mj - m2)
            den = lj * scale + jnp.exp2(sink - m2)
            outs.append(st[2 + j] * (scale / den))
        _store_heads(o_ref.at[bb], outs)


def _swa(z, zt, bias, sinks, nb=ATTN_BATCH):
    bsz, seq, _ = z.shape
    nq = seq // BLK
    qw = SWA_HEADS * HEAD_DIM
    kw = 2 * HEAD_DIM
    return pl.pallas_call(
        functools.partial(_swa_kernel, n_off=bias.shape[0]),
        grid=(bsz // nb, nq),
        in_specs=[pl.BlockSpec((nb, BLK, qw), lambda b, i: (b, i, COL_SWA_Q // qw)),
                  pl.BlockSpec((nb, seq, kw), lambda b, i: (b, 0, COL_SWA_K // kw)),
                  pl.BlockSpec((nb, nq, kw, BLK), lambda b, i: (b, 0, ROW_SWA_V // kw, 0)),
                  _full(bias.shape),
                  pl.BlockSpec(memory_space=pltpu.SMEM)],
        out_specs=pl.BlockSpec((nb, BLK, qw), lambda b, i: (b, i, 0)),
        out_shape=jax.ShapeDtypeStruct((bsz, seq, qw), BF16),
        compiler_params=_params(("parallel", "parallel")),
        name="swa_mixer",
    )(z, z, zt, bias, sinks)


def _dil_kernel(q0_ref, q1_ref, q2_ref, k_ref, vt_ref, b0_ref, b1_ref, b2_ref, o_ref):
    qi = pl.program_id(1)
    nb = q0_ref.shape[0]
    slot_rows = tuple(j * HEAD_DIM for j in range(N_SLOTS))
    kw = N_SLOTS * HEAD_DIM
    short, far_steps = [], []
    for bb in range(nb):
        vt_of = lambda kb, bb=bb: vt_ref[bb, kb]
        short.append([_window_softmax(qi, _slot_queries(q_ref[bb], kw), k_ref.at[bb], vt_of, bias_ref,
                                      bias_ref.shape[0], slot_rows)
                      for q_ref, bias_ref in ((q0_ref, b0_ref), (q1_ref, b1_ref))])
        wq2 = _slot_queries(q2_ref[bb], kw)

        def far_step(kb, state, bb=bb, wq2=wq2, vt_of=vt_of):
            sc = _dot_nt(_key_rows(k_ref.at[bb], kb), wq2) + b2_ref[qi - kb]
            yield
            return (yield from _softmax_step(sc, vt_of(kb), state, slot_rows))

        far_steps.append(far_step)
    far_states = _run_interleaved([f(qi, _softmax_init()) for f in far_steps])
    far_states = _chained_loop(jnp.maximum(qi - (b2_ref.shape[0] - 1), 0), qi, far_steps, far_states)
    for bb in range(nb):
        groups = short[bb] + [far_states[bb]]
        lses = [st[0] + jnp.log(st[1]) * LOG2E for st in groups]
        top = jnp.maximum(jnp.maximum(lses[0], lses[1]), lses[2])
        es = [jnp.exp2(v - top) for v in lses]
        den = es[0] + es[1] + es[2]
        outs = []
        for j in range(N_SLOTS):
            sl = slice(j * BLK, (j + 1) * BLK)
            acc = jnp.zeros((HEAD_DIM, BLK), F32)
            for st, e in zip(groups, es):
                acc = acc + (e[:, sl] / den[:, sl]) * (st[2 + j] / st[1][:, sl])
            outs.append(acc)
        _store_heads(o_ref.at[bb], outs)


def _dilated(z, zt, biases, nb=ATTN_BATCH):
    bsz, seq, _ = z.shape
    nq = seq // BLK
    qw = DIL_HEADS_PER_GROUP * HEAD_DIM
    qspec = lambda g: pl.BlockSpec((nb, BLK, qw), lambda b, i: (b, i, COL_DIL_Q // qw + g))
    return pl.pallas_call(
        _dil_kernel,
        grid=(bsz // nb, nq),
        in_specs=[qspec(0), qspec(1), qspec(2),
                  pl.BlockSpec((nb, seq, qw), lambda b, i: (b, 0, COL_DIL_K // qw)),
                  pl.BlockSpec((nb, nq, qw, BLK), lambda b, i: (b, 0, ROW_DIL_V // qw, 0)),
                  _full(biases[0].shape), _full(biases[1].shape), _full(biases[2].shape)],
        out_specs=pl.BlockSpec((nb, BLK, qw), lambda b, i: (b, i, 0)),
        out_shape=jax.ShapeDtypeStruct((bsz, seq, qw), BF16),
        compiler_params=_params(("parallel", "parallel")),
        name="dilated_mixer",
    )(z, z, z, z, zt, *biases)


def _sb_kernel(q_ref, k_ref, vt_ref, u_ref, o_ref):
    qi = pl.program_id(1)
    nb = q_ref.shape[0]
    u = u_ref[...]
    slot_rows = tuple(j * HEAD_DIM for j in range(N_SLOTS))
    shape = (BLK, N_SLOTS * BLK)
    strict =lax.broadcasted_iota(jnp.int32, shape, 0) < lax.broadcasted_iota(jnp.int32, shape, 1) % BLK

    def step(bb, wq, kb, state, diag):
        run = state[0]
        zz = _dot_nt(_key_rows(k_ref.at[bb], kb), wq)
        yield
        ls = jnp.minimum(zz, 0.0) - jnp.log(1.0 + jnp.exp2(-jnp.abs(zz))) * LOG2E
        lf = ls - zz
        if diag:
            lf = jnp.where(strict, lf, 0.0)
        hi = lf.astype(BF16)
        lo = (lf - hi.astype(F32)).astype(BF16)
        after = _dot(u, jnp.concatenate([hi, lo], axis=0)) + run
        yield
        a = jnp.exp2(ls + after)
        if diag:
            a = jnp.where(strict, a, 0.0)
        res = _slot_values(vt_ref[bb, kb], a.astype(BF16), slot_rows)
        yield
        accs = [state[1 + j] + res[j] for j in range(N_SLOTS)]
        return (run + jnp.sum(lf, axis=0, keepdims=True), *accs)

    init = (jnp.zeros((1, N_SLOTS * BLK), F32), *[jnp.zeros((HEAD_DIM, BLK), F32) for _ in range(N_SLOTS)])
    wqs = [_slot_queries(q_ref[bb], SB_HEADS * HEAD_DIM) for bb in range(nb)]
    steps = [lambda t, st, bb=bb: step(bb, wqs[bb], qi - 1 - t, st, False) for bb in range(nb)]
    states = _run_interleaved([step(bb, wqs[bb], qi, init, True) for bb in range(nb)])
    states = _chained_loop(0, qi, steps, states)
    for bb in range(nb):
        _store_heads(o_ref.at[bb], list(states[bb][1:]))


def _stick_breaking(z, zt, nb=ATTN_BATCH):
    bsz, seq, _ = z.shape
    nq = seq // BLK
    w = SB_HEADS * HEAD_DIM
    later = (np.arange(BLK)[None, :] > np.arange(BLK)[:, None]).astype(np.float32)
    later = np.concatenate([later, later], axis=1)
    return pl.pallas_call(
        _sb_kernel,
        grid=(bsz // nb, nq),
        in_specs=[pl.BlockSpec((nb, BLK, w), lambda b, i: (b, i, COL_SB_Q // w)),
                  pl.BlockSpec((nb, seq, w), lambda b, i: (b, 0, COL_SB_K // w)),
                  pl.BlockSpec((nb, nq, w, BLK), lambda b, i: (b, 0, ROW_SB_V // w, 0)),
                  _full((BLK, 2 * BLK))],
        out_specs=pl.BlockSpec((nb, BLK, w), lambda b, i: (b, i, 0)),
        out_shape=jax.ShapeDtypeStruct((bsz, seq, w), BF16),
        compiler_params=_params(("parallel", "parallel")),
        name="stick_breaking",
    )(z, z, zt, jnp.asarray(later, dtype=BF16))


def _merge_kernel(oa_ref, ob_ref, oc_ref, od_ref, mg_ref, wb_ref, wo_ref, x_ref, g_ref, lng_ref, lnb_ref, o_ref,
                  *, alpha):
    d = x_ref.shape[1]
    merged = jnp.zeros((x_ref.shape[0], d), F32)
    for b, ref in enumerate((oa_ref, ob_ref, oc_ref, od_ref)):
        proj = _dot(ref[...], wb_ref[b])
        gate = 0.5 * jnp.tanh(0.5 * mg_ref[:, b * d:(b + 1) * d].astype(F32)) + 0.5
        merged = merged + gate * proj
    y = _dot(merged.astype(BF16), wo_ref[...])
    v = alpha * x_ref[...] + g_ref[...] * y
    o_ref[...] = _ln(v) * lng_ref[...] + lnb_ref[...]


def _merge(branches, z2, wb, wo, x2, mod4, lng, lnb, seq, *, alpha, g_idx, tm=512):
    m, d = x2.shape
    bw = BRANCH_WIDTH
    per = seq // tm
    row = lambda w: pl.BlockSpec((tm, w), lambda i: (i, 0))
    return pl.pallas_call(
        functools.partial(_merge_kernel, alpha=alpha),
        grid=(m // tm,),
        in_specs=[row(bw), row(bw), row(bw), row(bw),
                  pl.BlockSpec((tm, N_BRANCHES * d), lambda i: (i, COL_MG // (N_BRANCHES * d))),
                  _full(wb.shape), _full(wo.shape),
                  row(d),
                  pl.BlockSpec((None, None, 1, d), lambda i: (i // per, g_idx, 0, 0)),
                  _full((1, d)), _full((1, d))],
        out_specs=row(d),
        out_shape=jax.ShapeDtypeStruct((m, d), F32),
        compiler_params=_params(("parallel",)),
        name="merge_out",
    )(*branches, z2, wb, wo, x2, mod4, lng.reshape(1, d), lnb.reshape(1, d))


def _ffn_kernel(x_ref, sc_ref, sh_ref, g_ref, lng_ref, lnb_ref, wg_ref, wu_ref, wd_ref, o_ref, *, alpha, tf):
    x = x_ref[...]
    h = (_ln(x) * (1.0 + sc_ref[...]) + sh_ref[...]).astype(BF16)
    y = None
    for f0 in range(0, wg_ref.shape[1], tf):
        a = _dot(h, wg_ref[:, f0:f0 + tf])
        u = _dot(h, wu_ref[:, f0:f0 + tf])
        act = (a * jax.nn.sigmoid(a)) * u
        t = _dot(act.astype(BF16), wd_ref[f0:f0 + tf, :])
        y = t if y is None else y + t
    v = alpha * x + g_ref[...] * y
    o_ref[...] = _ln(v) * lng_ref[...] + lnb_ref[...]


def _ffn(x2, mod4, wg, wu, wd, lng, lnb, seq, *, alpha, tm=512, tf=1408):
    m, d = x2.shape
    per = seq // tm
    modspec = lambda k: pl.BlockSpec((None, None, 1, d), lambda i: (i // per, k, 0, 0))
    resident = lambda shape: pl.BlockSpec(shape, lambda i: (0,) * len(shape), pipeline_mode=pl.Buffered(1))
    return pl.pallas_call(
        functools.partial(_ffn_kernel, alpha=alpha, tf=tf),
        grid=(m // tm,),
        in_specs=[pl.BlockSpec((tm, d), lambda i: (i, 0)),
                  modspec(4), modspec(3), modspec(5),
                  _full((1, d)), _full((1, d)),
                  resident(wg.shape), resident(wu.shape), resident(wd.shape)],
        out_specs=pl.BlockSpec((tm, d), lambda i: (i, 0)),
        out_shape=jax.ShapeDtypeStruct((m, d), F32),
        compiler_params=_params(("parallel",)),
        name="ffn_dense",
    )(x2, mod4, mod4, mod4, lng.reshape(1, d), lnb.reshape(1, d), wg, wu, wd)


MOE_WIN = 512
MOE_TILE = 256
MOE_FFN_TILE = 512
MOE_RING = 4
INFO_E1, INFO_E2, INFO_W1, INFO_W2, INFO_R1, INFO_R2 = range(6)


def _router_kernel(x_ref, sc_ref, sh_ref, wrh_ref, wrl_ref, ls_ref, h_ref, col_ref, row_ref, cum_ref, cnt_ref,
                   carry_ref):
    @pl.when(pl.program_id(0) == 0)
    def _():
        carry_ref[...] = jnp.zeros_like(carry_ref)

    tm = x_ref.shape[0]
    lane = lax.broadcasted_iota(jnp.int32, (tm, 128), 1).astype(F32)
    h32 = _ln(x_ref[...]) * (1.0 + sc_ref[...]) + sh_ref[...]
    h_hi = h32.astype(BF16)
    h_ref[...] = h_hi
    h_lo = (h32 - h_hi.astype(F32)).astype(BF16)
    logits = _dot(h_hi, wrh_ref[...]) + (_dot(h_lo, wrh_ref[...]) + _dot(h_hi, wrl_ref[...]))
    logits = jnp.where(lane < N_EXPERTS, logits, -jnp.inf)
    t1 = jnp.max(logits, axis=1, keepdims=True)
    i1 = jnp.min(jnp.where(logits == t1, lane, 128.0), axis=1, keepdims=True)
    rest = jnp.where(lane == i1, -jnp.inf, logits)
    t2 = jnp.max(rest, axis=1, keepdims=True)
    i2 = jnp.min(jnp.where(rest == t2, lane, 128.0), axis=1, keepdims=True)
    e2 = jnp.exp(t2 - t1)
    den = 1.0 + e2
    w1 = 1.0 / den
    w2 = e2 / den
    onehot = jnp.where(lane == i1, 1.0, jnp.where(lane == i2, 1.0, 0.0))
    before = _dot(ls_ref[...], onehot.astype(BF16)) + carry_ref[0:1, :]
    r1 = jnp.sum(jnp.where(lane == i1, before, 0.0), axis=1, keepdims=True)
    r2 = jnp.sum(jnp.where(lane == i2, before, 0.0), axis=1, keepdims=True)
    info = jnp.zeros((tm, 128), F32)
    for k, v in ((INFO_E1, i1), (INFO_E2, i2), (INFO_W1, w1), (INFO_W2, w2), (INFO_R1, r1), (INFO_R2, r2)):
        info = jnp.where(lane == k, v, info)
    col_ref[...] = info
    row_ref[...] = jnp.transpose(info)[0:8, :]
    cnt = jnp.sum(onehot, axis=0, keepdims=True)
    cum_ref[...] = carry_ref[...]
    cnt_ref[...] = jnp.broadcast_to(cnt, cnt_ref.shape)
    carry_ref[...] = carry_ref[...] + cnt


def _router(x2, mod4, wr, seq):
    m, d = x2.shape
    tm = MOE_WIN
    per = seq // tm
    n_win = m // tm
    wr_pad = jnp.pad(wr, ((0, 0), (0, 128 - wr.shape[1])))
    wr_hi = wr_pad.astype(BF16)
    wr_lo = (wr_pad - wr_hi.astype(F32)).astype(BF16)
    earlier =(np.arange(tm)[:, None] > np.arange(tm)[None, :]).astype(np.float32)
    modspec = lambda k: pl.BlockSpec((None, None, 1, d), lambda i: (i // per, k, 0, 0))
    return pl.pallas_call(
        _router_kernel,
        grid=(n_win,),
        in_specs=[pl.BlockSpec((tm, d), lambda i: (i, 0)), modspec(4), modspec(3),
                  _full((d, 128)), _full((d, 128)), _full((tm, tm))],
        out_specs=(pl.BlockSpec((tm, d), lambda i: (i, 0)),
                   pl.BlockSpec((tm, 128), lambda i: (i, 0)),
                   pl.BlockSpec((None, 8, tm), lambda i: (i, 0, 0)),
                   pl.BlockSpec((None, 8, 128), lambda i: (i, 0, 0)),
                   pl.BlockSpec((None, 8, 128), lambda i: (i, 0, 0))),
        out_shape=(jax.ShapeDtypeStruct((m, d), BF16),
                   jax.ShapeDtypeStruct((m, 128), F32),
                   jax.ShapeDtypeStruct((n_win, 8, tm), F32),
                   jax.ShapeDtypeStruct((n_win, 8, 128), F32),
                   jax.ShapeDtypeStruct((n_win, 8, 128), F32)),
        scratch_shapes=[pltpu.VMEM((8, 128), F32)],
        compiler_params=_params(("arbitrary",)),
        name="moe_router",
    )(x2, mod4, mod4, wr_hi, wr_lo, jnp.asarray(earlier, dtype=BF16))


def _expert_offset(e, start_ref):
    off = jnp.zeros_like(e)
    for k in range(N_EXPERTS):
        off = jnp.where(e == k, start_ref[k].astype(F32), off)
    return off


def _pair_flags(s, n, key_ref):
    last_idx = key_ref.shape[0] - 1
    key = key_ref[s]
    first = (s == 0) | (key != key_ref[jnp.maximum(s - 1, 0)])
    last = (s == n - 1) | (key != key_ref[jnp.minimum(s + 1, last_idx)])
    return s < n, first, last


def _ring_copy(src_ref, block_rows, idx_ref, buf_ref, sem_ref, step):
    slot = step % MOE_RING
    row0 = pl.multiple_of(idx_ref[step] * block_rows, block_rows)
    return pltpu.make_async_copy(src_ref.at[pl.ds(row0, block_rows)], buf_ref.at[slot], sem_ref.at[slot])


def _ring_start_ahead(src_ref, block_rows, idx_ref, buf_ref, sem_ref, s, n):
    ahead = MOE_RING - 1

    @pl.when(s == 0)
    def _():
        for k in range(ahead):
            @pl.when(k < n)
            def _():
                _ring_copy(src_ref, block_rows, idx_ref, buf_ref, sem_ref, k).start()

    @pl.when(s + ahead < n)
    def _():
        _ring_copy(src_ref, block_rows, idx_ref, buf_ref, sem_ref, s + ahead).start()


def _gather_kernel(pt_ref, pw_ref, n_ref, start_ref, h_any, row_ref, hs_ref, acc_ref, hbuf_ref, sem_ref):
    s = pl.program_id(0)
    n = n_ref[0]
    active, first, last = _pair_flags(s, n, pt_ref)
    _ring_start_ahead(h_any, MOE_WIN, pw_ref, hbuf_ref, sem_ref, s, n)

    @pl.when(active & first)
    def _():
        acc_ref[...] = jnp.zeros_like(acc_ref)

    @pl.when(active)
    def _():
        _ring_copy(h_any, MOE_WIN, pw_ref, hbuf_ref, sem_ref, s).wait()
        h_ref = hbuf_ref.at[s % MOE_RING]
        info = row_ref[pw_ref[s]]
        half = info.shape[1] // 2
        p1 = info[INFO_R1:INFO_R1 + 1] + _expert_offset(info[INFO_E1:INFO_E1 + 1], start_ref)
        p2 = info[INFO_R2:INFO_R2 + 1] + _expert_offset(info[INFO_E2:INFO_E2 + 1], start_ref)
        rows = (pt_ref[s] * MOE_TILE + lax.broadcasted_iota(jnp.int32, (MOE_TILE, half), 0)).astype(F32)
        part = None
        for c in range(2):
            sl = slice(c * half, (c + 1) * half)
            perm = jnp.where(rows == p1[:, sl], 1.0, jnp.where(rows == p2[:, sl], 1.0, 0.0)).astype(BF16)
            term = _dot(perm, h_ref[sl, :])
            part = term if part is None else part + term
        acc_ref[...] += part

    @pl.when(active & last)
    def _():
        hs_ref[...] = acc_ref[...].astype(hs_ref.dtype)


def _moe_ffn_kernel(te_ref, used_ref, hs_ref, wg_ref, wu_ref, wd_ref, ys_ref, *, tf):
    used = used_ref[pl.program_id(0)] > 0

    @pl.when(used)
    def _():
        h = hs_ref[...]
        y = None
        for f0 in range(0, wg_ref.shape[1], tf):
            a = _dot(h, wg_ref[:, f0:f0 + tf])
            u = _dot(h, wu_ref[:, f0:f0 + tf])
            act = (a * jax.nn.sigmoid(a)) * u
            t = _dot(act.astype(BF16), wd_ref[f0:f0 + tf, :])
            y = t if y is None else y + t
        ys_ref[...] = y.astype(ys_ref.dtype)

    @pl.when(jnp.logical_not(used))
    def _():
        ys_ref[...] = jnp.zeros_like(ys_ref)


def _combine_kernel(pw_ref, pt_ref, n_ref, start_ref, ys_any, col_ref, x_ref, g_ref, lng_ref, lnb_ref, o_ref,
                    acc_ref, pos_ref, ysbuf_ref, sem_ref, *, alpha):
    s = pl.program_id(0)
    n = n_ref[0]
    active, first, last = _pair_flags(s, n, pw_ref)
    _ring_start_ahead(ys_any, MOE_TILE, pt_ref, ysbuf_ref, sem_ref, s, n)

    @pl.when(active & first)
    def _():
        acc_ref[...] = jnp.zeros_like(acc_ref)
        info = col_ref[...]
        pos_ref[:, 0:1] = info[:, INFO_R1:INFO_R1 + 1] + _expert_offset(info[:, INFO_E1:INFO_E1 + 1], start_ref)
        pos_ref[:, 1:2] = info[:, INFO_R2:INFO_R2 + 1] + _expert_offset(info[:, INFO_E2:INFO_E2 + 1], start_ref)

    @pl.when(active)
    def _():
        win = col_ref.shape[0]
        p1 = pos_ref[:, 0:1]
        p2 = pos_ref[:, 1:2]
        row0 = (pt_ref[s] * MOE_TILE).astype(F32)
        in1 = (p1 >= row0) & (p1 < row0 + MOE_TILE)
        in2 = (p2 >= row0) & (p2 < row0 + MOE_TILE)
        weight = jnp.where(in1, col_ref[:, INFO_W1:INFO_W1 + 1], jnp.where(in2, col_ref[:, INFO_W2:INFO_W2 + 1], 0.0))
        cols = lax.broadcasted_iota(jnp.int32, (win, MOE_TILE), 1).astype(F32) + row0
        perm = jnp.where(cols == p1, 1.0, jnp.where(cols == p2, 1.0, 0.0)).astype(BF16)
        _ring_copy(ys_any, MOE_TILE, pt_ref, ysbuf_ref, sem_ref, s).wait()
        acc_ref[...] += weight * _dot(perm, ysbuf_ref[s % MOE_RING])

    @pl.when(active & last)
    def _():
        v = alpha * x_ref[...] + g_ref[...] * acc_ref[...]
        o_ref[...] = _ln(v) * lng_ref[...] + lnb_ref[...]


def _pair_list(mask, n_pairs):
    flat = mask.reshape(-1)
    n = jnp.sum(flat.astype(jnp.int32))
    idx = jnp.nonzero(flat, size=n_pairs, fill_value=0)[0].astype(jnp.int32)
    idx = jnp.where(jnp.arange(n_pairs) < n, idx, idx[jnp.maximum(n - 1, 0)])
    return idx // mask.shape[1], idx % mask.shape[1], n.reshape(1)


def _moe(x2, mod4, wr, wg, wu, wd, lng, lnb, seq, *, alpha, tf=1792):
    m, d = x2.shape
    n_exp, _, ff = wg.shape
    n_win = m // MOE_WIN
    per = seq // MOE_WIN
    n_rows = 2 * m + n_exp * MOE_FFN_TILE
    n_tiles = n_rows // MOE_TILE
    n_ffn = n_rows // MOE_FFN_TILE
    n_pairs = n_exp * n_win + n_tiles + 3 * n_exp

    h, col, row, cum, cnt = _router(x2, mod4, wr, seq)

    cum = cum[:, 0, :n_exp].astype(jnp.int32)
    cnt = cnt[:, 0, :n_exp].astype(jnp.int32)
    total = cum[-1] + cnt[-1]
    padded = (total + MOE_FFN_TILE - 1) // MOE_FFN_TILE * MOE_FFN_TILE
    end = jnp.cumsum(padded)
    start = (end - padded).astype(jnp.int32)

    def tile_expert(tile_rows):
        row0 = jnp.arange(n_rows // tile_rows, dtype=jnp.int32) * tile_rows
        te = jnp.minimum(jnp.sum((row0[:, None] >= end[None, :]).astype(jnp.int32), axis=1), n_exp - 1)
        return row0, te, row0 < end[-1]

    row0, te, used = tile_expert(MOE_TILE)
    local0 = row0 - start[te]
    lo = jnp.take(cum.T, te, axis=0)
    hi = lo + jnp.take(cnt.T, te, axis=0)
    meet = (lo < local0[:, None] + MOE_TILE) & (hi > local0[:, None]) & (hi > lo) & used[:, None]
    visit = meet.at[:, 0].set(meet[:, 0] | jnp.logical_not(jnp.any(meet, axis=1)))
    g_tile, g_win, g_n = _pair_list(visit, n_pairs)
    c_win, c_tile, c_n = _pair_list(meet.T, n_pairs)
    _, ffn_te, ffn_used = tile_expert(MOE_FFN_TILE)

    hs = pl.pallas_call(
        _gather_kernel,
        grid_spec=pltpu.PrefetchScalarGridSpec(
            num_scalar_prefetch=4,
            grid=(n_pairs,),
            in_specs=[pl.BlockSpec(memory_space=pl.ANY),
                      pl.BlockSpec((n_win, 8, MOE_WIN), lambda s, pt, pw, n, st: (0, 0, 0))],
            out_specs=pl.BlockSpec((MOE_TILE, d), lambda s, pt, pw, n, st: (pt[s], 0)),
            scratch_shapes=[pltpu.VMEM((MOE_TILE, d), F32), pltpu.VMEM((MOE_RING, MOE_WIN, d), BF16),
                            pltpu.SemaphoreType.DMA((MOE_RING,))]),
        out_shape=jax.ShapeDtypeStruct((n_rows, d), BF16),
        compiler_params=_params(("arbitrary",)),
        name="moe_gather",
    )(g_tile, g_win, g_n, start, h, row)

    expert = lambda shape: pl.BlockSpec((None,) + shape, lambda j, te, us: (te[j], 0, 0),
                                        pipeline_mode=pl.Buffered(1))
    ys = pl.pallas_call(
        functools.partial(_moe_ffn_kernel, tf=tf),
        grid_spec=pltpu.PrefetchScalarGridSpec(
            num_scalar_prefetch=2,
            grid=(n_ffn,),
            in_specs=[pl.BlockSpec((MOE_FFN_TILE, d), lambda j, te, us: (j, 0)),
                      expert((d, ff)), expert((d, ff)), expert((ff, d))],
            out_specs=pl.BlockSpec((MOE_FFN_TILE, d), lambda j, te, us: (j, 0))),
        out_shape=jax.ShapeDtypeStruct((n_rows, d), BF16),
        compiler_params=_params(("arbitrary",)),
        name="moe_experts",
    )(ffn_te, ffn_used.astype(jnp.int32), hs, wg, wu, wd)

    return pl.pallas_call(
        functools.partial(_combine_kernel, alpha=alpha),
        grid_spec=pltpu.PrefetchScalarGridSpec(
            num_scalar_prefetch=4,
            grid=(n_pairs,),
            in_specs=[pl.BlockSpec(memory_space=pl.ANY),
                      pl.BlockSpec((MOE_WIN, 128), lambda s, pw, pt, n, st: (pw[s], 0)),
                      pl.BlockSpec((MOE_WIN, d), lambda s, pw, pt, n, st: (pw[s], 0)),
                      pl.BlockSpec((None, None, 1, d), lambda s, pw, pt, n, st: (pw[s] // per, 5, 0, 0)),
                      pl.BlockSpec((1, d), lambda s, pw, pt, n, st: (0, 0)),
                      pl.BlockSpec((1, d), lambda s, pw, pt, n, st: (0, 0))],
            out_specs=pl.BlockSpec((MOE_WIN, d), lambda s, pw, pt, n, st: (pw[s], 0)),
            scratch_shapes=[pltpu.VMEM((MOE_WIN, d), F32), pltpu.VMEM((MOE_WIN, 128), F32),
                            pltpu.VMEM((MOE_RING, MOE_TILE, d), BF16), pltpu.SemaphoreType.DMA((MOE_RING,))]),
        out_shape=jax.ShapeDtypeStruct((m, d), F32),
        compiler_params=_params(("arbitrary",)),
        name="moe_combine",
    )(c_win, c_tile, c_n, start, ys, col, x2, mod4, lng.reshape(1, d), lnb.reshape(1, d))


def _in_columns():
    cols = np.full((N_COLS,), -1, np.int64)
    scale = np.ones((N_COLS,), np.float32)
    qs = LOG2E / math.sqrt(HEAD_DIM)

    def put(dst, src, width, s=1.0):
        cols[dst:dst + width] = np.arange(src, src + width)
        scale[dst:dst + width] = s

    put(COL_MG, IN_MG, COL_NSA_Q - COL_MG)
    put(COL_NSA_Q, IN_NSA_Q, 256, qs)
    put(COL_NSA_CMP, IN_K_CMP, 64)
    put(COL_NSA_CMP + 64, IN_V_CMP, 64)
    put(COL_NSA_SEL, IN_K_SEL, 64)
    put(COL_NSA_SEL + 64, IN_K_SEL, 64)
    put(COL_NSA_WIN, IN_K_WIN, 64)
    put(COL_NSA_WIN + 64, IN_K_WIN, 64)
    for j, h in enumerate(SWA_SLOT_HEADS):
        put(COL_SWA_Q + j * HEAD_DIM, IN_SWA_Q + h * HEAD_DIM, HEAD_DIM, qs)
    put(COL_SWA_K, IN_SWA_K, 128)
    put(COL_DIL_Q, IN_DIL_Q, 768, qs)
    put(COL_DIL_K, IN_DIL_K, 256)
    put(COL_SB_Q, IN_SB_Q, 256, qs)
    put(COL_SB_K, IN_SB_K, 256)
    return cols, scale


def _in_rows_t():
    rows = np.full((N_ROWS_T,), -1, np.int64)

    def put(dst, src, width):
        rows[dst:dst + width] = np.arange(src, src + width)

    put(ROW_NSA_V, IN_V_SEL, 64)
    put(ROW_NSA_V + 64, IN_V_WIN, 64)
    put(ROW_SWA_V, IN_SWA_V, 128)
    put(ROW_DIL_V, IN_DIL_V, 256)
    put(ROW_SB_V, IN_SB_V, 256)
    put(ROW_GATE, IN_GATE, GATE_COLS)
    return rows


def _select_columns(w, src):
    d = w.shape[0]
    parts = []
    i = 0
    while i < len(src):
        j = i + 1
        while j < len(src) and (src[j] == src[j - 1] + 1 if src[i] >= 0 else src[j] < 0):
            j += 1
        parts.append(w[:, src[i]:src[i] + (j - i)] if src[i] >= 0 else jnp.zeros((d, j - i), w.dtype))
        i = j
    return jnp.concatenate(parts, axis=1)


def _relayout_w_in(w_in):
    cols, scale = _in_columns()
    w = _select_columns(w_in, cols) * jnp.asarray(scale)
    wt = _select_columns(w_in, _in_rows_t())
    return w.astype(BF16), wt.T.astype(BF16)


def kernel(x, c, rel_bias, w_ada, b_ada, w_in, w_branch, w_out, cmp_pe, cmp_w1, cmp_w2, swa_sinks, ln_g, ln_b,
           ffn_w_gate, ffn_w_up, ffn_w_down, moe_router, moe_w_gate, moe_w_up, moe_w_down):
    bsz, seq, d = x.shape
    depth = w_ada.shape[0]
    alpha = (2 * depth) ** 0.25
    m = bsz * seq
    nq = seq // BLK

    nsa_heads = tuple(range(NSA_HEADS))
    bias_sel = _bias_tiles(rel_bias, _bucket_tiles(nq, seq, 1), nsa_heads)
    bias_win = _bias_tiles(rel_bias, _bucket_tiles(-(-(NSA_WINDOW - 1) // BLK) + 1, NSA_WINDOW - 1, 1), nsa_heads)
    bias_swa = _bias_tiles(rel_bias, _bucket_tiles(-(-(SWA_WINDOW - 1) // BLK) + 1, SWA_WINDOW - 1, 1),
                           tuple(NSA_HEADS + h for h in SWA_SLOT_HEADS))
    bias_dil = []
    for gi, (win, dil) in enumerate(DIL_PATTERNS):
        h0 = NSA_HEADS + SWA_HEADS + gi * DIL_HEADS_PER_GROUP
        bias_dil.append(_bias_tiles(rel_bias, _bucket_tiles(min(win // BLK + 1, nq), win, dil),
                                    tuple(range(h0, h0 + DIL_HEADS_PER_GROUP))))
    swa_rows = np.concatenate([np.arange(h * HEAD_DIM, (h + 1) * HEAD_DIM) for h in SWA_SLOT_HEADS])

    x2 = x.reshape(m, d)
    for layer in range(depth):
        mod4 = _ada(c, w_ada[layer], b_ada[layer]).reshape(bsz, 6, 1, d)
        w_nat, w_t = _relayout_w_in(w_in[layer])
        z2, zt = _inproj(x2, mod4, w_nat, w_t, seq, sc_idx=1, sh_idx=0)
        zt = zt.reshape(bsz, nq, N_ROWS_T, BLK)
        z = z2.reshape(bsz, seq, N_COLS)
        chunks = z[:, :, COL_NSA_CMP:COL_NSA_CMP + 2 * HEAD_DIM].reshape(bsz, seq // CMP_STRIDE,
                                                                        CMP_STRIDE * 2 * HEAD_DIM)
        kc, vct = _compress(chunks, cmp_pe[layer], cmp_w1[layer], cmp_w2[layer])
        o_a = _nsa(z, zt, kc, vct, bias_sel, bias_win)
        o_b = _swa(z, zt, bias_swa, swa_sinks[layer])
        o_c = _dilated(z, zt, bias_dil)
        o_d = _stick_breaking(z, zt)
        wb = w_branch[layer]
        wb = jnp.stack([wb[0], wb[1][swa_rows], wb[2], wb[3]]).astype(BF16)
        branches = [a.reshape(m, BRANCH_WIDTH) for a in (o_a, o_b, o_c, o_d)]
        x2 = _merge(branches, z2, wb, w_out[layer].astype(BF16), x2, mod4, ln_g[layer, 0], ln_b[layer, 0], seq,
                    alpha=alpha, g_idx=2)
        i = layer // 2
        if layer % 2 == 0:
            x2 = _ffn(x2, mod4, ffn_w_gate[i].astype(BF16), ffn_w_up[i].astype(BF16), ffn_w_down[i].astype(BF16),
                      ln_g[layer, 1], ln_b[layer, 1], seq, alpha=alpha)
        else:
            x2 = _moe(x2, mod4, moe_router[i], moe_w_gate[i].astype(BF16), moe_w_up[i].astype(BF16),
                      moe_w_down[i].astype(BF16), ln_g[layer, 1], ln_b[layer, 1], seq, alpha=alpha)
    return x2.reshape(bsz, seq, d)
```

```python
import functools
import math

import numpy as np
import jax
import jax.numpy as jnp
from jax import lax
from jax.experimental import pallas as pl
from jax.experimental.pallas import tpu as pltpu

F32 = jnp.float32
BF16 = jnp.bfloat16
HIGHEST = lax.Precision.HIGHEST

HEAD_DIM = 64
BLK = 128
N_SLOTS = 4
NSA_HEADS = 4
CMP_LEN = 32
CMP_STRIDE = 16
SEL_LEN = 64
N_SEL = 16
NSA_WINDOW = 512
SWA_HEADS = 4
SWA_WINDOW = 128
SWA_SLOT_HEADS = (0, 2, 1, 3)
DIL_PATTERNS = ((128, 1), (512, 4), (2048, 16))
DIL_HEADS_PER_GROUP = 4
SB_HEADS = 4
N_BRANCHES = 4
BRANCH_WIDTH = 4 * HEAD_DIM
NUM_BUCKETS = 32
MAX_DISTANCE = 2048
N_EXPERTS = 8
LN_EPS = 1e-5
NEG_INF = -1e30
TINY = 1e-30
FORCE_SCORE = 1e4
LOG2E = math.log2(math.e)

IN_NSA_Q, IN_K_CMP, IN_V_CMP, IN_K_SEL, IN_V_SEL, IN_K_WIN, IN_V_WIN, IN_GATE = 0, 256, 320, 384, 448, 512, 576, 640
IN_SWA_Q, IN_SWA_K, IN_SWA_V = 652, 908, 1036
IN_DIL_Q, IN_DIL_K, IN_DIL_V = 1164, 1932, 2188
IN_SB_Q, IN_SB_K, IN_SB_V = 2444, 2700, 2956
IN_MG = 3212
GATE_COLS = NSA_HEADS * 3

COL_MG = 0
COL_NSA_Q = 4096
COL_NSA_CMP = 4352
COL_NSA_SEL = 4480
COL_NSA_WIN = 4608
COL_NSA_GATE = 4736
COL_SWA_Q = 4864
COL_SWA_K = 5120
COL_DIL_Q = 5376
COL_DIL_K = 6144
COL_SB_Q = 6400
COL_SB_K = 6656
N_COLS = 6912
ROW_NSA_V = 0
ROW_SWA_V = 128
ROW_DIL_V = 256
ROW_SB_V = 512
ROW_GATE = 768
N_ROWS_T = 784

VMEM_LIMIT = 56 * 1024 * 1024
ATTN_BATCH = 4


def _params(sem):
    return pltpu.CompilerParams(dimension_semantics=sem, vmem_limit_bytes=VMEM_LIMIT)


def _ln(x):
    mu = jnp.mean(x, axis=-1, keepdims=True)
    xc = x - mu
    var = jnp.mean(xc * xc, axis=-1, keepdims=True)
    return xc * lax.rsqrt(var + LN_EPS)


def _dot_nt(a, b, precision=None):
    return lax.dot_general(a, b, (((1,), (1,)), ((), ())), preferred_element_type=F32, precision=precision)


def _dot(a, b, precision=None):
    return jnp.dot(a, b, preferred_element_type=F32, precision=precision)


def _full(shape):
    return pl.BlockSpec(shape, lambda *_: (0,) * len(shape))


def _ada_kernel(c_ref, w_ref, b_ref, o_ref):
    c = c_ref[...]
    s = c * jax.nn.sigmoid(c)
    o_ref[...] = _dot(s, w_ref[...], HIGHEST) + b_ref[...]


def _ada(c, w, b):
    bsz, d = c.shape
    n = w.shape[1]
    tn = 1024
    return pl.pallas_call(
        _ada_kernel,
        grid=(n // tn,),
        in_specs=[pl.BlockSpec((bsz, d), lambda j: (0, 0)),
                  pl.BlockSpec((d, tn), lambda j: (0, j)),
                  pl.BlockSpec((1, tn), lambda j: (0, j))],
        out_specs=pl.BlockSpec((bsz, tn), lambda j: (0, j)),
        out_shape=jax.ShapeDtypeStruct((bsz, n), F32),
        compiler_params=_params(("parallel",)),
        name="ada",
    )(c, w, b.reshape(1, n))


def _inproj_kernel(x_ref, sc_ref, sh_ref, w_ref, wt_ref, o_ref, ot_ref, *, tn):
    h = (_ln(x_ref[...]) * (1.0 + sc_ref[...]) + sh_ref[...]).astype(BF16)
    for n0 in range(0, w_ref.shape[1], tn):
        o_ref[:, n0:n0 + tn] = _dot(h, w_ref[:, n0:n0 + tn]).astype(o_ref.dtype)
    res = _dot_nt(wt_ref[...], h)
    for j in range(ot_ref.shape[0]):
        ot_ref[j] = res[:, j * BLK:(j + 1) * BLK].astype(ot_ref.dtype)


def _inproj(x2, mod4, w, wt, seq, *, sc_idx, sh_idx, tm=512, n_chunks=3):
    m, d = x2.shape
    n = w.shape[1]
    rows = wt.shape[0]
    per = seq // tm
    resident = lambda shape: pl.BlockSpec(shape, lambda i: (0,) * len(shape), pipeline_mode=pl.Buffered(1))
    return pl.pallas_call(
        functools.partial(_inproj_kernel, tn=n // n_chunks),
        grid=(m // tm,),
        in_specs=[pl.BlockSpec((tm, d), lambda i: (i, 0)),
                  pl.BlockSpec((None, None, 1, d), lambda i: (i // per, sc_idx, 0, 0)),
                  pl.BlockSpec((None, None, 1, d), lambda i: (i // per, sh_idx, 0, 0)),
                  resident(w.shape), resident(wt.shape)],
        out_specs=(pl.BlockSpec((tm, n), lambda i: (i, 0)),
                   pl.BlockSpec((tm // BLK, rows, BLK), lambda i: (i, 0, 0))),
        out_shape=(jax.ShapeDtypeStruct((m, n), BF16),
                   jax.ShapeDtypeStruct((m // BLK, rows, BLK), BF16)),
        compiler_params=_params(("parallel",)),
        name="inproj",
    )(x2, mod4, mod4, w, wt)


def _np_bucket(dist):
    dist = np.maximum(dist, 0)
    max_exact = NUM_BUCKETS // 2
    d_f = np.maximum(dist, 1).astype(np.float32)
    large = max_exact + (np.log(d_f / np.float32(max_exact)) / np.float32(math.log(MAX_DISTANCE / max_exact))
                         * np.float32(NUM_BUCKETS - max_exact)).astype(np.int32)
    large = np.minimum(large, NUM_BUCKETS - 1)
    return np.where(dist < max_exact, dist, large).astype(np.int32)


def _bucket_tiles(n_off, max_dist, dil):
    b = np.arange(BLK)[:, None]
    a = np.arange(BLK)[None, :]
    tiles = []
    for o in range(n_off):
        dist = o * BLK + a - b
        ok = (dist >= 0) & (dist <= max_dist) & (dist % dil == 0)
        tiles.append(np.where(ok, _np_bucket(dist), -1))
    return np.stack(tiles).astype(np.int32)


def _bias_kernel(tbl_ref, bk_ref, o_ref, *, heads):
    slot = pl.program_id(0)
    h = jnp.int32(heads[0])
    for j in range(1, len(heads)):
        h = jnp.where(slot == j, heads[j], h)
    b = bk_ref[...]
    out = jnp.where(b < 0, NEG_INF, 0.0).astype(F32)
    for k in range(NUM_BUCKETS):
        out = jnp.where(b == k, tbl_ref[k, h] * LOG2E, out)
    o_ref[...] = out


def _bias_tiles(rel_bias, buckets, heads):
    n_off = buckets.shape[0]
    return pl.pallas_call(
        functools.partial(_bias_kernel, heads=tuple(heads)),
        grid=(len(heads),),
        in_specs=[pl.BlockSpec(memory_space=pltpu.SMEM), _full((n_off, BLK, BLK))],
        out_specs=pl.BlockSpec((n_off, BLK, BLK), lambda s: (0, 0, s)),
        out_shape=jax.ShapeDtypeStruct((n_off, BLK, len(heads) * BLK), F32),
        compiler_params=_params(("parallel",)),
        name="bias_tiles",
    )(rel_bias, jnp.asarray(buckets))


def _slot_queries(q, kw):
    lane = lax.broadcasted_iota(jnp.int32, (BLK, kw), 1)
    zero = jnp.zeros((BLK, kw), q.dtype)
    parts = []
    for j in range(N_SLOTS):
        lo = j * HEAD_DIM
        grp = q[:, (lo // kw) * kw:(lo // kw + 1) * kw]
        inside = (lane >= lo % kw) & (lane < lo % kw + HEAD_DIM)
        parts.append(jnp.where(inside, grp, zero))
    return jnp.concatenate(parts, axis=0)


def _softmax_step(s, vt, state, slot_rows):
    m, l = state[0], state[1]
    m_new = jnp.maximum(m, jnp.max(s, axis=0, keepdims=True))
    alpha = jnp.exp2(m - m_new)
    p = jnp.exp2(s - m_new)
    l = alpha * l + jnp.sum(p, axis=0, keepdims=True)
    yield
    res = _slot_values(vt, p.astype(BF16), slot_rows)
    yield
    accs = [alpha[:, j * BLK:(j + 1) * BLK] * state[2 + j] + res[j] for j in range(N_SLOTS)]
    return (m_new, l, *accs)


def _softmax_init():
    return (jnp.full((1, N_SLOTS * BLK), NEG_INF, F32), jnp.zeros((1, N_SLOTS * BLK), F32),
            *[jnp.zeros((HEAD_DIM, BLK), F32) for _ in range(N_SLOTS)])


def _store_heads(o_ref, outs):
    o_ref[...] = jnp.transpose(jnp.concatenate(outs, axis=0)).astype(o_ref.dtype)


def _key_rows(k_ref, kb):
    return k_ref[pl.ds(pl.multiple_of(kb * BLK, BLK), BLK), :]


def _slot_values(vt, p, slot_rows):
    if vt.shape[0] > 2 * HEAD_DIM:
        return [_dot(vt[slot_rows[j]:slot_rows[j] + HEAD_DIM, :], p[:, j * BLK:(j + 1) * BLK])
                for j in range(N_SLOTS)]
    res = _dot(vt, p)
    return [res[slot_rows[j]:slot_rows[j] + HEAD_DIM, j * BLK:(j + 1) * BLK] for j in range(N_SLOTS)]


def _window_softmax(qi, wq, k_ref, vt_of, bias_ref, n_off, slot_rows):
    tiles = []
    m = None
    for o in range(n_off):
        kb = qi - o
        kbc = jnp.maximum(kb, 0)
        s = _dot_nt(_key_rows(k_ref, kbc), wq) + bias_ref[o]
        if o > 0:
            s = s + jnp.where(kb >= 0, 0.0, NEG_INF)
        tiles.append((kbc, s))
        mo = jnp.max(s, axis=0, keepdims=True)
        m = mo if m is None else jnp.maximum(m, mo)
    l = None
    res = None
    for kbc, s in tiles:
        p = jnp.exp2(s - m)
        lo = jnp.sum(p, axis=0, keepdims=True)
        ro = _slot_values(vt_of(kbc), p.astype(BF16), slot_rows)
        l = lo if l is None else l + lo
        res = ro if res is None else [a + b for a, b in zip(res, ro)]
    return (m, l, *res)


def _run_interleaved(gens):
    results = [None] * len(gens)
    live = list(range(len(gens)))
    while live:
        for i in tuple(live):
            try:
                next(gens[i])
            except StopIteration as done:
                results[i] = done.value
                live.remove(i)
    return results


def _chained_loop(lo, hi, steps, states):
    def advance(k, sts):
        return tuple(_run_interleaved([f(k, st) for f, st in zip(steps, sts)]))

    def body(i, sts):
        k = lo + 2 * i
        return advance(k + 1, advance(k, sts))

    pairs = (hi - lo) // 2
    states = lax.fori_loop(0, pairs, body, tuple(states))
    k_last = lo + 2 * pairs
    return lax.cond(k_last < hi, lambda sts: advance(k_last, sts), lambda sts: sts, states)


def _compress_kernel(ch_ref, wa_ref, wb_ref, pea_ref, peb_ref, w2k_ref, w2vt_ref, kc_ref, vct_ref):
    ch = ch_ref[...].astype(F32)
    wa = wa_ref[...]
    wb = wb_ref[...]
    first = _dot(ch, wa, HIGHEST)
    second = _dot(ch, wb, HIGHEST)
    pe = (_dot(pea_ref[...], wa, HIGHEST) + _dot(peb_ref[...], wb, HIGHEST))[0:1]
    n = second.shape[0]
    nxt = pltpu.roll(second, n - 1, 0)
    hid = jax.nn.gelu(first + nxt + pe)
    kc_ref[...] = _dot(hid, w2k_ref[...], HIGHEST).astype(kc_ref.dtype)
    vct_ref[...] = _dot_nt(w2vt_ref[...], hid, HIGHEST).astype(vct_ref.dtype)


def _compress(chunks, cmp_pe, cmp_w1, cmp_w2):
    bsz, n_chunk, width = chunks.shape
    per = CMP_LEN // 2
    hd = HEAD_DIM
    w1 = cmp_w1.reshape(2, CMP_LEN, hd, hd)
    wexp = jnp.einsum('cpdn,ce->pcden', w1, jnp.eye(2, dtype=F32)).reshape(CMP_LEN, 2 * hd, 2 * hd)
    wa = wexp[:per].reshape(width, 2 * hd)
    wb = wexp[per:].reshape(width, 2 * hd)
    pe = jnp.transpose(cmp_pe, (1, 0, 2)).reshape(CMP_LEN, 2 * hd)
    pea = jnp.broadcast_to(pe[:per].reshape(1, width), (8, width))
    peb = jnp.broadcast_to(pe[per:].reshape(1, width), (8, width))
    zero = jnp.zeros((hd, hd), F32)
    w2k = jnp.concatenate([jnp.concatenate([cmp_w2[0], cmp_w2[0]], 1), jnp.concatenate([zero, zero], 1)], 0)
    w2vt = jnp.concatenate([zero, cmp_w2[1].T], 1)
    return pl.pallas_call(
        _compress_kernel,
        grid=(bsz,),
        in_specs=[pl.BlockSpec((None, n_chunk, width), lambda b: (b, 0, 0)),
                  _full(wa.shape), _full(wb.shape), _full(pea.shape), _full(peb.shape), _full(w2k.shape),
                  _full(w2vt.shape)],
        out_specs=(pl.BlockSpec((None, n_chunk, 2 * hd), lambda b: (b, 0, 0)),
                   pl.BlockSpec((None, hd, n_chunk), lambda b: (b, 0, 0))),
        out_shape=(jax.ShapeDtypeStruct((bsz, n_chunk, 2 * hd), BF16),
                   jax.ShapeDtypeStruct((bsz, hd, n_chunk), BF16)),
        compiler_params=_params(("parallel",)),
        name="nsa_compress",
    )(chunks, wa, wb, pea, peb, w2k, w2vt)


def _nsa_kernel(q_ref, kc_ref, vct_ref, ovt_ref, ksel_ref, kwin_ref, vt_ref, gate_ref, bsel_ref, bwin_ref, et_ref,
                o_ref, *, n_blk, n_win_off):
    qi = pl.program_id(1)
    t0 = qi * BLK
    nb = q_ref.shape[0]
    n_cmp = kc_ref.shape[1]
    slot_rows = (0,) * N_SLOTS
    shape = (n_cmp, N_SLOTS * BLK)
    tq = t0 + lax.broadcasted_iota(jnp.int32, shape, 1) % BLK
    ci = lax.broadcasted_iota(jnp.int32, shape, 0)
    ok = (ci * CMP_STRIDE + (CMP_LEN - 1)) <= tq
    jb = lax.broadcasted_iota(jnp.int32, (n_blk, BLK), 0)
    cur = (t0 + lax.broadcasted_iota(jnp.int32, (n_blk, BLK), 1)) // SEL_LEN
    forced = (jb == 0) | (jb == cur) | (jb == cur - 1)

    o_cmps, o_wins, sel_steps = [], [], []
    for bb in range(nb):
        wq = _slot_queries(q_ref[bb], 2 * HEAD_DIM)

        s = jnp.where(ok, _dot_nt(kc_ref[bb], wq), NEG_INF)
        m = jnp.max(s, axis=0, keepdims=True)
        p = jnp.where(ok, jnp.exp2(s - m), 0.0)
        p = p / jnp.maximum(jnp.sum(p, axis=0, keepdims=True), TINY)
        o_cmps.append(_dot(vct_ref[bb], p.astype(BF16)))
        psum = p[:, 0:BLK]
        for j in range(1, N_SLOTS):
            psum = psum + p[:, j * BLK:(j + 1) * BLK]

        hi = psum.astype(BF16)
        lo = (psum - hi.astype(F32)).astype(BF16)
        score = _dot(ovt_ref[...], hi) + _dot(ovt_ref[...], lo)
        score = jnp.where(forced, FORCE_SCORE, jnp.where(jb > cur, NEG_INF, score))
        rank = jnp.zeros((n_blk, BLK), F32)
        for c in range(n_blk):
            row = score[c:c + 1, :]
            beats = (row > score) | ((row == score) & (jb > c))
            rank = rank + jnp.where(beats, 1.0, 0.0)
        sel = jnp.where(rank < N_SEL, 1.0, 0.0).astype(BF16)

        st = _window_softmax(qi, wq, kwin_ref.at[bb], lambda kb, bb=bb: vt_ref[bb, kb, HEAD_DIM:2 * HEAD_DIM, :],
                             bwin_ref, n_win_off, slot_rows)
        o_wins.append([st[2 + j] / st[1][:, j * BLK:(j + 1) * BLK] for j in range(N_SLOTS)])

        def sel_step(kb, state, bb=bb, wq=wq, sel=sel):
            pen = (_dot(et_ref[kb], sel) - 1.0) * (-NEG_INF)
            pen = jnp.concatenate([pen] * N_SLOTS, axis=1)
            sc = _dot_nt(_key_rows(ksel_ref.at[bb], kb), wq) + bsel_ref[qi - kb] + pen
            yield
            return (yield from _softmax_step(sc, vt_ref[bb, kb, 0:HEAD_DIM, :], state, slot_rows))

        sel_steps.append(sel_step)

    sel_states = _run_interleaved([f(qi, _softmax_init()) for f in sel_steps])
    sel_states = _chained_loop(0, qi, sel_steps, sel_states)
    for bb in range(nb):
        st = sel_states[bb]
        gate = jax.nn.sigmoid(gate_ref[bb].astype(F32))
        outs = []
        for j in range(N_SLOTS):
            o_sel = st[2 + j] / st[1][:, j * BLK:(j + 1) * BLK]
            outs.append(gate[3 * j:3 * j + 1] * o_cmps[bb][:, j * BLK:(j + 1) * BLK]
                        + gate[3 * j + 1:3 * j + 2] * o_sel + gate[3 * j + 2:3 * j + 3] * o_wins[bb][j])
        _store_heads(o_ref.at[bb], outs)


def _nsa(z, zt, kc, vct, bias_sel, bias_win, nb=2 * ATTN_BATCH):
    bsz, seq, _ = z.shape
    nq = seq // BLK
    n_cmp = kc.shape[1]
    n_blk = seq // SEL_LEN
    assert n_blk >= N_SEL
    ci = np.arange(n_cmp)[None, :]
    sj = np.arange(n_blk)[:, None]
    overlap_t = ((ci * CMP_STRIDE + CMP_LEN - 1 >= sj * SEL_LEN) & (ci * CMP_STRIDE < (sj + 1) * SEL_LEN)
                 & (ci < n_cmp - 1)).astype(np.float32)
    key_blk = (np.arange(seq) // SEL_LEN).reshape(nq, BLK, 1)
    expand_t = (key_blk == np.arange(n_blk)[None, None, :]).astype(np.float32)
    qw = NSA_HEADS * HEAD_DIM
    kw = 2 * HEAD_DIM
    return pl.pallas_call(
        functools.partial(_nsa_kernel, n_blk=n_blk, n_win_off=bias_win.shape[0]),
        grid=(bsz // nb, nq),
        in_specs=[pl.BlockSpec((nb, BLK, qw), lambda b, i: (b, i, COL_NSA_Q // qw)),
                  pl.BlockSpec((nb, n_cmp, kw), lambda b, i: (b, 0, 0)),
                  pl.BlockSpec((nb, HEAD_DIM, n_cmp), lambda b, i: (b, 0, 0)),
                  _full(overlap_t.shape),
                  pl.BlockSpec((nb, seq, kw), lambda b, i: (b, 0, COL_NSA_SEL // kw)),
                  pl.BlockSpec((nb, seq, kw), lambda b, i: (b, 0, COL_NSA_WIN // kw)),
                  pl.BlockSpec((nb, nq, kw, BLK), lambda b, i: (b, 0, ROW_NSA_V // kw, 0)),
                  pl.BlockSpec((nb, None, 16, BLK), lambda b, i: (b, i, ROW_GATE // 16, 0)),
                  _full(bias_sel.shape), _full(bias_win.shape), _full(expand_t.shape)],
        out_specs=pl.BlockSpec((nb, BLK, qw), lambda b, i: (b, i, 0)),
        out_shape=jax.ShapeDtypeStruct((bsz, seq, qw), BF16),
        compiler_params=_params(("parallel", "parallel")),
        name="nsa_mixer",
    )(z, kc, vct, jnp.asarray(overlap_t, dtype=BF16), z, z, zt, zt, bias_sel, bias_win,
      jnp.asarray(expand_t, dtype=BF16))


def _swa_kernel(q_ref, k_ref, vt_ref, bias_ref, sink_ref, o_ref, *, n_off):
    qi = pl.program_id(1)
    slot_rows = tuple((j % 2) * HEAD_DIM for j in range(N_SLOTS))
    for bb in range(q_ref.shape[0]):
        wq = _slot_queries(q_ref[bb], 2 * HEAD_DIM)
        st = _window_softmax(qi, wq, k_ref.at[bb], lambda kb, bb=bb: vt_ref[bb, kb], bias_ref, n_off, slot_rows)
        m, l = st[0], st[1]
        outs = []
        for j in range(N_SLOTS):
            sink = sink_ref[SWA_SLOT_HEADS[j]] * LOG2E
            mj = m[:, j * BLK:(j + 1) * BLK]
            lj = l[:, j * BLK:(j + 1) * BLK]
            m2 = jnp.maximum(mj, sink)
            scale = jnp.exp2(mj - m2)
            den = lj * scale + jnp.exp2(sink - m2)
            outs.append(st[2 + j] * (scale / den))
        _store_heads(o_ref.at[bb], outs)


def _swa(z, zt, bias, sinks, nb=ATTN_BATCH):
    bsz, seq, _ = z.shape
    nq = seq // BLK
    qw = SWA_HEADS * HEAD_DIM
    kw = 2 * HEAD_DIM
    return pl.pallas_call(
        functools.partial(_swa_kernel, n_off=bias.shape[0]),
        grid=(bsz // nb, nq),
        in_specs=[pl.BlockSpec((nb, BLK, qw), lambda b, i: (b, i, COL_SWA_Q // qw)),
                  pl.BlockSpec((nb, seq, kw), lambda b, i: (b, 0, COL_SWA_K // kw)),
                  pl.BlockSpec((nb, nq, kw, BLK), lambda b, i: (b, 0, ROW_SWA_V // kw, 0)),
                  _full(bias.shape),
                  pl.BlockSpec(memory_space=pltpu.SMEM)],
        out_specs=pl.BlockSpec((nb, BLK, qw), lambda b, i: (b, i, 0)),
        out_shape=jax.ShapeDtypeStruct((bsz, seq, qw), BF16),
        compiler_params=_params(("parallel", "parallel")),
        name="swa_mixer",
    )(z, z, zt, bias, sinks)


def _dil_kernel(q0_ref, q1_ref, q2_ref, k_ref, vt_ref, b0_ref, b1_ref, b2_ref, o_ref):
    qi = pl.program_id(1)
    nb = q0_ref.shape[0]
    slot_rows = tuple(j * HEAD_DIM for j in range(N_SLOTS))
    kw = N_SLOTS * HEAD_DIM
    short, far_steps = [], []
    for bb in range(nb):
        vt_of = lambda kb, bb=bb: vt_ref[bb, kb]
        short.append([_window_softmax(qi, _slot_queries(q_ref[bb], kw), k_ref.at[bb], vt_of, bias_ref,
                                      bias_ref.shape[0], slot_rows)
                      for q_ref, bias_ref in ((q0_ref, b0_ref), (q1_ref, b1_ref))])
        wq2 = _slot_queries(q2_ref[bb], kw)

        def far_step(kb, state, bb=bb, wq2=wq2, vt_of=vt_of):
            sc = _dot_nt(_key_rows(k_ref.at[bb], kb), wq2) + b2_ref[qi - kb]
            yield
            return (yield from _softmax_step(sc, vt_of(kb), state, slot_rows))

        far_steps.append(far_step)
    far_states = _run_interleaved([f(qi, _softmax_init()) for f in far_steps])
    far_states = _chained_loop(jnp.maximum(qi - (b2_ref.shape[0] - 1), 0), qi, far_steps, far_states)
    for bb in range(nb):
        groups = short[bb] + [far_states[bb]]
        lses = [st[0] + jnp.log(st[1]) * LOG2E for st in groups]
        top = jnp.maximum(jnp.maximum(lses[0], lses[1]), lses[2])
        es = [jnp.exp2(v - top) for v in lses]
        den = es[0] + es[1] + es[2]
        outs = []
        for j in range(N_SLOTS):
            sl = slice(j * BLK, (j + 1) * BLK)
            acc = jnp.zeros((HEAD_DIM, BLK), F32)
            for st, e in zip(groups, es):
                acc = acc + (e[:, sl] / den[:, sl]) * (st[2 + j] / st[1][:, sl])
            outs.append(acc)
        _store_heads(o_ref.at[bb], outs)


def _dilated(z, zt, biases, nb=ATTN_BATCH):
    bsz, seq, _ = z.shape
    nq = seq // BLK
    qw = DIL_HEADS_PER_GROUP * HEAD_DIM
    qspec = lambda g: pl.BlockSpec((nb, BLK, qw), lambda b, i: (b, i, COL_DIL_Q // qw + g))
    return pl.pallas_call(
        _dil_kernel,
        grid=(bsz // nb, nq),
        in_specs=[qspec(0), qspec(1), qspec(2),
                  pl.BlockSpec((nb, seq, qw), lambda b, i: (b, 0, COL_DIL_K // qw)),
                  pl.BlockSpec((nb, nq, qw, BLK), lambda b, i: (b, 0, ROW_DIL_V // qw, 0)),
                  _full(biases[0].shape), _full(biases[1].shape), _full(biases[2].shape)],
        out_specs=pl.BlockSpec((nb, BLK, qw), lambda b, i: (b, i, 0)),
        out_shape=jax.ShapeDtypeStruct((bsz, seq, qw), BF16),
        compiler_params=_params(("parallel", "parallel")),
        name="dilated_mixer",
    )(z, z, z, z, zt, *biases)


def _sb_kernel(q_ref, k_ref, vt_ref, u_ref, o_ref):
    qi = pl.program_id(1)
    nb = q_ref.shape[0]
    u = u_ref[...]
    slot_rows = tuple(j * HEAD_DIM for j in range(N_SLOTS))
    shape = (BLK, N_SLOTS * BLK)
    strict =lax.broadcasted_iota(jnp.int32, shape, 0) < lax.broadcasted_iota(jnp.int32, shape, 1) % BLK

    def step(bb, wq, kb, state, diag):
        run = state[0]
        zz = _dot_nt(_key_rows(k_ref.at[bb], kb), wq)
        yield
        ls = jnp.minimum(zz, 0.0) - jnp.log(1.0 + jnp.exp2(-jnp.abs(zz))) * LOG2E
        lf = ls - zz
        if diag:
            lf = jnp.where(strict, lf, 0.0)
        hi = lf.astype(BF16)
        lo = (lf - hi.astype(F32)).astype(BF16)
        after = _dot(u, jnp.concatenate([hi, lo], axis=0)) + run
        yield
        a = jnp.exp2(ls + after)
        if diag:
            a = jnp.where(strict, a, 0.0)
        res = _slot_values(vt_ref[bb, kb], a.astype(BF16), slot_rows)
        yield
        accs = [state[1 + j] + res[j] for j in range(N_SLOTS)]
        return (run + jnp.sum(lf, axis=0, keepdims=True), *accs)

    init = (jnp.zeros((1, N_SLOTS * BLK), F32), *[jnp.zeros((HEAD_DIM, BLK), F32) for _ in range(N_SLOTS)])
    wqs = [_slot_queries(q_ref[bb], SB_HEADS * HEAD_DIM) for bb in range(nb)]
    steps = [lambda t, st, bb=bb: step(bb, wqs[bb], qi - 1 - t, st, False) for bb in range(nb)]
    states = _run_interleaved([step(bb, wqs[bb], qi, init, True) for bb in range(nb)])
    states = _chained_loop(0, qi, steps, states)
    for bb in range(nb):
        _store_heads(o_ref.at[bb], list(states[bb][1:]))


def _stick_breaking(z, zt, nb=2 * ATTN_BATCH):
    bsz, seq, _ = z.shape
    nq = seq // BLK
    w = SB_HEADS * HEAD_DIM
    later = (np.arange(BLK)[None, :] > np.arange(BLK)[:, None]).astype(np.float32)
    later = np.concatenate([later, later], axis=1)
    return pl.pallas_call(
        _sb_kernel,
        grid=(bsz // nb, nq),
        in_specs=[pl.BlockSpec((nb, BLK, w), lambda b, i: (b, i, COL_SB_Q // w)),
                  pl.BlockSpec((nb, seq, w), lambda b, i: (b, 0, COL_SB_K // w)),
                  pl.BlockSpec((nb, nq, w, BLK), lambda b, i: (b, 0, ROW_SB_V // w, 0)),
                  _full((BLK, 2 * BLK))],
        out_specs=pl.BlockSpec((nb, BLK, w), lambda b, i: (b, i, 0)),
        out_shape=jax.ShapeDtypeStruct((bsz, seq, w), BF16),
        compiler_params=_params(("parallel", "parallel")),
        name="stick_breaking",
    )(z, z, zt, jnp.asarray(later, dtype=BF16))


def _merge_kernel(oa_ref, ob_ref, oc_ref, od_ref, mg_ref, wb_ref, wo_ref, x_ref, g_ref, lng_ref, lnb_ref, o_ref,
                  *, alpha):
    d = x_ref.shape[1]
    merged = jnp.zeros((x_ref.shape[0], d), F32)
    for b, ref in enumerate((oa_ref, ob_ref, oc_ref, od_ref)):
        proj = _dot(ref[...], wb_ref[b])
        gate = 0.5 * jnp.tanh(0.5 * mg_ref[:, b * d:(b + 1) * d].astype(F32)) + 0.5
        merged = merged + gate * proj
    y = _dot(merged.astype(BF16), wo_ref[...])
    v = alpha * x_ref[...] + g_ref[...] * y
    o_ref[...] = _ln(v) * lng_ref[...] + lnb_ref[...]


def _merge(branches, z2, wb, wo, x2, mod4, lng, lnb, seq, *, alpha, g_idx, tm=512):
    m, d = x2.shape
    bw = BRANCH_WIDTH
    per = seq // tm
    row = lambda w: pl.BlockSpec((tm, w), lambda i: (i, 0))
    return pl.pallas_call(
        functools.partial(_merge_kernel, alpha=alpha),
        grid=(m // tm,),
        in_specs=[row(bw), row(bw), row(bw), row(bw),
                  pl.BlockSpec((tm, N_BRANCHES * d), lambda i: (i, COL_MG // (N_BRANCHES * d))),
                  _full(wb.shape), _full(wo.shape),
                  row(d),
                  pl.BlockSpec((None, None, 1, d), lambda i: (i // per, g_idx, 0, 0)),
                  _full((1, d)), _full((1, d))],
        out_specs=row(d),
        out_shape=jax.ShapeDtypeStruct((m, d), F32),
        compiler_params=_params(("parallel",)),
        name="merge_out",
    )(*branches, z2, wb, wo, x2, mod4, lng.reshape(1, d), lnb.reshape(1, d))


def _ffn_kernel(x_ref, sc_ref, sh_ref, g_ref, lng_ref, lnb_ref, wg_ref, wu_ref, wd_ref, o_ref, *, alpha, tf):
    x = x_ref[...]
    h = (_ln(x) * (1.0 + sc_ref[...]) + sh_ref[...]).astype(BF16)
    y = None
    for f0 in range(0, wg_ref.shape[1], tf):
        a = _dot(h, wg_ref[:, f0:f0 + tf])
        u = _dot(h, wu_ref[:, f0:f0 + tf])
        act = (a * jax.nn.sigmoid(a)) * u
        t = _dot(act.astype(BF16), wd_ref[f0:f0 + tf, :])
        y = t if y is None else y + t
    v = alpha * x + g_ref[...] * y
    o_ref[...] = _ln(v) * lng_ref[...] + lnb_ref[...]


def _ffn(x2, mod4, wg, wu, wd, lng, lnb, seq, *, alpha, tm=512, tf=1408):
    m, d = x2.shape
    per = seq // tm
    modspec = lambda k: pl.BlockSpec((None, None, 1, d), lambda i: (i // per, k, 0, 0))
    resident = lambda shape: pl.BlockSpec(shape, lambda i: (0,) * len(shape), pipeline_mode=pl.Buffered(1))
    return pl.pallas_call(
        functools.partial(_ffn_kernel, alpha=alpha, tf=tf),
        grid=(m // tm,),
        in_specs=[pl.BlockSpec((tm, d), lambda i: (i, 0)),
                  modspec(4), modspec(3), modspec(5),
                  _full((1, d)), _full((1, d)),
                  resident(wg.shape), resident(wu.shape), resident(wd.shape)],
        out_specs=pl.BlockSpec((tm, d), lambda i: (i, 0)),
        out_shape=jax.ShapeDtypeStruct((m, d), F32),
        compiler_params=_params(("parallel",)),
        name="ffn_dense",
    )(x2, mod4, mod4, mod4, lng.reshape(1, d), lnb.reshape(1, d), wg, wu, wd)


MOE_WIN = 512
MOE_TILE = 256
MOE_FFN_TILE = 512
MOE_RING = 4
INFO_E1, INFO_E2, INFO_W1, INFO_W2, INFO_R1, INFO_R2 = range(6)


def _router_kernel(x_ref, sc_ref, sh_ref, wrh_ref, wrl_ref, ls_ref, h_ref, col_ref, row_ref, cum_ref, cnt_ref,
                   carry_ref):
    @pl.when(pl.program_id(0) == 0)
    def _():
        carry_ref[...] = jnp.zeros_like(carry_ref)

    tm = x_ref.shape[0]
    lane = lax.broadcasted_iota(jnp.int32, (tm, 128), 1).astype(F32)
    h32 = _ln(x_ref[...]) * (1.0 + sc_ref[...]) + sh_ref[...]
    h_hi = h32.astype(BF16)
    h_ref[...] = h_hi
    h_lo = (h32 - h_hi.astype(F32)).astype(BF16)
    logits = _dot(h_hi, wrh_ref[...]) + (_dot(h_lo, wrh_ref[...]) + _dot(h_hi, wrl_ref[...]))
    logits = jnp.where(lane < N_EXPERTS, logits, -jnp.inf)
    t1 = jnp.max(logits, axis=1, keepdims=True)
    i1 = jnp.min(jnp.where(logits == t1, lane, 128.0), axis=1, keepdims=True)
    rest = jnp.where(lane == i1, -jnp.inf, logits)
    t2 = jnp.max(rest, axis=1, keepdims=True)
    i2 = jnp.min(jnp.where(rest == t2, lane, 128.0), axis=1, keepdims=True)
    e2 = jnp.exp(t2 - t1)
    den = 1.0 + e2
    w1 = 1.0 / den
    w2 = e2 / den
    onehot = jnp.where(lane == i1, 1.0, jnp.where(lane == i2, 1.0, 0.0))
    before = _dot(ls_ref[...], onehot.astype(BF16)) + carry_ref[0:1, :]
    r1 = jnp.sum(jnp.where(lane == i1, before, 0.0), axis=1, keepdims=True)
    r2 = jnp.sum(jnp.where(lane == i2, before, 0.0), axis=1, keepdims=True)
    info = jnp.zeros((tm, 128), F32)
    for k, v in ((INFO_E1, i1), (INFO_E2, i2), (INFO_W1, w1), (INFO_W2, w2), (INFO_R1, r1), (INFO_R2, r2)):
        info = jnp.where(lane == k, v, info)
    col_ref[...] = info
    row_ref[...] = jnp.transpose(info)[0:8, :]
    cnt = jnp.sum(onehot, axis=0, keepdims=True)
    cum_ref[...] = carry_ref[...]
    cnt_ref[...] = jnp.broadcast_to(cnt, cnt_ref.shape)
    carry_ref[...] = carry_ref[...] + cnt


def _router(x2, mod4, wr, seq):
    m, d = x2.shape
    tm = MOE_WIN
    per = seq // tm
    n_win = m // tm
    wr_pad = jnp.pad(wr, ((0, 0), (0, 128 - wr.shape[1])))
    wr_hi = wr_pad.astype(BF16)
    wr_lo = (wr_pad - wr_hi.astype(F32)).astype(BF16)
    earlier =(np.arange(tm)[:, None] > np.arange(tm)[None, :]).astype(np.float32)
    modspec = lambda k: pl.BlockSpec((None, None, 1, d), lambda i: (i // per, k, 0, 0))
    return pl.pallas_call(
        _router_kernel,
        grid=(n_win,),
        in_specs=[pl.BlockSpec((tm, d), lambda i: (i, 0)), modspec(4), modspec(3),
                  _full((d, 128)), _full((d, 128)), _full((tm, tm))],
        out_specs=(pl.BlockSpec((tm, d), lambda i: (i, 0)),
                   pl.BlockSpec((tm, 128), lambda i: (i, 0)),
                   pl.BlockSpec((None, 8, tm), lambda i: (i, 0, 0)),
                   pl.BlockSpec((None, 8, 128), lambda i: (i, 0, 0)),
                   pl.BlockSpec((None, 8, 128), lambda i: (i, 0, 0))),
        out_shape=(jax.ShapeDtypeStruct((m, d), BF16),
                   jax.ShapeDtypeStruct((m, 128), F32),
                   jax.ShapeDtypeStruct((n_win, 8, tm), F32),
                   jax.ShapeDtypeStruct((n_win, 8, 128), F32),
                   jax.ShapeDtypeStruct((n_win, 8, 128), F32)),
        scratch_shapes=[pltpu.VMEM((8, 128), F32)],
        compiler_params=_params(("arbitrary",)),
        name="moe_router",
    )(x2, mod4, mod4, wr_hi, wr_lo, jnp.asarray(earlier, dtype=BF16))


def _expert_offset(e, start_ref):
    off = jnp.zeros_like(e)
    for k in range(N_EXPERTS):
        off = jnp.where(e == k, start_ref[k].astype(F32), off)
    return off


def _pair_flags(s, n, key_ref):
    last_idx = key_ref.shape[0] - 1
    key = key_ref[s]
    first = (s == 0) | (key != key_ref[jnp.maximum(s - 1, 0)])
    last = (s == n - 1) | (key != key_ref[jnp.minimum(s + 1, last_idx)])
    return s < n, first, last


def _ring_copy(src_ref, block_rows, idx_ref, buf_ref, sem_ref, step):
    slot = step % MOE_RING
    row0 = pl.multiple_of(idx_ref[step] * block_rows, block_rows)
    return pltpu.make_async_copy(src_ref.at[pl.ds(row0, block_rows)], buf_ref.at[slot], sem_ref.at[slot])


def _ring_start_ahead(src_ref, block_rows, idx_ref, buf_ref, sem_ref, s, n):
    ahead = MOE_RING - 1

    @pl.when(s == 0)
    def _():
        for k in range(ahead):
            @pl.when(k < n)
            def _():
                _ring_copy(src_ref, block_rows, idx_ref, buf_ref, sem_ref, k).start()

    @pl.when(s + ahead < n)
    def _():
        _ring_copy(src_ref, block_rows, idx_ref, buf_ref, sem_ref, s + ahead).start()


def _gather_kernel(pt_ref, pw_ref, n_ref, start_ref, h_any, row_ref, hs_ref, acc_ref, hbuf_ref, sem_ref):
    s = pl.program_id(0)
    n = n_ref[0]
    active, first, last = _pair_flags(s, n, pt_ref)
    _ring_start_ahead(h_any, MOE_WIN, pw_ref, hbuf_ref, sem_ref, s, n)

    @pl.when(active & first)
    def _():
        acc_ref[...] = jnp.zeros_like(acc_ref)

    @pl.when(active)
    def _():
        _ring_copy(h_any, MOE_WIN, pw_ref, hbuf_ref, sem_ref, s).wait()
        h_ref = hbuf_ref.at[s % MOE_RING]
        info = row_ref[pw_ref[s]]
        half = info.shape[1] // 2
        p1 = info[INFO_R1:INFO_R1 + 1] + _expert_offset(info[INFO_E1:INFO_E1 + 1], start_ref)
        p2 = info[INFO_R2:INFO_R2 + 1] + _expert_offset(info[INFO_E2:INFO_E2 + 1], start_ref)
        rows = (pt_ref[s] * MOE_TILE + lax.broadcasted_iota(jnp.int32, (MOE_TILE, half), 0)).astype(F32)
        part = None
        for c in range(2):
            sl = slice(c * half, (c + 1) * half)
            perm = jnp.where(rows == p1[:, sl], 1.0, jnp.where(rows == p2[:, sl], 1.0, 0.0)).astype(BF16)
            term = _dot(perm, h_ref[sl, :])
            part = term if part is None else part + term
        acc_ref[...] += part

    @pl.when(active & last)
    def _():
        hs_ref[...] = acc_ref[...].astype(hs_ref.dtype)


def _moe_ffn_kernel(te_ref, used_ref, hs_ref, wg_ref, wu_ref, wd_ref, ys_ref, *, tf):
    used = used_ref[pl.program_id(0)] > 0

    @pl.when(used)
    def _():
        h = hs_ref[...]
        y = None
        for f0 in range(0, wg_ref.shape[1], tf):
            a = _dot(h, wg_ref[:, f0:f0 + tf])
            u = _dot(h, wu_ref[:, f0:f0 + tf])
            act = (a * jax.nn.sigmoid(a)) * u
            t = _dot(act.astype(BF16), wd_ref[f0:f0 + tf, :])
            y = t if y is None else y + t
        ys_ref[...] = y.astype(ys_ref.dtype)

    @pl.when(jnp.logical_not(used))
    def _():
        ys_ref[...] = jnp.zeros_like(ys_ref)


def _combine_kernel(pw_ref, pt_ref, n_ref, start_ref, ys_any, col_ref, x_ref, g_ref, lng_ref, lnb_ref, o_ref,
                    acc_ref, pos_ref, ysbuf_ref, sem_ref, *, alpha):
    s = pl.program_id(0)
    n = n_ref[0]
    active, first, last = _pair_flags(s, n, pw_ref)
    _ring_start_ahead(ys_any, MOE_TILE, pt_ref, ysbuf_ref, sem_ref, s, n)

    @pl.when(active & first)
    def _():
        acc_ref[...] = jnp.zeros_like(acc_ref)
        info = col_ref[...]
        pos_ref[:, 0:1] = info[:, INFO_R1:INFO_R1 + 1] + _expert_offset(info[:, INFO_E1:INFO_E1 + 1], start_ref)
        pos_ref[:, 1:2] = info[:, INFO_R2:INFO_R2 + 1] + _expert_offset(info[:, INFO_E2:INFO_E2 + 1], start_ref)

    @pl.when(active)
    def _():
        win = col_ref.shape[0]
        p1 = pos_ref[:, 0:1]
        p2 = pos_ref[:, 1:2]
        row0 = (pt_ref[s] * MOE_TILE).astype(F32)
        in1 = (p1 >= row0) & (p1 < row0 + MOE_TILE)
        in2 = (p2 >= row0) & (p2 < row0 + MOE_TILE)
        weight = jnp.where(in1, col_ref[:, INFO_W1:INFO_W1 + 1], jnp.where(in2, col_ref[:, INFO_W2:INFO_W2 + 1], 0.0))
        cols = lax.broadcasted_iota(jnp.int32, (win, MOE_TILE), 1).astype(F32) + row0
        perm = jnp.where(cols == p1, 1.0, jnp.where(cols == p2, 1.0, 0.0)).astype(BF16)
        _ring_copy(ys_any, MOE_TILE, pt_ref, ysbuf_ref, sem_ref, s).wait()
        acc_ref[...] += weight * _dot(perm, ysbuf_ref[s % MOE_RING])

    @pl.when(active & last)
    def _():
        v = alpha * x_ref[...] + g_ref[...] * acc_ref[...]
        o_ref[...] = _ln(v) * lng_ref[...] + lnb_ref[...]


def _pair_list(mask, n_pairs):
    flat = mask.reshape(-1)
    n = jnp.sum(flat.astype(jnp.int32))
    idx = jnp.nonzero(flat, size=n_pairs, fill_value=0)[0].astype(jnp.int32)
    idx = jnp.where(jnp.arange(n_pairs) < n, idx, idx[jnp.maximum(n - 1, 0)])
    return idx // mask.shape[1], idx % mask.shape[1], n.reshape(1)


def _moe(x2, mod4, wr, wg, wu, wd, lng, lnb, seq, *, alpha, tf=1792):
    m, d = x2.shape
    n_exp, _, ff = wg.shape
    n_win = m // MOE_WIN
    per = seq // MOE_WIN
    n_rows = 2 * m + n_exp * MOE_FFN_TILE
    n_tiles = n_rows // MOE_TILE
    n_ffn = n_rows // MOE_FFN_TILE
    n_pairs = n_exp * n_win + n_tiles + 3 * n_exp

    h, col, row, cum, cnt = _router(x2, mod4, wr, seq)

    cum = cum[:, 0, :n_exp].astype(jnp.int32)
    cnt = cnt[:, 0, :n_exp].astype(jnp.int32)
    total = cum[-1] + cnt[-1]
    padded = (total + MOE_FFN_TILE - 1) // MOE_FFN_TILE * MOE_FFN_TILE
    end = jnp.cumsum(padded)
    start = (end - padded).astype(jnp.int32)

    def tile_expert(tile_rows):
        row0 = jnp.arange(n_rows // tile_rows, dtype=jnp.int32) * tile_rows
        te = jnp.minimum(jnp.sum((row0[:, None] >= end[None, :]).astype(jnp.int32), axis=1), n_exp - 1)
        return row0, te, row0 < end[-1]

    row0, te, used = tile_expert(MOE_TILE)
    local0 = row0 - start[te]
    lo = jnp.take(cum.T, te, axis=0)
    hi = lo + jnp.take(cnt.T, te, axis=0)
    meet = (lo < local0[:, None] + MOE_TILE) & (hi > local0[:, None]) & (hi > lo) & used[:, None]
    visit = meet.at[:, 0].set(meet[:, 0] | jnp.logical_not(jnp.any(meet, axis=1)))
    g_tile, g_win, g_n = _pair_list(visit, n_pairs)
    c_win, c_tile, c_n = _pair_list(meet.T, n_pairs)
    _, ffn_te, ffn_used = tile_expert(MOE_FFN_TILE)

    hs = pl.pallas_call(
        _gather_kernel,
        grid_spec=pltpu.PrefetchScalarGridSpec(
            num_scalar_prefetch=4,
            grid=(n_pairs,),
            in_specs=[pl.BlockSpec(memory_space=pl.ANY),
                      pl.BlockSpec((n_win, 8, MOE_WIN), lambda s, pt, pw, n, st: (0, 0, 0))],
            out_specs=pl.BlockSpec((MOE_TILE, d), lambda s, pt, pw, n, st: (pt[s], 0)),
            scratch_shapes=[pltpu.VMEM((MOE_TILE, d), F32), pltpu.VMEM((MOE_RING, MOE_WIN, d), BF16),
                            pltpu.SemaphoreType.DMA((MOE_RING,))]),
        out_shape=jax.ShapeDtypeStruct((n_rows, d), BF16),
        compiler_params=_params(("arbitrary",)),
        name="moe_gather",
    )(g_tile, g_win, g_n, start, h, row)

    expert = lambda shape: pl.BlockSpec((None,) + shape, lambda j, te, us: (te[j], 0, 0),
                                        pipeline_mode=pl.Buffered(1))
    ys = pl.pallas_call(
        functools.partial(_moe_ffn_kernel, tf=tf),
        grid_spec=pltpu.PrefetchScalarGridSpec(
            num_scalar_prefetch=2,
            grid=(n_ffn,),
            in_specs=[pl.BlockSpec((MOE_FFN_TILE, d), lambda j, te, us: (j, 0)),
                      expert((d, ff)), expert((d, ff)), expert((ff, d))],
            out_specs=pl.BlockSpec((MOE_FFN_TILE, d), lambda j, te, us: (j, 0))),
        out_shape=jax.ShapeDtypeStruct((n_rows, d), BF16),
        compiler_params=_params(("arbitrary",)),
        name="moe_experts",
    )(ffn_te, ffn_used.astype(jnp.int32), hs, wg, wu, wd)

    return pl.pallas_call(
        functools.partial(_combine_kernel, alpha=alpha),
        grid_spec=pltpu.PrefetchScalarGridSpec(
            num_scalar_prefetch=4,
            grid=(n_pairs,),
            in_specs=[pl.BlockSpec(memory_space=pl.ANY),
                      pl.BlockSpec((MOE_WIN, 128), lambda s, pw, pt, n, st: (pw[s], 0)),
                      pl.BlockSpec((MOE_WIN, d), lambda s, pw, pt, n, st: (pw[s], 0)),
                      pl.BlockSpec((None, None, 1, d), lambda s, pw, pt, n, st: (pw[s] // per, 5, 0, 0)),
                      pl.BlockSpec((1, d), lambda s, pw, pt, n, st: (0, 0)),
                      pl.BlockSpec((1, d), lambda s, pw, pt, n, st: (0, 0))],
            out_specs=pl.BlockSpec((MOE_WIN, d), lambda s, pw, pt, n, st: (pw[s], 0)),
            scratch_shapes=[pltpu.VMEM((MOE_WIN, d), F32), pltpu.VMEM((MOE_WIN, 128), F32),
                            pltpu.VMEM((MOE_RING, MOE_TILE, d), BF16), pltpu.SemaphoreType.DMA((MOE_RING,))]),
        out_shape=jax.ShapeDtypeStruct((m, d), F32),
        compiler_params=_params(("arbitrary",)),
        name="moe_combine",
    )(c_win, c_tile, c_n, start, ys, col, x2, mod4, lng.reshape(1, d), lnb.reshape(1, d))


def _in_columns():
    cols = np.full((N_COLS,), -1, np.int64)
    scale = np.ones((N_COLS,), np.float32)
    qs = LOG2E / math.sqrt(HEAD_DIM)

    def put(dst, src, width, s=1.0):
        cols[dst:dst + width] = np.arange(src, src + width)
        scale[dst:dst + width] = s

    put(COL_MG, IN_MG, COL_NSA_Q - COL_MG)
    put(COL_NSA_Q, IN_NSA_Q, 256, qs)
    put(COL_NSA_CMP, IN_K_CMP, 64)
    put(COL_NSA_CMP + 64, IN_V_CMP, 64)
    put(COL_NSA_SEL, IN_K_SEL, 64)
    put(COL_NSA_SEL + 64, IN_K_SEL, 64)
    put(COL_NSA_WIN, IN_K_WIN, 64)
    put(COL_NSA_WIN + 64, IN_K_WIN, 64)
    for j, h in enumerate(SWA_SLOT_HEADS):
        put(COL_SWA_Q + j * HEAD_DIM, IN_SWA_Q + h * HEAD_DIM, HEAD_DIM, qs)
    put(COL_SWA_K, IN_SWA_K, 128)
    put(COL_DIL_Q, IN_DIL_Q, 768, qs)
    put(COL_DIL_K, IN_DIL_K, 256)
    put(COL_SB_Q, IN_SB_Q, 256, qs)
    put(COL_SB_K, IN_SB_K, 256)
    return cols, scale


def _in_rows_t():
    rows = np.full((N_ROWS_T,), -1, np.int64)

    def put(dst, src, width):
        rows[dst:dst + width] = np.arange(src, src + width)

    put(ROW_NSA_V, IN_V_SEL, 64)
    put(ROW_NSA_V + 64, IN_V_WIN, 64)
    put(ROW_SWA_V, IN_SWA_V, 128)
    put(ROW_DIL_V, IN_DIL_V, 256)
    put(ROW_SB_V, IN_SB_V, 256)
    put(ROW_GATE, IN_GATE, GATE_COLS)
    return rows


def _select_columns(w, src):
    d = w.shape[0]
    parts = []
    i = 0
    while i < len(src):
        j = i + 1
        while j < len(src) and (src[j] == src[j - 1] + 1 if src[i] >= 0 else src[j] < 0):
            j += 1
        parts.append(w[:, src[i]:src[i] + (j - i)] if src[i] >= 0 else jnp.zeros((d, j - i), w.dtype))
        i = j
    return jnp.concatenate(parts, axis=1)


def _relayout_w_in(w_in):
    cols, scale = _in_columns()
    w = _select_columns(w_in, cols) * jnp.asarray(scale)
    wt = _select_columns(w_in, _in_rows_t())
    return w.astype(BF16), wt.T.astype(BF16)


def kernel(x, c, rel_bias, w_ada, b_ada, w_in, w_branch, w_out, cmp_pe, cmp_w1, cmp_w2, swa_sinks, ln_g, ln_b,
           ffn_w_gate, ffn_w_up, ffn_w_down, moe_router, moe_w_gate, moe_w_up, moe_w_down):
    bsz, seq, d = x.shape
    depth = w_ada.shape[0]
    alpha = (2 * depth) ** 0.25
    m = bsz * seq
    nq = seq // BLK

    nsa_heads = tuple(range(NSA_HEADS))
    bias_sel = _bias_tiles(rel_bias, _bucket_tiles(nq, seq, 1), nsa_heads)
    bias_win = _bias_tiles(rel_bias, _bucket_tiles(-(-(NSA_WINDOW - 1) // BLK) + 1, NSA_WINDOW - 1, 1), nsa_heads)
    bias_swa = _bias_tiles(rel_bias, _bucket_tiles(-(-(SWA_WINDOW - 1) // BLK) + 1, SWA_WINDOW - 1, 1),
                           tuple(NSA_HEADS + h for h in SWA_SLOT_HEADS))
    bias_dil = []
    for gi, (win, dil) in enumerate(DIL_PATTERNS):
        h0 = NSA_HEADS + SWA_HEADS + gi * DIL_HEADS_PER_GROUP
        bias_dil.append(_bias_tiles(rel_bias, _bucket_tiles(min(win // BLK + 1, nq), win, dil),
                                    tuple(range(h0, h0 + DIL_HEADS_PER_GROUP))))
    swa_rows = np.concatenate([np.arange(h * HEAD_DIM, (h + 1) * HEAD_DIM) for h in SWA_SLOT_HEADS])

    x2 = x.reshape(m, d)
    for layer in range(depth):
        mod4 = _ada(c, w_ada[layer], b_ada[layer]).reshape(bsz, 6, 1, d)
        w_nat, w_t = _relayout_w_in(w_in[layer])
        z2, zt = _inproj(x2, mod4, w_nat, w_t, seq, sc_idx=1, sh_idx=0)
        zt = zt.reshape(bsz, nq, N_ROWS_T, BLK)
        z = z2.reshape(bsz, seq, N_COLS)
        chunks = z[:, :, COL_NSA_CMP:COL_NSA_CMP + 2 * HEAD_DIM].reshape(bsz, seq // CMP_STRIDE,
                                                                        CMP_STRIDE * 2 * HEAD_DIM)
        kc, vct = _compress(chunks, cmp_pe[layer], cmp_w1[layer], cmp_w2[layer])
        o_a = _nsa(z, zt, kc, vct, bias_sel, bias_win)
        o_b = _swa(z, zt, bias_swa, swa_sinks[layer])
        o_c = _dilated(z, zt, bias_dil)
        o_d = _stick_breaking(z, zt)
        wb = w_branch[layer]
        wb = jnp.stack([wb[0], wb[1][swa_rows], wb[2], wb[3]]).astype(BF16)
        branches = [a.reshape(m, BRANCH_WIDTH) for a in (o_a, o_b, o_c, o_d)]
        x2 = _merge(branches, z2, wb, w_out[layer].astype(BF16), x2, mod4, ln_g[layer, 0], ln_b[layer, 0], seq,
                    alpha=alpha, g_idx=2)
        i = layer // 2
        if layer % 2 == 0:
            x2 = _ffn(x2, mod4, ffn_w_gate[i].astype(BF16), ffn_w_up[i].astype(BF16), ffn_w_down[i].astype(BF16),
                      ln_g[layer, 1], ln_b[layer, 1], seq, alpha=alpha)
        else:
            x2 = _moe(x2, mod4, moe_router[i], moe_w_gate[i].astype(BF16), moe_w_up[i].astype(BF16),
                      moe_w_down[i].astype(BF16), ln_g[layer, 1], ln_b[layer, 1], seq, alpha=alpha)
    return x2.reshape(bsz, seq, d)
```
